```python
import math
import jax
import jax.numpy as jnp
from jax import lax

D_MODEL = 4096
BATCH = 4
SEQ = 2048
DEPTH = 4
DEC_BATCH = 8
DEC_SEQ = 8
PAST_LEN = 8192
PAGE_SIZE = 128

ALPHA = (2 * DEPTH) ** 0.25
BETA_INIT = (8 * DEPTH) ** -0.25
LN_EPS = 1e-5

RW_HD = 64
RW_W = D_MODEL // 4
RW_HEADS = RW_W // RW_HD
RW_DECAY_R = 64
RW_AAA_R = 64
RW_GATE_R = 160
RW_COLS = 3 * RW_W + RW_DECAY_R + RW_AAA_R + RW_GATE_R
RW_GN_EPS = 64e-5

GD_HD = 128
GD_W = D_MODEL // 4
GD_HEADS = GD_W // GD_HD
GD_QKV = 3 * GD_W
GD_CONV = 4
GD_CHUNK = 64
GD_COLS = GD_QKV + GD_W + 2 * GD_HEADS

NS_HD = 128
NS_W = D_MODEL // 2
NS_HEADS = NS_W // NS_HD
NS_KV = 4
NS_GROUP = NS_HEADS // NS_KV
NS_KVW = NS_KV * NS_HD
NS_COLS = NS_W + 6 * NS_KVW + 3 * NS_HEADS
CMP_LEN = 32
CMP_STRIDE = 16
CMP_HIDDEN = 128
SEL_BLOCK = 64
SEL_TOPK = 16
WINDOW = 512
NS_QBLOCK = 32
FORCE_SCORE = 1e4

REL_BUCKETS = 32
REL_MAX_DIST = 1024

D_MIX = RW_W + GD_W + NS_W
IN_COLS = RW_COLS + GD_COLS + NS_COLS

PEER_HEADS = 8
PEER_NKEYS = 128
PEER_EXPERTS = PEER_NKEYS ** 2
PEER_DKEY = 256
PEER_TOPK = 16
PEER_TBLOCK = 128

kernel_name = 'hymba_rwkv7_gdn_nsa_peer_step'


def layer_norm(x, g, b):
    xf = x.astype(jnp.float32)
    mu = jnp.mean(xf, -1, keepdims=True)
    var = jnp.mean(jnp.square(xf - mu), -1, keepdims=True)
    return ((xf - mu) * lax.rsqrt(var + LN_EPS) * g + b).astype(x.dtype)


def l2_normalize(x):
    return x / jnp.maximum(jnp.sqrt(jnp.sum(x * x, -1, keepdims=True)), 1e-12)


def masked_softmax(s, mask):
    s = jnp.where(mask, s, -jnp.inf)
    m = jnp.max(s, -1, keepdims=True)
    m = jnp.where(jnp.isfinite(m), m, 0.0)
    e = jnp.where(mask, jnp.exp(s - m), 0.0)
    return e / jnp.maximum(jnp.sum(e, -1, keepdims=True), 1e-30)


def rel_bucket(d):
    d = jnp.maximum(d, 0)
    exact = REL_BUCKETS // 2
    logd = jnp.log(jnp.maximum(d, 1).astype(jnp.float32) / exact) / math.log(REL_MAX_DIST / exact)
    large = jnp.minimum(exact + (logd * (REL_BUCKETS - exact)).astype(jnp.int32), REL_BUCKETS - 1)
    return jnp.where(d < exact, d, large)


def rwkv7_mixer(p, shift0, s0, mu, w0, w_up, a0, a_up, g_up, k_k, k_a, r_k, ln_g, ln_b):
    f32 = jnp.float32
    B, T, _ = p.shape
    prev = jnp.concatenate([shift0[:, None].astype(p.dtype), p[:, :-1]], axis=1)
    m = p + mu * (prev - p)
    r = m[..., :RW_W]
    k = m[..., RW_W:2 * RW_W]
    v = m[..., 2 * RW_W:3 * RW_W]
    o = 3 * RW_W
    wl = m[..., o:o + RW_DECAY_R]
    o += RW_DECAY_R
    al = m[..., o:o + RW_AAA_R]
    o += RW_AAA_R
    gl = m[..., o:o + RW_GATE_R]
    w = -jax.nn.softplus(-(w0 + jnp.tanh(wl) @ w_up).astype(f32)) - 0.5
    decay = jnp.exp(-jnp.exp(w))
    a = jax.nn.sigmoid((a0 + al @ a_up).astype(f32))
    g = jax.nn.sigmoid(gl) @ g_up

    def heads(t):
        return t.reshape(B, T, RW_HEADS, RW_HD).astype(f32)

    kk = l2_normalize(heads(k * k_k))
    k = k * (1.0 + (a - 1.0) * k_a)
    r_, k_, v_, w_, a_ = heads(r), heads(k), heads(v), heads(decay), heads(a)

    def step(S, inp):
        rt, wt, kt, vt, kkt, at = inp
        sa = jnp.einsum('bhvk,bhk->bhv', S, -kkt)
        S = S * wt[:, :, None, :] + sa[..., None] * (kkt * at)[:, :, None, :] + vt[..., None] * kt[:, :, None, :]
        return S, jnp.einsum('bhvk,bhk->bhv', S, rt)

    xs = tuple(jnp.moveaxis(t, 1, 0) for t in (r_, w_, k_, v_, kk, a_))
    sT, y = lax.scan(step, s0.astype(f32), xs)
    y = jnp.moveaxis(y, 0, 1)
    ym = jnp.mean(y, -1, keepdims=True)
    yv = jnp.mean(jnp.square(y - ym), -1, keepdims=True)
    y = ((y - ym) * lax.rsqrt(yv + RW_GN_EPS)).reshape(B, T, RW_W) * ln_g + ln_b
    bonus = jnp.sum(r_ * k_ * r_k, -1, keepdims=True) * v_
    y = (y + bonus.reshape(B, T, RW_W)) * g
    return y.astype(p.dtype), p[:, -1], sT


def gated_delta_chunked(q, k, v, beta, g, s0):
    B, T, H, DK = q.shape
    DV = v.shape[-1]
    C = GD_CHUNK
    n = -(-T // C)
    pad = n * C - T

    def chunks(a):
        a = jnp.pad(a, [(0, 0), (0, pad)] + [(0, 0)] * (a.ndim - 2))
        a = a.reshape((B, n, C) + a.shape[2:])
        return jnp.moveaxis(jnp.moveaxis(a, 1, 0), 3, 2)

    qc, kc, vc, bc, gc = (chunks(t) for t in (q, k, v, beta, g))
    G = jnp.cumsum(gc, -1)
    tri = jnp.tril(jnp.ones((C, C), bool))
    strict = jnp.tril(jnp.ones((C, C), bool), -1)
    diff = G[..., :, None] - G[..., None, :]
    decay = jnp.where(tri, jnp.exp(jnp.where(tri, diff, 0.0)), 0.0)
    kb = kc * bc[..., None]
    Lm = jnp.where(strict, jnp.einsum('...id,...jd->...ij', kb, kc) * decay, 0.0)
    eye = jnp.eye(C, dtype=Lm.dtype)
    Tinv = lax.linalg.triangular_solve(eye + Lm, jnp.broadcast_to(eye, Lm.shape), left_side=True, lower=True, unit_diagonal=True)
    U = Tinv @ (vc * bc[..., None])
    W = Tinv @ (kb * jnp.exp(G)[..., None])
    Aqk = jnp.einsum('...id,...jd->...ij', qc, kc) * decay
    Qg = qc * jnp.exp(G)[..., None]
    Glast = G[..., -1:]
    Kd = kc * jnp.exp(Glast - G)[..., None]
    aG = jnp.exp(Glast)[..., None]

    def step(S, inp):
        U_, W_, Q_, A_, K_, a_ = inp
        vn = U_ - W_ @ S
        out = Q_ @ S + A_ @ vn
        S = S * a_ + jnp.swapaxes(K_, -1, -2) @ vn
        return S, out

    sT, o = lax.scan(step, s0, (U, W, Qg, Aqk, Kd, aG))
    o = jnp.moveaxis(jnp.moveaxis(o, 0, 1), 2, 3).reshape(B, n * C, H, DV)[:, :T]
    return o, sT


def gated_deltanet_mixer(p, conv0, s0, conv_w, a_log, dt_bias, norm_g):
    f32 = jnp.float32
    B, T, _ = p.shape
    qkv = p[..., :GD_QKV]
    z = p[..., GD_QKV:GD_QKV + GD_W]
    bl = p[..., GD_QKV + GD_W:GD_QKV + GD_W + GD_HEADS]
    al = p[..., GD_QKV + GD_W + GD_HEADS:]
    xc = jnp.concatenate([conv0.astype(p.dtype), qkv], axis=1)
    conv = xc[:, :T] * conv_w[0]
    for i in range(1, GD_CONV):
        conv = conv + xc[:, i:i + T] * conv_w[i]
    conv = jax.nn.silu(conv).astype(f32)
    q = l2_normalize(conv[..., :GD_W].reshape(B, T, GD_HEADS, GD_HD)) * GD_HD ** -0.5
    k = l2_normalize(conv[..., GD_W:2 * GD_W].reshape(B, T, GD_HEADS, GD_HD))
    v = conv[..., 2 * GD_W:].reshape(B, T, GD_HEADS, GD_HD)
    beta = jax.nn.sigmoid(bl.astype(f32))
    g = -jnp.exp(a_log.astype(f32)) * jax.nn.softplus((al + dt_bias).astype(f32))
    o, sT = gated_delta_chunked(q, k, v, beta, g, s0.astype(f32))
    o = o * lax.rsqrt(jnp.mean(o * o, -1, keepdims=True) + 1e-6) * norm_g
    o = o.reshape(B, T, GD_W) * jax.nn.silu(z.astype(f32))
    return o.astype(p.dtype), xc[:, -(GD_CONV - 1):], sT


def nsa_project(p):
    B, T, _ = p.shape
    q = p[..., :NS_W].reshape(B, T, NS_HEADS, NS_HD)
    kv = p[..., NS_W:NS_W + 6 * NS_KVW].reshape(B, T, 6, NS_KV, NS_HD)
    gates = jax.nn.sigmoid(p[..., NS_W + 6 * NS_KVW:].astype(jnp.float32)).reshape(B, T, 3, NS_HEADS)
    return q, kv, gates


def nsa_compress(x, w1, w2):
    B, L = x.shape[:2]
    r = CMP_LEN // CMP_STRIDE
    n_sub = L // CMP_STRIDE
    n_cmp = n_sub - r + 1
    sub = x[:, :n_sub * CMP_STRIDE].reshape(B, n_sub, CMP_STRIDE, NS_KV, NS_HD)
    blk = jnp.concatenate([sub[:, i:i + n_cmp] for i in range(r)], axis=2)
    flat = jnp.moveaxis(blk, 3, 2).reshape(B, n_cmp, NS_KV, CMP_LEN * NS_HD)
    return jax.nn.gelu(flat @ w1, approximate=False) @ w2


def nsa_sel_blocks(x):
    B, L = x.shape[:2]
    ns = -(-L // SEL_BLOCK)
    x = jnp.pad(x, ((0, 0), (0, ns * SEL_BLOCK - L), (0, 0), (0, 0)))
    return jnp.transpose(x.reshape(B, ns, SEL_BLOCK, NS_KV, NS_HD), (0, 3, 1, 2, 4))


def nsa_attend(q, t, gates, kcmp, vcmp, kb, vb, kwin, vwin, pwin, rel_bias):
    f32 = jnp.float32
    B, Tq = q.shape[:2]
    qg = jnp.transpose(q.reshape(B, Tq, NS_KV, NS_GROUP, NS_HD), (0, 2, 3, 1, 4)).astype(f32) * NS_HD ** -0.5
    bias_tab = rel_bias.astype(f32).reshape(REL_BUCKETS, NS_KV, NS_GROUP)

    def shared_bias(d):
        return jnp.transpose(bias_tab[rel_bucket(d)], (2, 3, 0, 1))

    nc = kcmp.shape[1]
    ci = jnp.arange(nc)
    dc = t[:, None] - (ci * CMP_STRIDE + CMP_LEN - 1)[None, :]
    sc = jnp.einsum('bkgqd,bnkd->bkgqn', qg, kcmp.astype(f32)) + shared_bias(dc)
    pc = masked_softmax(sc, dc >= 0)
    o_cmp = jnp.einsum('bkgqn,bnkd->bkgqd', pc, vcmp.astype(f32))
    ns = kb.shape[2]
    j = jnp.arange(ns)
    overlap = ((ci[:, None] * CMP_STRIDE < (j[None, :] + 1) * SEL_BLOCK) & (ci[:, None] * CMP_STRIDE + CMP_LEN > j[None, :] * SEL_BLOCK)).astype(f32)
    ps = jnp.einsum('bkgqn,nj->bkqj', pc, overlap)
    cur = t // SEL_BLOCK
    valid = j[None, :] <= cur[:, None]
    forced = (j[None, :] == 0) | (j[None, :] == cur[:, None]) | (j[None, :] == cur[:, None] - 1)
    score = jnp.where(valid, jnp.where(forced, FORCE_SCORE, ps), -jnp.inf)
    nsel = min(SEL_TOPK, ns)
    top_s, top_i = lax.top_k(score, nsel)
    bi = jnp.arange(B)[:, None, None, None]
    ki = jnp.arange(NS_KV)[None, :, None, None]
    kg = kb[bi, ki, top_i].astype(f32)
    vg = vb[bi, ki, top_i].astype(f32)
    pos = top_i[..., None] * SEL_BLOCK + jnp.arange(SEL_BLOCK)
    ds = t[None, None, :, None, None] - pos
    ms = jnp.isfinite(top_s)[..., None] & (ds >= 0)
    sb = jnp.moveaxis(bias_tab[rel_bucket(ds), ki[..., None]], -1, 2)
    ss = jnp.einsum('bkgqd,bkqnld->bkgqnl', qg, kg) + sb
    m_all = nsel * SEL_BLOCK
    psl = masked_softmax(ss.reshape(B, NS_KV, NS_GROUP, Tq, m_all), ms[:, :, None].reshape(B, NS_KV, 1, Tq, m_all))
    o_slc = jnp.einsum('bkgqm,bkqmd->bkgqd', psl, vg.reshape(B, NS_KV, Tq, m_all, NS_HD))
    dw = t[:, None] - pwin[None, :]
    mw = (dw >= 0) & (dw <= WINDOW) & (pwin[None, :] >= 0)
    sw = jnp.einsum('bkgqd,blkd->bkgql', qg, kwin.astype(f32)) + shared_bias(dw)
    pw = masked_softmax(sw, mw)
    o_win = jnp.einsum('bkgql,blkd->bkgqd', pw, vwin.astype(f32))
    gt = jnp.moveaxis(gates, 1, -1).reshape(B, 3, NS_KV, NS_GROUP, Tq)[..., None]
    o = gt[:, 0] * o_cmp + gt[:, 1] * o_slc + gt[:, 2] * o_win
    return jnp.transpose(o, (0, 3, 1, 2, 4)).reshape(B, Tq, NS_W).astype(q.dtype)


def nsa_prompt(p, phi_k1, phi_k2, phi_v1, phi_v2, rel_bias):
    q, kv, gates = nsa_project(p)
    B, T = q.shape[:2]
    kcmp = nsa_compress(kv[:, :, 0], phi_k1, phi_k2)
    vcmp = nsa_compress(kv[:, :, 1], phi_v1, phi_v2)
    kb = nsa_sel_blocks(kv[:, :, 2])
    vb = nsa_sel_blocks(kv[:, :, 3])
    pad_w = ((0, 0), (WINDOW, 0), (0, 0), (0, 0))
    kw = jnp.pad(kv[:, :, 4], pad_w)
    vw = jnp.pad(kv[:, :, 5], pad_w)
    qb = NS_QBLOCK if T % NS_QBLOCK == 0 else T

    def block(i):
        s = i * qb
        t = s + jnp.arange(qb)
        pw = s - WINDOW + jnp.arange(WINDOW + qb)
        return nsa_attend(lax.dynamic_slice_in_dim(q, s, qb, 1), t, lax.dynamic_slice_in_dim(gates, s, qb, 1), kcmp, vcmp, kb, vb, lax.dynamic_slice_in_dim(kw, s, WINDOW + qb, 1), lax.dynamic_slice_in_dim(vw, s, WINDOW + qb, 1), pw, rel_bias)

    o = lax.map(block, jnp.arange(T // qb))
    o = jnp.moveaxis(o, 0, 1).reshape(B, T, NS_W)
    wl = min(WINDOW, T)
    return o, kv[:, :, :4], kv[:, T - wl:, 4:]


def nsa_sample(p, cache_kv_l, page_table, win_buf, phi_k1, phi_k2, phi_v1, phi_v2, rel_bias):
    q, kv, gates = nsa_project(p)
    DB, Tn = q.shape[:2]
    past = cache_kv_l[page_table]
    past = past.reshape(DB, -1, 4, NS_KV, NS_HD)
    P = past.shape[1]
    ctx = jnp.concatenate([past, kv[:, :, :4].astype(past.dtype)], axis=1)
    kcmp = nsa_compress(ctx[:, :, 0], phi_k1, phi_k2)
    vcmp = nsa_compress(ctx[:, :, 1], phi_v1, phi_v2)
    kb = nsa_sel_blocks(ctx[:, :, 2])
    vb = nsa_sel_blocks(ctx[:, :, 3])
    Wb = win_buf.shape[1]
    win = jnp.concatenate([win_buf, kv[:, :, 4:].astype(win_buf.dtype)], axis=1)
    t = P + jnp.arange(Tn)
    pw = P - Wb + jnp.arange(Wb + Tn)
    o = nsa_attend(q, t, gates, kcmp, vcmp, kb, vb, win[:, :, 0], win[:, :, 1], pw, rel_bias)
    return o, kv[:, :, :4], win[:, Tn:]


def peer_ffn(x, wq, k1, k2, u_tab, v_tab):
    f32 = jnp.float32
    Bx, T, D = x.shape
    xt = x.reshape(-1, D)
    n = xt.shape[0]
    q = (xt @ wq).astype(f32).reshape(n, PEER_HEADS, 2, PEER_DKEY // 2)
    s1 = jnp.einsum('nhd,kd->nhk', q[:, :, 0], k1.astype(f32))
    s2 = jnp.einsum('nhd,kd->nhk', q[:, :, 1], k2.astype(f32))
    v1, i1 = lax.top_k(s1, PEER_TOPK)
    v2, i2 = lax.top_k(s2, PEER_TOPK)
    cand = (v1[..., :, None] + v2[..., None, :]).reshape(n, PEER_HEADS, PEER_TOPK * PEER_TOPK)
    cidx = (i1[..., :, None] * PEER_NKEYS + i2[..., None, :]).reshape(n, PEER_HEADS, PEER_TOPK * PEER_TOPK)
    sv, sp = lax.top_k(cand, PEER_TOPK)
    eidx = jnp.take_along_axis(cidx, sp, axis=-1)
    gw = jax.nn.softmax(sv, axis=-1)
    nb = -(-n // PEER_TBLOCK)
    pad = nb * PEER_TBLOCK - n
    xb = jnp.pad(xt, ((0, pad), (0, 0))).reshape(nb, PEER_TBLOCK, D)
    eb = jnp.pad(eidx.reshape(n, -1), ((0, pad), (0, 0))).reshape(nb, PEER_TBLOCK, -1)
    gb = jnp.pad(gw.reshape(n, -1), ((0, pad), (0, 0))).reshape(nb, PEER_TBLOCK, -1)

    def blk(args):
        xi, ei, gi = args
        u = jnp.take(u_tab, ei, axis=0)
        h = jax.nn.gelu(jnp.einsum('td,ted->te', xi, u).astype(f32), approximate=False)
        vv = jnp.take(v_tab, ei, axis=0)
        return jnp.einsum('te,ted->td', (gi * h).astype(vv.dtype), vv)

    out = lax.map(blk, (xb, eb, gb)).reshape(-1, D)[:n]
    return out.reshape(Bx, T, D).astype(x.dtype)


def residual_block(x, mix, w_out, ln1_g, ln1_b, ln2_g, ln2_b, peer_wq, peer_k1, peer_k2, peer_u, peer_v):
    x = layer_norm(ALPHA * x + mix @ w_out, ln1_g, ln1_b)
    return layer_norm(ALPHA * x + peer_ffn(x, peer_wq, peer_k1, peer_k2, peer_u, peer_v), ln2_g, ln2_b)


def setup_inputs(seed: int = 0) -> dict:
    key = jax.random.key(seed)
    ks = iter(jax.random.split(key, 48))

    def nrm(shape, scale):
        return jax.random.normal(next(ks), shape, jnp.float32) * scale

    def unif(shape, lo, hi):
        return jax.random.uniform(next(ks), shape, jnp.float32, lo, hi)

    n_pages = PAST_LEN // PAGE_SIZE
    n_phys = (5 * DEC_BATCH * n_pages + 3) // 4
    win_buf = min(WINDOW, PAST_LEN)
    x_prompt = nrm((BATCH, SEQ, D_MODEL), 1.0)
    x_sample = nrm((DEC_BATCH, DEC_SEQ, D_MODEL), 1.0)
    cache_kv = nrm((DEPTH, n_phys, PAGE_SIZE, 4, NS_KV, NS_HD), 1.0)
    cache_win = nrm((DEPTH, DEC_BATCH, win_buf, 2, NS_KV, NS_HD), 1.0)
    state_rwkv = nrm((DEPTH, DEC_BATCH, RW_HEADS, RW_HD, RW_HD), 0.3)
    state_rwkv_shift = nrm((DEPTH, DEC_BATCH, RW_COLS), 1.0)
    state_gdn = nrm((DEPTH, DEC_BATCH, GD_HEADS, GD_HD, GD_HD), 0.1)
    state_gdn_conv = nrm((DEPTH, DEC_BATCH, GD_CONV - 1, GD_QKV), 1.0)
    perm = jax.random.permutation(next(ks), n_phys)
    page_table = perm[:DEC_BATCH * n_pages].reshape(DEC_BATCH, n_pages).astype(jnp.int32)
    dt = jnp.exp(unif((DEPTH, GD_HEADS), math.log(1e-3), math.log(1e-1)))
    return {
        'x_prompt': x_prompt,
        'x_sample': x_sample,
        'cache_kv': cache_kv,
        'cache_win': cache_win,
        'state_rwkv': state_rwkv,
        'state_rwkv_shift': state_rwkv_shift,
        'state_gdn': state_gdn,
        'state_gdn_conv': state_gdn_conv,
        'page_table': page_table,
        'w_in': nrm((DEPTH, D_MODEL, IN_COLS), D_MODEL ** -0.5),
        'w_out': nrm((DEPTH, D_MIX, D_MODEL), BETA_INIT * D_MIX ** -0.5),
        'ln1_g': 1.0 + nrm((DEPTH, D_MODEL), 0.02),
        'ln1_b': nrm((DEPTH, D_MODEL), 0.02),
        'ln2_g': 1.0 + nrm((DEPTH, D_MODEL), 0.02),
        'ln2_b': nrm((DEPTH, D_MODEL), 0.02),
        'rw_mu': unif((DEPTH, RW_COLS), 0.0, 1.0),
        'rw_w0': unif((DEPTH, RW_W), -6.0, -1.0),
        'rw_w_up': nrm((DEPTH, RW_DECAY_R, RW_W), 0.1 * RW_DECAY_R ** -0.5),
        'rw_a0': nrm((DEPTH, RW_W), 0.1),
        'rw_a_up': nrm((DEPTH, RW_AAA_R, RW_W), 0.1 * RW_AAA_R ** -0.5),
        'rw_g_up': nrm((DEPTH, RW_GATE_R, RW_W), RW_GATE_R ** -0.5),
        'rw_k_k': 0.85 + nrm((DEPTH, RW_W), 0.02),
        'rw_k_a': 1.0 + nrm((DEPTH, RW_W), 0.02),
        'rw_r_k': nrm((DEPTH, RW_HEADS, RW_HD), 0.1),
        'rw_ln_g': 1.0 + nrm((DEPTH, RW_W), 0.02),
        'rw_ln_b': nrm((DEPTH, RW_W), 0.02),
        'gd_conv_w': nrm((DEPTH, GD_CONV, GD_QKV), GD_CONV ** -0.5),
        'gd_a_log': jnp.log(unif((DEPTH, GD_HEADS), 1.0, 16.0)),
        'gd_dt_bias': dt + jnp.log(-jnp.expm1(-dt)),
        'gd_norm_g': 1.0 + nrm((DEPTH, GD_HD), 0.02),
        'ns_phi_k1': nrm((DEPTH, CMP_LEN * NS_HD, CMP_HIDDEN), (CMP_LEN * NS_HD) ** -0.5),
        'ns_phi_k2': nrm((DEPTH, CMP_HIDDEN, NS_HD), CMP_HIDDEN ** -0.5),
        'ns_phi_v1': nrm((DEPTH, CMP_LEN * NS_HD, CMP_HIDDEN), (CMP_LEN * NS_HD) ** -0.5),
        'ns_phi_v2': nrm((DEPTH, CMP_HIDDEN, NS_HD), CMP_HIDDEN ** -0.5),
        'rel_bias': nrm((REL_BUCKETS, NS_HEADS), 0.5),
        'peer_wq': nrm((DEPTH, D_MODEL, PEER_HEADS * PEER_DKEY), D_MODEL ** -0.5),
        'peer_k1': nrm((DEPTH, PEER_NKEYS, PEER_DKEY // 2), (PEER_DKEY // 2) ** -0.5),
        'peer_k2': nrm((DEPTH, PEER_NKEYS, PEER_DKEY // 2), (PEER_DKEY // 2) ** -0.5),
        'peer_u': nrm((DEPTH, PEER_EXPERTS, D_MODEL), BETA_INIT * D_MODEL ** -0.5),
        'peer_v': nrm((DEPTH, PEER_EXPERTS, D_MODEL), BETA_INIT),
    }


def reference(x_prompt, x_sample, cache_kv, cache_win, state_rwkv, state_rwkv_shift, state_gdn, state_gdn_conv, page_table, w_in, w_out, ln1_g, ln1_b, ln2_g, ln2_b, rw_mu, rw_w0, rw_w_up, rw_a0, rw_a_up, rw_g_up, rw_k_k, rw_k_a, rw_r_k, rw_ln_g, rw_ln_b, gd_conv_w, gd_a_log, gd_dt_bias, gd_norm_g, ns_phi_k1, ns_phi_k2, ns_phi_v1, ns_phi_v2, rel_bias, peer_wq, peer_k1, peer_k2, peer_u, peer_v):
    xp, xs = x_prompt, x_sample
    B = xp.shape[0]
    o_b = RW_COLS
    o_c = RW_COLS + GD_COLS
    kv_p, kv_s, win_p, win_s, rw_p, rw_s, sh_p, sh_s, gd_p, gd_s, cv_p, cv_s = ([] for _ in range(12))
    for l in range(DEPTH):
        rw = (rw_mu[l], rw_w0[l], rw_w_up[l], rw_a0[l], rw_a_up[l], rw_g_up[l], rw_k_k[l], rw_k_a[l], rw_r_k[l], rw_ln_g[l], rw_ln_b[l])
        gd = (gd_conv_w[l], gd_a_log[l], gd_dt_bias[l], gd_norm_g[l])
        phi = (ns_phi_k1[l], ns_phi_k2[l], ns_phi_v1[l], ns_phi_v2[l])
        tail = (w_out[l], ln1_g[l], ln1_b[l], ln2_g[l], ln2_b[l], peer_wq[l], peer_k1[l], peer_k2[l], peer_u[l], peer_v[l])
        pp = xp @ w_in[l]
        a, sh, rs = rwkv7_mixer(pp[..., :o_b], jnp.zeros((B, RW_COLS), pp.dtype), jnp.zeros((B, RW_HEADS, RW_HD, RW_HD), jnp.float32), *rw)
        b, cv, gs = gated_deltanet_mixer(pp[..., o_b:o_c], jnp.zeros((B, GD_CONV - 1, GD_QKV), pp.dtype), jnp.zeros((B, GD_HEADS, GD_HD, GD_HD), jnp.float32), *gd)
        c, kvr, wr = nsa_prompt(pp[..., o_c:], *phi, rel_bias)
        xp = residual_block(xp, jnp.concatenate([a, b, c.astype(a.dtype)], -1), *tail)
        kv_p.append(kvr)
        win_p.append(wr)
        rw_p.append(rs)
        sh_p.append(sh)
        gd_p.append(gs)
        cv_p.append(cv)
        ps = xs @ w_in[l]
        a, sh, rs = rwkv7_mixer(ps[..., :o_b], state_rwkv_shift[l], state_rwkv[l], *rw)
        b, cv, gs = gated_deltanet_mixer(ps[..., o_b:o_c], state_gdn_conv[l], state_gdn[l], *gd)
        c, kvr, wr = nsa_sample(ps[..., o_c:], cache_kv[l], page_table, cache_win[l], *phi, rel_bias)
        xs = residual_block(xs, jnp.concatenate([a, b, c.astype(a.dtype)], -1), *tail)
        kv_s.append(kvr)
        win_s.append(wr)
        rw_s.append(rs)
        sh_s.append(sh)
        gd_s.append(gs)
        cv_s.append(cv)
    st = jnp.stack
    return (xp, xs, st(kv_p), st(kv_s), st(win_p), st(win_s), st(rw_p), st(rw_s), st(sh_p), st(sh_s), st(gd_p), st(gd_s), st(cv_p), st(cv_s))
```

```python
import functools
import math

import jax
import jax.numpy as jnp
from jax import lax
from jax.experimental import pallas as pl
from jax.experimental.pallas import tpu as pltpu

D_MODEL = 4096
BATCH = 4
SEQ = 2048
DEPTH = 4
DEC_BATCH = 8
DEC_SEQ = 8
PAST_LEN = 8192
PAGE_SIZE = 128

ALPHA = (2 * DEPTH) ** 0.25
LN_EPS = 1e-5

RW_HD = 64
RW_W = D_MODEL // 4
RW_HEADS = RW_W // RW_HD
RW_DECAY_R = 64
RW_AAA_R = 64
RW_GATE_R = 160
RW_COLS = 3 * RW_W + RW_DECAY_R + RW_AAA_R + RW_GATE_R
RW_GN_EPS = 64e-5

GD_HD = 128
GD_W = D_MODEL // 4
GD_HEADS = GD_W // GD_HD
GD_QKV = 3 * GD_W
GD_CONV = 4
GD_CHUNK = 64
GD_COLS = GD_QKV + GD_W + 2 * GD_HEADS

NS_HD = 128
NS_W = D_MODEL // 2
NS_HEADS = NS_W // NS_HD
NS_KV = 4
NS_GROUP = NS_HEADS // NS_KV
NS_KVW = NS_KV * NS_HD
NS_COLS = NS_W + 6 * NS_KVW + 3 * NS_HEADS
CMP_LEN = 32
CMP_STRIDE = 16
CMP_HIDDEN = 128
SEL_BLOCK = 64
SEL_TOPK = 16
WINDOW = 512
NS_QBLOCK = 32
FORCE_SCORE = 1e4

REL_BUCKETS = 32
REL_MAX_DIST = 1024

D_MIX = RW_W + GD_W + NS_W
IN_COLS = RW_COLS + GD_COLS + NS_COLS

PEER_HEADS = 8
PEER_NKEYS = 128
PEER_EXPERTS = PEER_NKEYS ** 2
PEER_DKEY = 256
PEER_TOPK = 16
PEER_TBLOCK = 128

V7X_VMEM_LIMIT_BYTES = 56 * 1024 * 1024


def _matmul_body(x_ref, w_ref, o_ref):
    o_ref[...] = jnp.dot(x_ref[...].astype(jnp.bfloat16), w_ref[...].astype(jnp.bfloat16),
                         preferred_element_type=jnp.float32)


def _pick_tile(n, target):
    t = min(n, target)
    while n % t:
        t //= 2
    return t


def matmul(x, w, tm=512, tn=512):
    M, K = x.shape
    N = w.shape[1]
    tm = _pick_tile(M, tm)
    tn = min(tn, N)
    return pl.pallas_call(
        _matmul_body,
        grid=(M // tm, pl.cdiv(N, tn)),
        in_specs=[pl.BlockSpec((tm, K), lambda i, j: (i, 0)),
                  pl.BlockSpec((K, tn), lambda i, j: (0, j))],
        out_specs=pl.BlockSpec((tm, tn), lambda i, j: (i, j)),
        out_shape=jax.ShapeDtypeStruct((M, N), jnp.float32),
        compiler_params=pltpu.CompilerParams(
            dimension_semantics=("parallel", "parallel"),
            vmem_limit_bytes=V7X_VMEM_LIMIT_BYTES),
        name="proj_matmul",
    )(x, w)


def matmul3(x, w):
    B, T, K = x.shape
    return matmul(x.reshape(B * T, K), w).reshape(B, T, -1)


def layer_norm(x, g, b):
    xf = x.astype(jnp.float32)
    mu = jnp.mean(xf, -1, keepdims=True)
    var = jnp.mean(jnp.square(xf - mu), -1, keepdims=True)
    return ((xf - mu) * lax.rsqrt(var + LN_EPS) * g + b).astype(x.dtype)


def l2_normalize(x):
    return x / jnp.maximum(jnp.sqrt(jnp.sum(x * x, -1, keepdims=True)), 1e-12)


def masked_softmax(s, mask):
    s = jnp.where(mask, s, -jnp.inf)
    m = jnp.max(s, -1, keepdims=True)
    m = jnp.where(jnp.isfinite(m), m, 0.0)
    e = jnp.where(mask, jnp.exp(s - m), 0.0)
    return e / jnp.maximum(jnp.sum(e, -1, keepdims=True), 1e-30)


def rel_bucket(d):
    d = jnp.maximum(d, 0)
    exact = REL_BUCKETS // 2
    logd = jnp.log(jnp.maximum(d, 1).astype(jnp.float32) / exact) / math.log(REL_MAX_DIST / exact)
    large = jnp.minimum(exact + (logd * (REL_BUCKETS - exact)).astype(jnp.int32), REL_BUCKETS - 1)
    return jnp.where(d < exact, d, large)


def rwkv7_mixer(p, shift0, s0, mu, w0, w_up, a0, a_up, g_up, k_k, k_a, r_k, ln_g, ln_b):
    f32 = jnp.float32
    B, T, _ = p.shape
    prev = jnp.concatenate([shift0[:, None].astype(p.dtype), p[:, :-1]], axis=1)
    m = p + mu * (prev - p)
    r = m[..., :RW_W]
    k = m[..., RW_W:2 * RW_W]
    v = m[..., 2 * RW_W:3 * RW_W]
    o = 3 * RW_W
    wl = m[..., o:o + RW_DECAY_R]
    o += RW_DECAY_R
    al = m[..., o:o + RW_AAA_R]
    o += RW_AAA_R
    gl = m[..., o:o + RW_GATE_R]
    w = -jax.nn.softplus(-(w0 + jnp.tanh(wl) @ w_up).astype(f32)) - 0.5
    decay = jnp.exp(-jnp.exp(w))
    a = jax.nn.sigmoid((a0 + al @ a_up).astype(f32))
    g = jax.nn.sigmoid(gl) @ g_up

    def heads(t):
        return t.reshape(B, T, RW_HEADS, RW_HD).astype(f32)

    kk = l2_normalize(heads(k * k_k))
    k = k * (1.0 + (a - 1.0) * k_a)
    r_, k_, v_, w_, a_ = heads(r), heads(k), heads(v), heads(decay), heads(a)

    def step(S, inp):
        rt, wt, kt, vt, kkt, at = inp
        sa = jnp.einsum('bhvk,bhk->bhv', S, -kkt)
        S = S * wt[:, :, None, :] + sa[..., None] * (kkt * at)[:, :, None, :] + vt[..., None] * kt[:, :, None, :]
        return S, jnp.einsum('bhvk,bhk->bhv', S, rt)

    xs = tuple(jnp.moveaxis(t, 1, 0) for t in (r_, w_, k_, v_, kk, a_))
    sT, y = lax.scan(step, s0.astype(f32), xs)
    y = jnp.moveaxis(y, 0, 1)
    ym = jnp.mean(y, -1, keepdims=True)
    yv = jnp.mean(jnp.square(y - ym), -1, keepdims=True)
    y = ((y - ym) * lax.rsqrt(yv + RW_GN_EPS)).reshape(B, T, RW_W) * ln_g + ln_b
    bonus = jnp.sum(r_ * k_ * r_k, -1, keepdims=True) * v_
    y = (y + bonus.reshape(B, T, RW_W)) * g
    return y.astype(p.dtype), p[:, -1], sT


def gated_delta_chunked(q, k, v, beta, g, s0):
    B, T, H, DK = q.shape
    DV = v.shape[-1]
    C = GD_CHUNK
    n = -(-T // C)
    pad = n * C - T

    def chunks(a):
        a = jnp.pad(a, [(0, 0), (0, pad)] + [(0, 0)] * (a.ndim - 2))
        a = a.reshape((B, n, C) + a.shape[2:])
        return jnp.moveaxis(jnp.moveaxis(a, 1, 0), 3, 2)

    qc, kc, vc, bc, gc = (chunks(t) for t in (q, k, v, beta, g))
    G = jnp.cumsum(gc, -1)
    tri = jnp.tril(jnp.ones((C, C), bool))
    strict = jnp.tril(jnp.ones((C, C), bool), -1)
    diff = G[..., :, None] - G[..., None, :]
    decay = jnp.where(tri, jnp.exp(jnp.where(tri, diff, 0.0)), 0.0)
    kb = kc * bc[..., None]
    Lm = jnp.where(strict, jnp.einsum('...id,...jd->...ij', kb, kc) * decay, 0.0)
    eye = jnp.eye(C, dtype=Lm.dtype)
    Tinv = lax.linalg.triangular_solve(eye + Lm, jnp.broadcast_to(eye, Lm.shape), left_side=True, lower=True, unit_diagonal=True)
    U = Tinv @ (vc * bc[..., None])
    W = Tinv @ (kb * jnp.exp(G)[..., None])
    Aqk = jnp.einsum('...id,...jd->...ij', qc, kc) * decay
    Qg = qc * jnp.exp(G)[..., None]
    Glast = G[..., -1:]
    Kd = kc * jnp.exp(Glast - G)[..., None]
    aG = jnp.exp(Glast)[..., None]

    def step(S, inp):
        U_, W_, Q_, A_, K_, a_ = inp
        vn = U_ - W_ @ S
        out = Q_ @ S + A_ @ vn
        S = S * a_ + jnp.swapaxes(K_, -1, -2) @ vn
        return S, out

    sT, o = lax.scan(step, s0, (U, W, Qg, Aqk, Kd, aG))
    o = jnp.moveaxis(jnp.moveaxis(o, 0, 1), 2, 3).reshape(B, n * C, H, DV)[:, :T]
    return o, sT


def gated_deltanet_mixer(p, conv0, s0, conv_w, a_log, dt_bias, norm_g):
    f32 = jnp.float32
    B, T, _ = p.shape
    qkv = p[..., :GD_QKV]
    z = p[..., GD_QKV:GD_QKV + GD_W]
    bl = p[..., GD_QKV + GD_W:GD_QKV + GD_W + GD_HEADS]
    al = p[..., GD_QKV + GD_W + GD_HEADS:]
    xc = jnp.concatenate([conv0.astype(p.dtype), qkv], axis=1)
    conv = xc[:, :T] * conv_w[0]
    for i in range(1, GD_CONV):
        conv = conv + xc[:, i:i + T] * conv_w[i]
    conv = jax.nn.silu(conv).astype(f32)
    q = l2_normalize(conv[..., :GD_W].reshape(B, T, GD_HEADS, GD_HD)) * GD_HD ** -0.5
    k = l2_normalize(conv[..., GD_W:2 * GD_W].reshape(B, T, GD_HEADS, GD_HD))
    v = conv[..., 2 * GD_W:].reshape(B, T, GD_HEADS, GD_HD)
    beta = jax.nn.sigmoid(bl.astype(f32))
    g = -jnp.exp(a_log.astype(f32)) * jax.nn.softplus((al + dt_bias).astype(f32))
    o, sT = gated_delta_chunked(q, k, v, beta, g, s0.astype(f32))
    o = o * lax.rsqrt(jnp.mean(o * o, -1, keepdims=True) + 1e-6) * norm_g
    o = o.reshape(B, T, GD_W) * jax.nn.silu(z.astype(f32))
    return o.astype(p.dtype), xc[:, -(GD_CONV - 1):], sT


def nsa_project(p):
    B, T, _ = p.shape
    q = p[..., :NS_W].reshape(B, T, NS_HEADS, NS_HD)
    kv = p[..., NS_W:NS_W + 6 * NS_KVW].reshape(B, T, 6, NS_KV, NS_HD)
    gates = jax.nn.sigmoid(p[..., NS_W + 6 * NS_KVW:].astype(jnp.float32)).reshape(B, T, 3, NS_HEADS)
    return q, kv, gates


def nsa_compress(x, w1, w2):
    B, L = x.shape[:2]
    r = CMP_LEN // CMP_STRIDE
    n_sub = L // CMP_STRIDE
    n_cmp = n_sub - r + 1
    sub = x[:, :n_sub * CMP_STRIDE].reshape(B, n_sub, CMP_STRIDE, NS_KV, NS_HD)
    blk = jnp.concatenate([sub[:, i:i + n_cmp] for i in range(r)], axis=2)
    flat = jnp.moveaxis(blk, 3, 2).reshape(B, n_cmp, NS_KV, CMP_LEN * NS_HD)
    return jax.nn.gelu(flat @ w1, approximate=False) @ w2


def nsa_sel_blocks(x):
    B, L = x.shape[:2]
    ns = -(-L // SEL_BLOCK)
    x = jnp.pad(x, ((0, 0), (0, ns * SEL_BLOCK - L), (0, 0), (0, 0)))
    return jnp.transpose(x.reshape(B, ns, SEL_BLOCK, NS_KV, NS_HD), (0, 3, 1, 2, 4))


def nsa_attend(q, t, gates, kcmp, vcmp, kb, vb, kwin, vwin, pwin, rel_bias):
    f32 = jnp.float32
    B, Tq = q.shape[:2]
    qg = jnp.transpose(q.reshape(B, Tq, NS_KV, NS_GROUP, NS_HD), (0, 2, 3, 1, 4)).astype(f32) * NS_HD ** -0.5
    bias_tab = rel_bias.astype(f32).reshape(REL_BUCKETS, NS_KV, NS_GROUP)

    def shared_bias(d):
        return jnp.transpose(bias_tab[rel_bucket(d)], (2, 3, 0, 1))

    nc = kcmp.shape[1]
    ci = jnp.arange(nc)
    dc = t[:, None] - (ci * CMP_STRIDE + CMP_LEN - 1)[None, :]
    sc = jnp.einsum('bkgqd,bnkd->bkgqn', qg, kcmp.astype(f32)) + shared_bias(dc)
    pc = masked_softmax(sc, dc >= 0)
    o_cmp = jnp.einsum('bkgqn,bnkd->bkgqd', pc, vcmp.astype(f32))
    ns = kb.shape[2]
    j = jnp.arange(ns)
    overlap = ((ci[:, None] * CMP_STRIDE < (j[None, :] + 1) * SEL_BLOCK) & (ci[:, None] * CMP_STRIDE + CMP_LEN > j[None, :] * SEL_BLOCK)).astype(f32)
    ps = jnp.einsum('bkgqn,nj->bkqj', pc, overlap)
    cur = t // SEL_BLOCK
    valid = j[None, :] <= cur[:, None]
    forced = (j[None, :] == 0) | (j[None, :] == cur[:, None]) | (j[None, :] == cur[:, None] - 1)
    score = jnp.where(valid, jnp.where(forced, FORCE_SCORE, ps), -jnp.inf)
    nsel = min(SEL_TOPK, ns)
    top_s, top_i = lax.top_k(score, nsel)
    bi = jnp.arange(B)[:, None, None, None]
    ki = jnp.arange(NS_KV)[None, :, None, None]
    kg = kb[bi, ki, top_i].astype(f32)
    vg = vb[bi, ki, top_i].astype(f32)
    pos = top_i[..., None] * SEL_BLOCK + jnp.arange(SEL_BLOCK)
    ds = t[None, None, :, None, None] - pos
    ms = jnp.isfinite(top_s)[..., None] & (ds >= 0)
    sb = jnp.moveaxis(bias_tab[rel_bucket(ds), ki[..., None]], -1, 2)
    ss = jnp.einsum('bkgqd,bkqnld->bkgqnl', qg, kg) + sb
    m_all = nsel * SEL_BLOCK
    psl = masked_softmax(ss.reshape(B, NS_KV, NS_GROUP, Tq, m_all), ms[:, :, None].reshape(B, NS_KV, 1, Tq, m_all))
    o_slc = jnp.einsum('bkgqm,bkqmd->bkgqd', psl, vg.reshape(B, NS_KV, Tq, m_all, NS_HD))
    dw = t[:, None] - pwin[None, :]
    mw = (dw >= 0) & (dw <= WINDOW) & (pwin[None, :] >= 0)
    sw = jnp.einsum('bkgqd,blkd->bkgql', qg, kwin.astype(f32)) + shared_bias(dw)
    pw = masked_softmax(sw, mw)
    o_win = jnp.einsum('bkgql,blkd->bkgqd', pw, vwin.astype(f32))
    gt = jnp.moveaxis(gates, 1, -1).reshape(B, 3, NS_KV, NS_GROUP, Tq)[..., None]
    o = gt[:, 0] * o_cmp + gt[:, 1] * o_slc + gt[:, 2] * o_win
    return jnp.transpose(o, (0, 3, 1, 2, 4)).reshape(B, Tq, NS_W).astype(q.dtype)


def nsa_prompt(p, phi_k1, phi_k2, phi_v1, phi_v2, rel_bias):
    q, kv, gates = nsa_project(p)
    B, T = q.shape[:2]
    kcmp = nsa_compress(kv[:, :, 0], phi_k1, phi_k2)
    vcmp = nsa_compress(kv[:, :, 1], phi_v1, phi_v2)
    kb = nsa_sel_blocks(kv[:, :, 2])
    vb = nsa_sel_blocks(kv[:, :, 3])
    pad_w = ((0, 0), (WINDOW, 0), (0, 0), (0, 0))
    kw = jnp.pad(kv[:, :, 4], pad_w)
    vw = jnp.pad(kv[:, :, 5], pad_w)
    qb = NS_QBLOCK if T % NS_QBLOCK == 0 else T

    def block(i):
        s = i * qb
        t = s + jnp.arange(qb)
        pw = s - WINDOW + jnp.arange(WINDOW + qb)
        return nsa_attend(lax.dynamic_slice_in_dim(q, s, qb, 1), t, lax.dynamic_slice_in_dim(gates, s, qb, 1), kcmp, vcmp, kb, vb, lax.dynamic_slice_in_dim(kw, s, WINDOW + qb, 1), lax.dynamic_slice_in_dim(vw, s, WINDOW + qb, 1), pw, rel_bias)

    o = lax.map(block, jnp.arange(T // qb))
    o = jnp.moveaxis(o, 0, 1).reshape(B, T, NS_W)
    wl = min(WINDOW, T)
    return o, kv[:, :, :4], kv[:, T - wl:, 4:]


def nsa_sample(p, cache_kv_l, page_table, win_buf, phi_k1, phi_k2, phi_v1, phi_v2, rel_bias):
    q, kv, gates = nsa_project(p)
    DB, Tn = q.shape[:2]
    past = cache_kv_l[page_table]
    past = past.reshape(DB, -1, 4, NS_KV, NS_HD)
    P = past.shape[1]
    ctx = jnp.concatenate([past, kv[:, :, :4].astype(past.dtype)], axis=1)
    kcmp = nsa_compress(ctx[:, :, 0], phi_k1, phi_k2)
    vcmp = nsa_compress(ctx[:, :, 1], phi_v1, phi_v2)
    kb = nsa_sel_blocks(ctx[:, :, 2])
    vb = nsa_sel_blocks(ctx[:, :, 3])
    Wb = win_buf.shape[1]
    win = jnp.concatenate([win_buf, kv[:, :, 4:].astype(win_buf.dtype)], axis=1)
    t = P + jnp.arange(Tn)
    pw = P - Wb + jnp.arange(Wb + Tn)
    o = nsa_attend(q, t, gates, kcmp, vcmp, kb, vb, win[:, :, 0], win[:, :, 1], pw, rel_bias)
    return o, kv[:, :, :4], win[:, Tn:]


def peer_ffn(x, wq, k1, k2, u_tab, v_tab):
    f32 = jnp.float32
    Bx, T, D = x.shape
    xt = x.reshape(-1, D)
    n = xt.shape[0]
    q = matmul(xt, wq).astype(f32).reshape(n, PEER_HEADS, 2, PEER_DKEY // 2)
    s1 = jnp.einsum('nhd,kd->nhk', q[:, :, 0], k1.astype(f32))
    s2 = jnp.einsum('nhd,kd->nhk', q[:, :, 1], k2.astype(f32))
    v1, i1 = lax.top_k(s1, PEER_TOPK)
    v2, i2 = lax.top_k(s2, PEER_TOPK)
    cand = (v1[..., :, None] + v2[..., None, :]).reshape(n, PEER_HEADS, PEER_TOPK * PEER_TOPK)
    cidx = (i1[..., :, None] * PEER_NKEYS + i2[..., None, :]).reshape(n, PEER_HEADS, PEER_TOPK * PEER_TOPK)
    sv, sp = lax.top_k(cand, PEER_TOPK)
    eidx = jnp.take_along_axis(cidx, sp, axis=-1)
    gw = jax.nn.softmax(sv, axis=-1)
    nb = -(-n // PEER_TBLOCK)
    pad = nb * PEER_TBLOCK - n
    xb = jnp.pad(xt, ((0, pad), (0, 0))).reshape(nb, PEER_TBLOCK, D)
    eb = jnp.pad(eidx.reshape(n, -1), ((0, pad), (0, 0))).reshape(nb, PEER_TBLOCK, -1)
    gb = jnp.pad(gw.reshape(n, -1), ((0, pad), (0, 0))).reshape(nb, PEER_TBLOCK, -1)

    def blk(args):
        xi, ei, gi = args
        u = jnp.take(u_tab, ei, axis=0)
        h = jax.nn.gelu(jnp.einsum('td,ted->te', xi, u).astype(f32), approximate=False)
        vv = jnp.take(v_tab, ei, axis=0)
        return jnp.einsum('te,ted->td', (gi * h).astype(vv.dtype), vv)

    out = lax.map(blk, (xb, eb, gb)).reshape(-1, D)[:n]
    return out.reshape(Bx, T, D).astype(x.dtype)


def residual_block(x, mix, w_out, ln1_g, ln1_b, ln2_g, ln2_b, peer_wq, peer_k1, peer_k2, peer_u, peer_v):
    x = layer_norm(ALPHA * x + matmul3(mix, w_out), ln1_g, ln1_b)
    return layer_norm(ALPHA * x + peer_ffn(x, peer_wq, peer_k1, peer_k2, peer_u, peer_v), ln2_g, ln2_b)


def kernel(x_prompt, x_sample, cache_kv, cache_win, state_rwkv, state_rwkv_shift, state_gdn, state_gdn_conv, page_table, w_in, w_out, ln1_g, ln1_b, ln2_g, ln2_b, rw_mu, rw_w0, rw_w_up, rw_a0, rw_a_up, rw_g_up, rw_k_k, rw_k_a, rw_r_k, rw_ln_g, rw_ln_b, gd_conv_w, gd_a_log, gd_dt_bias, gd_norm_g, ns_phi_k1, ns_phi_k2, ns_phi_v1, ns_phi_v2, rel_bias, peer_wq, peer_k1, peer_k2, peer_u, peer_v):
    xp, xs = x_prompt, x_sample
    B = xp.shape[0]
    o_b = RW_COLS
    o_c = RW_COLS + GD_COLS
    kv_p, kv_s, win_p, win_s, rw_p, rw_s, sh_p, sh_s, gd_p, gd_s, cv_p, cv_s = ([] for _ in range(12))
    for l in range(DEPTH):
        rw = (rw_mu[l], rw_w0[l], rw_w_up[l], rw_a0[l], rw_a_up[l], rw_g_up[l], rw_k_k[l], rw_k_a[l], rw_r_k[l], rw_ln_g[l], rw_ln_b[l])
        gd = (gd_conv_w[l], gd_a_log[l], gd_dt_bias[l], gd_norm_g[l])
        phi = (ns_phi_k1[l], ns_phi_k2[l], ns_phi_v1[l], ns_phi_v2[l])
        tail = (w_out[l], ln1_g[l], ln1_b[l], ln2_g[l], ln2_b[l], peer_wq[l], peer_k1[l], peer_k2[l], peer_u[l], peer_v[l])
        pp = matmul3(xp, w_in[l])
        a, sh, rs = rwkv7_mixer(pp[..., :o_b], jnp.zeros((B, RW_COLS), pp.dtype), jnp.zeros((B, RW_HEADS, RW_HD, RW_HD), jnp.float32), *rw)
        b, cv, gs = gated_deltanet_mixer(pp[..., o_b:o_c], jnp.zeros((B, GD_CONV - 1, GD_QKV), pp.dtype), jnp.zeros((B, GD_HEADS, GD_HD, GD_HD), jnp.float32), *gd)
        c, kvr, wr = nsa_prompt(pp[..., o_c:], *phi, rel_bias)
        xp = residual_block(xp, jnp.concatenate([a, b, c.astype(a.dtype)], -1), *tail)
        kv_p.append(kvr)
        win_p.append(wr)
        rw_p.append(rs)
        sh_p.append(sh)
        gd_p.append(gs)
        cv_p.append(cv)
        ps = matmul3(xs, w_in[l])
        a, sh, rs = rwkv7_mixer(ps[..., :o_b], state_rwkv_shift[l], state_rwkv[l], *rw)
        b, cv, gs = gated_deltanet_mixer(ps[..., o_b:o_c], state_gdn_conv[l], state_gdn[l], *gd)
        c, kvr, wr = nsa_sample(ps[..., o_c:], cache_kv[l], page_table, cache_win[l], *phi, rel_bias)
        xs = residual_block(xs, jnp.concatenate([a, b, c.astype(a.dtype)], -1), *tail)
        kv_s.append(kvr)
        win_s.append(wr)
        rw_s.append(rs)
        sh_s.append(sh)
        gd_s.append(gs)
        cv_s.append(cv)
    st = jnp.stack
    return (xp, xs, st(kv_p), st(kv_s), st(win_p), st(win_s), st(rw_p), st(rw_s), st(sh_p), st(sh_s), st(gd_p), st(gd_s), st(cv_p), st(cv_s))
```

```python
import functools
import math

import jax
import jax.numpy as jnp
from jax import lax
from jax.experimental import pallas as pl
from jax.experimental.pallas import tpu as pltpu

D_MODEL = 4096
BATCH = 4
SEQ = 2048
DEPTH = 4
DEC_BATCH = 8
DEC_SEQ = 8
PAST_LEN = 8192
PAGE_SIZE = 128

ALPHA = (2 * DEPTH) ** 0.25
LN_EPS = 1e-5

RW_HD = 64
RW_W = D_MODEL // 4
RW_HEADS = RW_W // RW_HD
RW_DECAY_R = 64
RW_AAA_R = 64
RW_GATE_R = 160
RW_COLS = 3 * RW_W + RW_DECAY_R + RW_AAA_R + RW_GATE_R
RW_GN_EPS = 64e-5

GD_HD = 128
GD_W = D_MODEL // 4
GD_HEADS = GD_W // GD_HD
GD_QKV = 3 * GD_W
GD_CONV = 4
GD_CHUNK = 64
GD_COLS = GD_QKV + GD_W + 2 * GD_HEADS

NS_HD = 128
NS_W = D_MODEL // 2
NS_HEADS = NS_W // NS_HD
NS_KV = 4
NS_GROUP = NS_HEADS // NS_KV
NS_KVW = NS_KV * NS_HD
NS_COLS = NS_W + 6 * NS_KVW + 3 * NS_HEADS
CMP_LEN = 32
CMP_STRIDE = 16
CMP_HIDDEN = 128
SEL_BLOCK = 64
SEL_TOPK = 16
WINDOW = 512
NS_QBLOCK = 32
FORCE_SCORE = 1e4

REL_BUCKETS = 32
REL_MAX_DIST = 1024

D_MIX = RW_W + GD_W + NS_W
IN_COLS = RW_COLS + GD_COLS + NS_COLS

PEER_HEADS = 8
PEER_NKEYS = 128
PEER_EXPERTS = PEER_NKEYS ** 2
PEER_DKEY = 256
PEER_TOPK = 16
PEER_TBLOCK = 128

V7X_VMEM_LIMIT_BYTES = 56 * 1024 * 1024


def _matmul_body(x_ref, w_ref, o_ref):
    o_ref[...] = jnp.dot(x_ref[...].astype(jnp.bfloat16), w_ref[...].astype(jnp.bfloat16),
                         preferred_element_type=jnp.float32)


def _pick_tile(n, target):
    t = min(n, target)
    while n % t:
        t //= 2
    return t


def matmul(x, w, tm=512, tn=512):
    M, K = x.shape
    N = w.shape[1]
    tm = _pick_tile(M, tm)
    tn = min(tn, N)
    return pl.pallas_call(
        _matmul_body,
        grid=(M // tm, pl.cdiv(N, tn)),
        in_specs=[pl.BlockSpec((tm, K), lambda i, j: (i, 0)),
                  pl.BlockSpec((K, tn), lambda i, j: (0, j))],
        out_specs=pl.BlockSpec((tm, tn), lambda i, j: (i, j)),
        out_shape=jax.ShapeDtypeStruct((M, N), jnp.float32),
        compiler_params=pltpu.CompilerParams(
            dimension_semantics=("parallel", "parallel"),
            vmem_limit_bytes=V7X_VMEM_LIMIT_BYTES),
        name="proj_matmul",
    )(x, w)


def matmul3(x, w):
    B, T, K = x.shape
    return matmul(x.reshape(B * T, K), w).reshape(B, T, -1)


def layer_norm(x, g, b):
    xf = x.astype(jnp.float32)
    mu = jnp.mean(xf, -1, keepdims=True)
    var = jnp.mean(jnp.square(xf - mu), -1, keepdims=True)
    return ((xf - mu) * lax.rsqrt(var + LN_EPS) * g + b).astype(x.dtype)


def l2_normalize(x):
    return x / jnp.maximum(jnp.sqrt(jnp.sum(x * x, -1, keepdims=True)), 1e-12)


def masked_softmax(s, mask):
    s = jnp.where(mask, s, -jnp.inf)
    m = jnp.max(s, -1, keepdims=True)
    m = jnp.where(jnp.isfinite(m), m, 0.0)
    e = jnp.where(mask, jnp.exp(s - m), 0.0)
    return e / jnp.maximum(jnp.sum(e, -1, keepdims=True), 1e-30)


def rel_bucket(d):
    d = jnp.maximum(d, 0)
    exact = REL_BUCKETS // 2
    logd = jnp.log(jnp.maximum(d, 1).astype(jnp.float32) / exact) / math.log(REL_MAX_DIST / exact)
    large = jnp.minimum(exact + (logd * (REL_BUCKETS - exact)).astype(jnp.int32), REL_BUCKETS - 1)
    return jnp.where(d < exact, d, large)


RW_TC = 128


def _rwkv_scan_body(r_ref, w_ref, k_ref, v_ref, kk_ref, ka_ref, s0_ref, y_ref, s_ref, vt_ref, yt_ref,
                    *, n_pairs, n_steps):
    c = pl.program_id(2)
    N = RW_HD
    lane = lax.broadcasted_iota(jnp.int32, (N, 2 * N), 1)
    first = lane < N

    @pl.when(c == 0)
    def _():
        s_ref[...] = s0_ref[...]

    for p in range(n_pairs):
        vt_ref[p] = v_ref[0, p].T
    yt_ref[...] = jnp.zeros_like(yt_ref)

    def half_sums(x):
        a = jnp.sum(jnp.where(first, x, 0.0), axis=1, keepdims=True)
        b = jnp.sum(jnp.where(first, 0.0, x), axis=1, keepdims=True)
        return a, b

    def step(j, carry):
        hit = lane == j
        for p in range(n_pairs):
            S = s_ref[0, p]
            row = lambda ref: ref[0, p, pl.ds(j, 1), :]
            sa_a, sa_b = half_sums(S * row(kk_ref))
            sa = -jnp.where(first, sa_a, sa_b)
            va = jnp.sum(jnp.where(hit, vt_ref[p, 0:N, :], 0.0), axis=1, keepdims=True)
            vb = jnp.sum(jnp.where(hit, vt_ref[p, N:2 * N, :], 0.0), axis=1, keepdims=True)
            S = S * row(w_ref) + sa * row(ka_ref) + jnp.where(first, va, vb) * row(k_ref)
            s_ref[0, p] = S
            ya, yb = half_sums(S * row(r_ref))
            yt_ref[p, 0:N, :] = jnp.where(hit, ya, yt_ref[p, 0:N, :])
            yt_ref[p, N:2 * N, :] = jnp.where(hit, yb, yt_ref[p, N:2 * N, :])
        return carry

    lax.fori_loop(0, n_steps, step, 0)
    for p in range(n_pairs):
        y_ref[0, p] = yt_ref[p].T


def rwkv_scan(r, w, k, v, kk, ka, s0, pairs_per_step=4):
    B, T, H, N = r.shape
    P = H // 2
    Tp = -(-T // RW_TC) * RW_TC
    n_steps = min(T, RW_TC)
    assert T % n_steps == 0

    def prep(x, fill):
        x = x.reshape(B, T, P, 2 * N)
        if Tp != T:
            x = jnp.pad(x, ((0, 0), (0, Tp - T), (0, 0), (0, 0)), constant_values=fill)
        return jnp.transpose(x, (0, 2, 1, 3))

    ins = [prep(r, 0.0), prep(w, 1.0), prep(k, 0.0), prep(v, 0.0), prep(kk, 0.0), prep(ka, 0.0)]
    s0p = jnp.transpose(s0.reshape(B, P, 2, N, N), (0, 1, 3, 2, 4)).reshape(B, P, N, 2 * N)
    pp = pairs_per_step
    seq = pl.BlockSpec((1, pp, RW_TC, 2 * N), lambda b, g, c: (b, g, c, 0))
    st = pl.BlockSpec((1, pp, N, 2 * N), lambda b, g, c: (b, g, 0, 0))
    y, sT = pl.pallas_call(
        functools.partial(_rwkv_scan_body, n_pairs=pp, n_steps=n_steps),
        grid=(B, P // pp, Tp // RW_TC),
        in_specs=[seq] * 6 + [st],
        out_specs=[seq, st],
        out_shape=[jax.ShapeDtypeStruct((B, P, Tp, 2 * N), jnp.float32),
                   jax.ShapeDtypeStruct((B, P, N, 2 * N), jnp.float32)],
        scratch_shapes=[pltpu.VMEM((pp, 2 * N, RW_TC), jnp.float32), pltpu.VMEM((pp, 2 * N, RW_TC), jnp.float32)],
        compiler_params=pltpu.CompilerParams(dimension_semantics=("parallel", "parallel", "arbitrary"),
                                             vmem_limit_bytes=V7X_VMEM_LIMIT_BYTES),
        name="rwkv_scan",
    )(*ins, s0p)
    y = jnp.transpose(y, (0, 2, 1, 3))[:, :T].reshape(B, T, H, N)
    sT = jnp.transpose(sT.reshape(B, P, N, 2, N), (0, 1, 3, 2, 4)).reshape(B, H, N, N)
    return y, sT


def rwkv7_mixer(p, shift0, s0, mu, w0, w_up, a0, a_up, g_up, k_k, k_a, r_k, ln_g, ln_b):
    f32 = jnp.float32
    B, T, _ = p.shape
    prev = jnp.concatenate([shift0[:, None].astype(p.dtype), p[:, :-1]], axis=1)
    m = p + mu * (prev - p)
    r = m[..., :RW_W]
    k = m[..., RW_W:2 * RW_W]
    v = m[..., 2 * RW_W:3 * RW_W]
    o = 3 * RW_W
    wl = m[..., o:o + RW_DECAY_R]
    o += RW_DECAY_R
    al = m[..., o:o + RW_AAA_R]
    o += RW_AAA_R
    gl = m[..., o:o + RW_GATE_R]
    w = -jax.nn.softplus(-(w0 + jnp.tanh(wl) @ w_up).astype(f32)) - 0.5
    decay = jnp.exp(-jnp.exp(w))
    a = jax.nn.sigmoid((a0 + al @ a_up).astype(f32))
    g = jax.nn.sigmoid(gl) @ g_up

    def heads(t):
        return t.reshape(B, T, RW_HEADS, RW_HD).astype(f32)

    kk = l2_normalize(heads(k * k_k))
    k = k * (1.0 + (a - 1.0) * k_a)
    r_, k_, v_, w_, a_ = heads(r), heads(k), heads(v), heads(decay), heads(a)

    y, sT = rwkv_scan(r_, w_, k_, v_, kk, kk * a_, s0.astype(f32))
    ym = jnp.mean(y, -1, keepdims=True)
    yv = jnp.mean(jnp.square(y - ym), -1, keepdims=True)
    y = ((y - ym) * lax.rsqrt(yv + RW_GN_EPS)).reshape(B, T, RW_W) * ln_g + ln_b
    bonus = jnp.sum(r_ * k_ * r_k, -1, keepdims=True) * v_
    y = (y + bonus.reshape(B, T, RW_W)) * g
    return y.astype(p.dtype), p[:, -1], sT


def gated_delta_chunked(q, k, v, beta, g, s0):
    B, T, H, DK = q.shape
    DV = v.shape[-1]
    C = GD_CHUNK
    n = -(-T // C)
    pad = n * C - T

    def chunks(a):
        a = jnp.pad(a, [(0, 0), (0, pad)] + [(0, 0)] * (a.ndim - 2))
        a = a.reshape((B, n, C) + a.shape[2:])
        return jnp.moveaxis(jnp.moveaxis(a, 1, 0), 3, 2)

    qc, kc, vc, bc, gc = (chunks(t) for t in (q, k, v, beta, g))
    G = jnp.cumsum(gc, -1)
    tri = jnp.tril(jnp.ones((C, C), bool))
    strict = jnp.tril(jnp.ones((C, C), bool), -1)
    diff = G[..., :, None] - G[..., None, :]
    decay = jnp.where(tri, jnp.exp(jnp.where(tri, diff, 0.0)), 0.0)
    kb = kc * bc[..., None]
    Lm = jnp.where(strict, jnp.einsum('...id,...jd->...ij', kb, kc) * decay, 0.0)
    eye = jnp.eye(C, dtype=Lm.dtype)
    Tinv = lax.linalg.triangular_solve(eye + Lm, jnp.broadcast_to(eye, Lm.shape), left_side=True, lower=True, unit_diagonal=True)
    U = Tinv @ (vc * bc[..., None])
    W = Tinv @ (kb * jnp.exp(G)[..., None])
    Aqk = jnp.einsum('...id,...jd->...ij', qc, kc) * decay
    Qg = qc * jnp.exp(G)[..., None]
    Glast = G[..., -1:]
    Kd = kc * jnp.exp(Glast - G)[..., None]
    aG = jnp.exp(Glast)[..., None]

    def step(S, inp):
        U_, W_, Q_, A_, K_, a_ = inp
        vn = U_ - W_ @ S
        out = Q_ @ S + A_ @ vn
        S = S * a_ + jnp.swapaxes(K_, -1, -2) @ vn
        return S, out

    sT, o = lax.scan(step, s0, (U, W, Qg, Aqk, Kd, aG))
    o = jnp.moveaxis(jnp.moveaxis(o, 0, 1), 2, 3).reshape(B, n * C, H, DV)[:, :T]
    return o, sT


def gated_deltanet_mixer(p, conv0, s0, conv_w, a_log, dt_bias, norm_g):
    f32 = jnp.float32
    B, T, _ = p.shape
    qkv = p[..., :GD_QKV]
    z = p[..., GD_QKV:GD_QKV + GD_W]
    bl = p[..., GD_QKV + GD_W:GD_QKV + GD_W + GD_HEADS]
    al = p[..., GD_QKV + GD_W + GD_HEADS:]
    xc = jnp.concatenate([conv0.astype(p.dtype), qkv], axis=1)
    conv = xc[:, :T] * conv_w[0]
    for i in range(1, GD_CONV):
        conv = conv + xc[:, i:i + T] * conv_w[i]
    conv = jax.nn.silu(conv).astype(f32)
    q = l2_normalize(conv[..., :GD_W].reshape(B, T, GD_HEADS, GD_HD)) * GD_HD ** -0.5
    k = l2_normalize(conv[..., GD_W:2 * GD_W].reshape(B, T, GD_HEADS, GD_HD))
    v = conv[..., 2 * GD_W:].reshape(B, T, GD_HEADS, GD_HD)
    beta = jax.nn.sigmoid(bl.astype(f32))
    g = -jnp.exp(a_log.astype(f32)) * jax.nn.softplus((al + dt_bias).astype(f32))
    o, sT = gated_delta_chunked(q, k, v, beta, g, s0.astype(f32))
    o = o * lax.rsqrt(jnp.mean(o * o, -1, keepdims=True) + 1e-6) * norm_g
    o = o.reshape(B, T, GD_W) * jax.nn.silu(z.astype(f32))
    return o.astype(p.dtype), xc[:, -(GD_CONV - 1):], sT


def nsa_project(p):
    B, T, _ = p.shape
    q = p[..., :NS_W].reshape(B, T, NS_HEADS, NS_HD)
    kv = p[..., NS_W:NS_W + 6 * NS_KVW].reshape(B, T, 6, NS_KV, NS_HD)
    gates = jax.nn.sigmoid(p[..., NS_W + 6 * NS_KVW:].astype(jnp.float32)).reshape(B, T, 3, NS_HEADS)
    return q, kv, gates


def nsa_compress(x, w1, w2):
    B, L = x.shape[:2]
    r = CMP_LEN // CMP_STRIDE
    n_sub = L // CMP_STRIDE
    n_cmp = n_sub - r + 1
    sub = x[:, :n_sub * CMP_STRIDE].reshape(B, n_sub, CMP_STRIDE, NS_KV, NS_HD)
    blk = jnp.concatenate([sub[:, i:i + n_cmp] for i in range(r)], axis=2)
    flat = jnp.moveaxis(blk, 3, 2).reshape(B, n_cmp, NS_KV, CMP_LEN * NS_HD)
    return jax.nn.gelu(flat @ w1, approximate=False) @ w2


def nsa_sel_blocks(x):
    B, L = x.shape[:2]
    ns = -(-L // SEL_BLOCK)
    x = jnp.pad(x, ((0, 0), (0, ns * SEL_BLOCK - L), (0, 0), (0, 0)))
    return jnp.transpose(x.reshape(B, ns, SEL_BLOCK, NS_KV, NS_HD), (0, 3, 1, 2, 4))


def nsa_attend(q, t, gates, kcmp, vcmp, kb, vb, kwin, vwin, pwin, rel_bias):
    f32 = jnp.float32
    B, Tq = q.shape[:2]
    qg = jnp.transpose(q.reshape(B, Tq, NS_KV, NS_GROUP, NS_HD), (0, 2, 3, 1, 4)).astype(f32) * NS_HD ** -0.5
    bias_tab = rel_bias.astype(f32).reshape(REL_BUCKETS, NS_KV, NS_GROUP)

    def shared_bias(d):
        return jnp.transpose(bias_tab[rel_bucket(d)], (2, 3, 0, 1))

    nc = kcmp.shape[1]
    ci = jnp.arange(nc)
    dc = t[:, None] - (ci * CMP_STRIDE + CMP_LEN - 1)[None, :]
    sc = jnp.einsum('bkgqd,bnkd->bkgqn', qg, kcmp.astype(f32)) + shared_bias(dc)
    pc = masked_softmax(sc, dc >= 0)
    o_cmp = jnp.einsum('bkgqn,bnkd->bkgqd', pc, vcmp.astype(f32))
    ns = kb.shape[2]
    j = jnp.arange(ns)
    overlap = ((ci[:, None] * CMP_STRIDE < (j[None, :] + 1) * SEL_BLOCK) & (ci[:, None] * CMP_STRIDE + CMP_LEN > j[None, :] * SEL_BLOCK)).astype(f32)
    ps = jnp.einsum('bkgqn,nj->bkqj', pc, overlap)
    cur = t // SEL_BLOCK
    valid = j[None, :] <= cur[:, None]
    forced = (j[None, :] == 0) | (j[None, :] == cur[:, None]) | (j[None, :] == cur[:, None] - 1)
    score = jnp.where(valid, jnp.where(forced, FORCE_SCORE, ps), -jnp.inf)
    nsel = min(SEL_TOPK, ns)
    top_s, top_i = lax.top_k(score, nsel)
    bi = jnp.arange(B)[:, None, None, None]
    ki = jnp.arange(NS_KV)[None, :, None, None]
    kg = kb[bi, ki, top_i].astype(f32)
    vg = vb[bi, ki, top_i].astype(f32)
    pos = top_i[..., None] * SEL_BLOCK + jnp.arange(SEL_BLOCK)
    ds = t[None, None, :, None, None] - pos
    ms = jnp.isfinite(top_s)[..., None] & (ds >= 0)
    sb = jnp.moveaxis(bias_tab[rel_bucket(ds), ki[..., None]], -1, 2)
    ss = jnp.einsum('bkgqd,bkqnld->bkgqnl', qg, kg) + sb
    m_all = nsel * SEL_BLOCK
    psl = masked_softmax(ss.reshape(B, NS_KV, NS_GROUP, Tq, m_all), ms[:, :, None].reshape(B, NS_KV, 1, Tq, m_all))
    o_slc = jnp.einsum('bkgqm,bkqmd->bkgqd', psl, vg.reshape(B, NS_KV, Tq, m_all, NS_HD))
    dw = t[:, None] - pwin[None, :]
    mw = (dw >= 0) & (dw <= WINDOW) & (pwin[None, :] >= 0)
    sw = jnp.einsum('bkgqd,blkd->bkgql', qg, kwin.astype(f32)) + shared_bias(dw)
    pw = masked_softmax(sw, mw)
    o_win = jnp.einsum('bkgql,blkd->bkgqd', pw, vwin.astype(f32))
    gt = jnp.moveaxis(gates, 1, -1).reshape(B, 3, NS_KV, NS_GROUP, Tq)[..., None]
    o = gt[:, 0] * o_cmp + gt[:, 1] * o_slc + gt[:, 2] * o_win
    return jnp.transpose(o, (0, 3, 1, 2, 4)).reshape(B, Tq, NS_W).astype(q.dtype)


TQ = 128
NEG_BIG = -1e30


def _stack_groups(x):
    return jnp.concatenate([x[:, g * NS_HD:(g + 1) * NS_HD] for g in range(NS_GROUP)], axis=0)


def _nsa_prompt_body(q_ref, ksel_ref, vsel_ref, kwin_ref, vwin_ref, gate_ref, kcmp_ref, vcmp_ref,
                     bsel_ref, bcmp_ref, o_ref, m_ref, l_ref, acc_ref, mask_ref, *, n_tiles):
    f32, bf16 = jnp.float32, jnp.bfloat16
    qi = pl.program_id(2)
    R = NS_GROUP * TQ
    qg = (_stack_groups(q_ref[0]) * NS_HD ** -0.5).astype(bf16)
    row = lax.broadcasted_iota(jnp.int32, (TQ, TQ), 0)
    col = lax.broadcasted_iota(jnp.int32, (TQ, TQ), 1)
    t_q = qi * TQ + row

    def nt_dot(a, b):
        return lax.dot_general(a, b, (((1,), (1,)), ((), ())), preferred_element_type=f32)

    def tile4(x):
        return jnp.concatenate([x] * NS_GROUP, axis=0)

    sc = nt_dot(qg, kcmp_ref[0, 0].astype(bf16))
    sc = sc + jnp.concatenate([bcmp_ref[g, 0] for g in range(NS_GROUP)], axis=0)
    okc = tile4((t_q - (col * CMP_STRIDE + CMP_LEN - 1)) >= 0)
    mc = jnp.max(jnp.where(okc, sc, NEG_BIG), axis=-1, keepdims=True)
    ec = jnp.where(okc, jnp.exp(sc - mc), 0.0)
    pc = ec / jnp.maximum(jnp.sum(ec, axis=-1, keepdims=True), 1e-30)
    o_cmp = jnp.dot(pc.astype(bf16), vcmp_ref[0, 0].astype(bf16), preferred_element_type=f32)

    pc_sum = pc[0:TQ] + pc[TQ:2 * TQ] + pc[2 * TQ:3 * TQ] + pc[3 * TQ:4 * TQ]
    overlap = ((row * CMP_STRIDE < (col + 1) * SEL_BLOCK) & (row * CMP_STRIDE + CMP_LEN > col * SEL_BLOCK)).astype(f32)
    ps = jnp.dot(pc_sum, overlap, preferred_element_type=f32, precision=lax.Precision.HIGHEST)
    cur = t_q // SEL_BLOCK
    ns = n_tiles * (TQ // SEL_BLOCK)
    valid = (col <= cur) & (col < ns)
    forced = (col == 0) | (col == cur) | (col == cur - 1)
    score = jnp.where(valid, jnp.where(forced, FORCE_SCORE, ps), -jnp.inf)
    rank = jnp.zeros((TQ, TQ), jnp.int32)
    for i in range(ns):
        ci = jnp.broadcast_to(score[:, i:i + 1], (TQ, TQ))
        beats = (ci > score) | ((ci == score) & (col > i))
        rank = rank + beats.astype(jnp.int32)
    sel = (valid & (rank < min(SEL_TOPK, ns))).astype(bf16)
    for kj in range(n_tiles):
        expand = (row == (kj * (TQ // SEL_BLOCK) + col // SEL_BLOCK)).astype(bf16)
        mask_ref[kj] = jnp.dot(sel, expand, preferred_element_type=f32)

    def attend(k_ref, v_ref, lo, kind):
        m_ref[...] = jnp.full((R, 1), NEG_BIG, f32)
        l_ref[...] = jnp.zeros((R, 1), f32)
        acc_ref[...] = jnp.zeros((R, NS_HD), f32)

        def step(kj, carry):
            off = pl.multiple_of(kj * TQ, TQ)
            kt = k_ref[0, pl.ds(off, TQ), :].astype(bf16)
            vt = v_ref[0, pl.ds(off, TQ), :].astype(bf16)
            delta = qi - kj
            s = nt_dot(qg, kt)
            s = s + jnp.concatenate([bsel_ref[g, delta] for g in range(NS_GROUP)], axis=0)
            d = delta * TQ + row - col
            if kind == "sel":
                ok = (d >= 0) & (mask_ref[kj] > 0.5)
            else:
                ok = (d >= 0) & (d <= WINDOW)
            ok = tile4(ok)
            m_old = m_ref[...]
            m_new = jnp.maximum(m_old, jnp.max(jnp.where(ok, s, NEG_BIG), axis=-1, keepdims=True))
            e = jnp.where(ok, jnp.exp(s - m_new), 0.0)
            scale = jnp.exp(m_old - m_new)
            l_ref[...] = l_ref[...] * scale + jnp.sum(e, axis=-1, keepdims=True)
            acc_ref[...] = acc_ref[...] * scale + jnp.dot(e.astype(bf16), vt, preferred_element_type=f32)
            m_ref[...] = m_new
            return carry

        lax.fori_loop(lo, qi + 1, step, 0)
        return acc_ref[...] / jnp.maximum(l_ref[...], 1e-30)

    o_sel = attend(ksel_ref, vsel_ref, 0, "sel")
    o_win = attend(kwin_ref, vwin_ref, jnp.maximum(qi - WINDOW // TQ, 0), "win")

    gates = jax.nn.sigmoid(gate_ref[0, 0].astype(f32))
    outs = []
    for g in range(NS_GROUP):
        sl = slice(g * TQ, (g + 1) * TQ)
        og = (gates[:, g:g + 1] * o_cmp[sl] + gates[:, NS_GROUP + g:NS_GROUP + g + 1] * o_sel[sl]
              + gates[:, 2 * NS_GROUP + g:2 * NS_GROUP + g + 1] * o_win[sl])
        outs.append(og)
    o_ref[0] = jnp.concatenate(outs, axis=-1)


def nsa_bias_tiles(rel_bias, n_tiles):
    tab = rel_bias.astype(jnp.float32)
    iq = jnp.arange(TQ)[:, None]
    ik = jnp.arange(TQ)[None, :]
    dl = jnp.arange(n_tiles)[:, None, None]
    bsel = jnp.transpose(tab[rel_bucket(dl * TQ + iq - ik)], (3, 0, 1, 2))
    bcmp = jnp.transpose(tab[rel_bucket(dl * TQ + iq - (ik * CMP_STRIDE + CMP_LEN - 1))], (3, 0, 1, 2))
    return bsel, bcmp


def nsa_prompt_attention(pn, kcmp, vcmp, bsel, bcmp):
    B, T, _ = pn.shape
    n_tiles = T // TQ
    n_cmp = kcmp.shape[1]
    assert T % TQ == 0 and n_cmp <= TQ
    padc = ((0, 0), (0, 0), (0, TQ - n_cmp), (0, 0))
    kc = jnp.pad(jnp.transpose(kcmp, (0, 2, 1, 3)), padc)
    vc = jnp.pad(jnp.transpose(vcmp, (0, 2, 1, 3)), padc)
    glog = pn[..., NS_W + 6 * NS_KVW:].reshape(B, T, 3, NS_KV, NS_GROUP)
    glog = jnp.transpose(glog, (0, 3, 1, 2, 4)).reshape(B, NS_KV, T, 3 * NS_GROUP)
    kv0 = NS_W // NS_HD

    def kv_spec(slot):
        return pl.BlockSpec((1, T, NS_HD), lambda b, k, i, s=slot: (b, 0, kv0 + s * NS_KV + k))

    R = NS_GROUP * TQ
    return pl.pallas_call(
        functools.partial(_nsa_prompt_body, n_tiles=n_tiles),
        grid=(B, NS_KV, n_tiles),
        in_specs=[
            pl.BlockSpec((1, TQ, NS_GROUP * NS_HD), lambda b, k, i: (b, i, k)),
            kv_spec(2), kv_spec(3), kv_spec(4), kv_spec(5),
            pl.BlockSpec((1, 1, TQ, 3 * NS_GROUP), lambda b, k, i: (b, k, i, 0)),
            pl.BlockSpec((1, 1, TQ, NS_HD), lambda b, k, i: (b, k, 0, 0)),
            pl.BlockSpec((1, 1, TQ, NS_HD), lambda b, k, i: (b, k, 0, 0)),
            pl.BlockSpec((NS_GROUP, n_tiles, TQ, TQ), lambda b, k, i: (k, 0, 0, 0)),
            pl.BlockSpec((NS_GROUP, 1, TQ, TQ), lambda b, k, i: (k, i, 0, 0)),
        ],
        out_specs=pl.BlockSpec((1, TQ, NS_GROUP * NS_HD), lambda b, k, i: (b, i, k)),
        out_shape=jax.ShapeDtypeStruct((B, T, NS_W), jnp.float32),
        scratch_shapes=[pltpu.VMEM((R, 1), jnp.float32), pltpu.VMEM((R, 1), jnp.float32),
                        pltpu.VMEM((R, NS_HD), jnp.float32), pltpu.VMEM((n_tiles, TQ, TQ), jnp.float32)],
        compiler_params=pltpu.CompilerParams(
            dimension_semantics=("parallel", "parallel", "arbitrary"),
            vmem_limit_bytes=V7X_VMEM_LIMIT_BYTES),
        name="nsa_prompt_attention",
    )(pn, pn, pn, pn, pn, glog, kc, vc, bsel, bcmp)


def nsa_prompt(p, phi_k1, phi_k2, phi_v1, phi_v2, bsel, bcmp):
    B, T, _ = p.shape
    kv = p[..., NS_W:NS_W + 6 * NS_KVW].reshape(B, T, 6, NS_KV, NS_HD)
    kcmp = nsa_compress(kv[:, :, 0], phi_k1, phi_k2)
    vcmp = nsa_compress(kv[:, :, 1], phi_v1, phi_v2)
    o = nsa_prompt_attention(p, kcmp, vcmp, bsel, bcmp)
    wl = min(WINDOW, T)
    return o, kv[:, :, :4], kv[:, T - wl:, 4:]


def nsa_sample(p, cache_kv_l, page_table, win_buf, phi_k1, phi_k2, phi_v1, phi_v2, rel_bias):
    q, kv, gates = nsa_project(p)
    DB, Tn = q.shape[:2]
    past = cache_kv_l[page_table]
    past = past.reshape(DB, -1, 4, NS_KV, NS_HD)
    P = past.shape[1]
    ctx = jnp.concatenate([past, kv[:, :, :4].astype(past.dtype)], axis=1)
    kcmp = nsa_compress(ctx[:, :, 0], phi_k1, phi_k2)
    vcmp = nsa_compress(ctx[:, :, 1], phi_v1, phi_v2)
    kb = nsa_sel_blocks(ctx[:, :, 2])
    vb = nsa_sel_blocks(ctx[:, :, 3])
    Wb = win_buf.shape[1]
    win = jnp.concatenate([win_buf, kv[:, :, 4:].astype(win_buf.dtype)], axis=1)
    t = P + jnp.arange(Tn)
    pw = P - Wb + jnp.arange(Wb + Tn)
    o = nsa_attend(q, t, gates, kcmp, vcmp, kb, vb, win[:, :, 0], win[:, :, 1], pw, rel_bias)
    return o, kv[:, :, :4], win[:, Tn:]


PEER_ROUTE_TM = 128


def gelu_erf(x):
    return 0.5 * x * (1.0 + lax.erf(x * (2.0 ** -0.5)))


def _top_rows(work, n_rows, k, row_iota):
    vals, idxs = [], []
    for _ in range(k):
        m = jnp.max(work, axis=0, keepdims=True)
        idx = jnp.min(jnp.where(work == m, row_iota, n_rows), axis=0, keepdims=True)
        vals.append(m)
        idxs.append(idx)
        work = jnp.where(row_iota == idx, -jnp.inf, work)
    return vals, idxs


def _peer_route_body(q_ref, k1_ref, k2_ref, g_ref, i1_s, i2_s, w_s):
    f32, bf16 = jnp.float32, jnp.bfloat16
    tm = q_ref.shape[0]
    half = PEER_DKEY // 2
    K = PEER_TOPK
    rows = lax.broadcasted_iota(jnp.int32, (PEER_NKEYS, tm), 0)
    crow = lax.broadcasted_iota(jnp.int32, (K * K, tm), 0)
    k1 = k1_ref[...].astype(bf16)
    k2 = k2_ref[...].astype(bf16)

    def nt_dot(a, b):
        return lax.dot_general(a, b, (((1,), (1,)), ((), ())), preferred_element_type=f32)

    for h in range(PEER_HEADS):
        q1 = q_ref[:, h * PEER_DKEY:h * PEER_DKEY + half].astype(bf16)
        q2 = q_ref[:, h * PEER_DKEY + half:(h + 1) * PEER_DKEY].astype(bf16)
        v1, i1 = _top_rows(nt_dot(k1, q1), PEER_NKEYS, K, rows)
        v2, i2 = _top_rows(nt_dot(k2, q2), PEER_NKEYS, K, rows)
        v2m = jnp.concatenate(v2, axis=0)
        i2m = jnp.concatenate(i2, axis=0)
        cand = jnp.concatenate([v1[a] + v2m for a in range(K)], axis=0)
        cidx = jnp.concatenate([i1[a] * PEER_NKEYS + i2m for a in range(K)], axis=0)
        sv, pos = _top_rows(cand, K * K, K, crow)
        eidx = [jnp.max(jnp.where(crow == pos[k], cidx, 0), axis=0, keepdims=True) for k in range(K)]
        svm = jnp.concatenate(sv, axis=0)
        em = jnp.concatenate(eidx, axis=0)
        e = jnp.exp(svm - svm[0:1])
        gw = e / jnp.sum(e, axis=0, keepdims=True)
        i1_s[h * K:(h + 1) * K, :] = (em // PEER_NKEYS).astype(f32)
        i2_s[h * K:(h + 1) * K, :] = (em % PEER_NKEYS).astype(f32)
        w_s[h * K:(h + 1) * K, :] = gw
    i1_s[...] = i1_s[...].T
    i2_s[...] = i2_s[...].T
    w_s[...] = w_s[...].T
    sub = lax.broadcasted_iota(jnp.int32, (PEER_NKEYS, PEER_HEADS * K), 0).astype(f32)

    def per_token(t, carry):
        a_w = jnp.where(sub == i1_s[pl.ds(t, 1), :], w_s[pl.ds(t, 1), :], 0.0).astype(bf16)
        b_1 = jnp.where(sub == i2_s[pl.ds(t, 1), :], 1.0, 0.0).astype(bf16)
        g_ref[t] = nt_dot(a_w, b_1).astype(g_ref.dtype)
        return carry

    lax.fori_loop(0, tm, per_token, 0)


def peer_route(q, k1, k2):
    n = q.shape[0]
    tm = PEER_ROUTE_TM
    S = PEER_HEADS * PEER_TOPK
    assert n % tm == 0 and S == tm
    g = pl.pallas_call(
        _peer_route_body,
        grid=(n // tm,),
        in_specs=[pl.BlockSpec((tm, PEER_HEADS * PEER_DKEY), lambda i: (i, 0)),
                  pl.BlockSpec((PEER_NKEYS, PEER_DKEY // 2), lambda i: (0, 0)),
                  pl.BlockSpec((PEER_NKEYS, PEER_DKEY // 2), lambda i: (0, 0))],
        out_specs=pl.BlockSpec((tm, PEER_NKEYS, PEER_NKEYS), lambda i: (i, 0, 0)),
        out_shape=jax.ShapeDtypeStruct((n, PEER_NKEYS, PEER_NKEYS), jnp.bfloat16),
        scratch_shapes=[pltpu.VMEM((S, tm), jnp.float32)] * 3,
        compiler_params=pltpu.CompilerParams(dimension_semantics=("parallel",),
                                             vmem_limit_bytes=V7X_VMEM_LIMIT_BYTES),
        name="peer_route",
    )(q, k1, k2)
    return g.reshape(n, PEER_EXPERTS)


def _peer_expert_body(x_ref, g_ref, u_ref, v_ref, o_ref):
    f32, bf16 = jnp.float32, jnp.bfloat16
    e = pl.program_id(1)
    h = lax.dot_general(x_ref[...], u_ref[...], (((1,), (1,)), ((), ())), preferred_element_type=f32)
    p = (g_ref[...].astype(f32) * gelu_erf(h)).astype(bf16)
    upd = jnp.dot(p, v_ref[...], preferred_element_type=f32)

    @pl.when(e == 0)
    def _():
        o_ref[...] = upd

    @pl.when(e > 0)
    def _():
        o_ref[...] += upd


def peer_experts(x, g, u, v, tm=512, te=512):
    n, D = x.shape
    E = u.shape[0]
    tm = min(tm, n)
    assert n % tm == 0 and E % te == 0
    return pl.pallas_call(
        _peer_expert_body,
        grid=(n // tm, E // te),
        in_specs=[pl.BlockSpec((tm, D), lambda i, e: (i, 0)),
                  pl.BlockSpec((tm, te), lambda i, e: (i, e)),
                  pl.BlockSpec((te, D), lambda i, e: (e, 0)),
                  pl.BlockSpec((te, D), lambda i, e: (e, 0))],
        out_specs=pl.BlockSpec((tm, D), lambda i, e: (i, 0)),
        out_shape=jax.ShapeDtypeStruct((n, D), jnp.float32),
        compiler_params=pltpu.CompilerParams(dimension_semantics=("parallel", "arbitrary"),
                                             vmem_limit_bytes=V7X_VMEM_LIMIT_BYTES),
        name="peer_experts",
    )(x, g, u, v)


def peer_ffn(x, wq, k1, k2, u_bf, v_bf):
    Bx, T, D = x.shape
    n = Bx * T
    pad = -n % PEER_ROUTE_TM
    xt = jnp.pad(x.reshape(n, D), ((0, pad), (0, 0)))
    g = peer_route(matmul(xt, wq), k1, k2)
    out = peer_experts(xt.astype(jnp.bfloat16), g, u_bf, v_bf)
    return out[:n].reshape(Bx, T, D).astype(x.dtype)


def residual_block(x, mix, w_out, ln1_g, ln1_b, ln2_g, ln2_b, peer_wq, peer_k1, peer_k2, peer_u, peer_v):
    x = layer_norm(ALPHA * x + matmul3(mix, w_out), ln1_g, ln1_b)
    return layer_norm(ALPHA * x + peer_ffn(x, peer_wq, peer_k1, peer_k2, peer_u, peer_v), ln2_g, ln2_b)


def kernel(x_prompt, x_sample, cache_kv, cache_win, state_rwkv, state_rwkv_shift, state_gdn, state_gdn_conv, page_table, w_in, w_out, ln1_g, ln1_b, ln2_g, ln2_b, rw_mu, rw_w0, rw_w_up, rw_a0, rw_a_up, rw_g_up, rw_k_k, rw_k_a, rw_r_k, rw_ln_g, rw_ln_b, gd_conv_w, gd_a_log, gd_dt_bias, gd_norm_g, ns_phi_k1, ns_phi_k2, ns_phi_v1, ns_phi_v2, rel_bias, peer_wq, peer_k1, peer_k2, peer_u, peer_v):
    xp, xs = x_prompt, x_sample
    B = xp.shape[0]
    o_b = RW_COLS
    o_c = RW_COLS + GD_COLS
    bsel, bcmp = nsa_bias_tiles(rel_bias, SEQ // TQ)
    kv_p, kv_s, win_p, win_s, rw_p, rw_s, sh_p, sh_s, gd_p, gd_s, cv_p, cv_s = ([] for _ in range(12))
    for l in range(DEPTH):
        rw = (rw_mu[l], rw_w0[l], rw_w_up[l], rw_a0[l], rw_a_up[l], rw_g_up[l], rw_k_k[l], rw_k_a[l], rw_r_k[l], rw_ln_g[l], rw_ln_b[l])
        gd = (gd_conv_w[l], gd_a_log[l], gd_dt_bias[l], gd_norm_g[l])
        phi = (ns_phi_k1[l], ns_phi_k2[l], ns_phi_v1[l], ns_phi_v2[l])
        tail = (w_out[l], ln1_g[l], ln1_b[l], ln2_g[l], ln2_b[l], peer_wq[l], peer_k1[l], peer_k2[l],
                peer_u[l].astype(jnp.bfloat16), peer_v[l].astype(jnp.bfloat16))
        pp = matmul3(xp, w_in[l])
        a, sh, rs = rwkv7_mixer(pp[..., :o_b], jnp.zeros((B, RW_COLS), pp.dtype), jnp.zeros((B, RW_HEADS, RW_HD, RW_HD), jnp.float32), *rw)
        b, cv, gs = gated_deltanet_mixer(pp[..., o_b:o_c], jnp.zeros((B, GD_CONV - 1, GD_QKV), pp.dtype), jnp.zeros((B, GD_HEADS, GD_HD, GD_HD), jnp.float32), *gd)
        c, kvr, wr = nsa_prompt(pp[..., o_c:], *phi, bsel, bcmp)
        xp = residual_block(xp, jnp.concatenate([a, b, c.astype(a.dtype)], -1), *tail)
        kv_p.append(kvr)
        win_p.append(wr)
        rw_p.append(rs)
        sh_p.append(sh)
        gd_p.append(gs)
        cv_p.append(cv)
        ps = matmul3(xs, w_in[l])
        a, sh, rs = rwkv7_mixer(ps[..., :o_b], state_rwkv_shift[l], state_rwkv[l], *rw)
        b, cv, gs = gated_deltanet_mixer(ps[..., o_b:o_c], state_gdn_conv[l], state_gdn[l], *gd)
        c, kvr, wr = nsa_sample(ps[..., o_c:], cache_kv[l], page_table, cache_win[l], *phi, rel_bias)
        xs = residual_block(xs, jnp.concatenate([a, b, c.astype(a.dtype)], -1), *tail)
        kv_s.append(kvr)
        win_s.append(wr)
        rw_s.append(rs)
        sh_s.append(sh)
        gd_s.append(gs)
        cv_s.append(cv)
    st = jnp.stack
    return (xp, xs, st(kv_p), st(kv_s), st(win_p), st(win_s), st(rw_p), st(rw_s), st(sh_p), st(sh_s), st(gd_p), st(gd_s), st(cv_p), st(cv_s))
```

```python
import functools
import math

import jax
import jax.numpy as jnp
from jax import lax
from jax.experimental import pallas as pl
from jax.experimental.pallas import tpu as pltpu

D_MODEL = 4096
BATCH = 4
SEQ = 2048
DEPTH = 4
DEC_BATCH = 8
DEC_SEQ = 8
PAST_LEN = 8192
PAGE_SIZE = 128

ALPHA = (2 * DEPTH) ** 0.25
LN_EPS = 1e-5

RW_HD = 64
RW_W = D_MODEL // 4
RW_HEADS = RW_W // RW_HD
RW_DECAY_R = 64
RW_AAA_R = 64
RW_GATE_R = 160
RW_COLS = 3 * RW_W + RW_DECAY_R + RW_AAA_R + RW_GATE_R
RW_GN_EPS = 64e-5

GD_HD = 128
GD_W = D_MODEL // 4
GD_HEADS = GD_W // GD_HD
GD_QKV = 3 * GD_W
GD_CONV = 4
GD_CHUNK = 64
GD_COLS = GD_QKV + GD_W + 2 * GD_HEADS

NS_HD = 128
NS_W = D_MODEL // 2
NS_HEADS = NS_W // NS_HD
NS_KV = 4
NS_GROUP = NS_HEADS // NS_KV
NS_KVW = NS_KV * NS_HD
NS_COLS = NS_W + 6 * NS_KVW + 3 * NS_HEADS
CMP_LEN = 32
CMP_STRIDE = 16
CMP_HIDDEN = 128
SEL_BLOCK = 64
SEL_TOPK = 16
WINDOW = 512
NS_QBLOCK = 32
FORCE_SCORE = 1e4

REL_BUCKETS = 32
REL_MAX_DIST = 1024

D_MIX = RW_W + GD_W + NS_W
IN_COLS = RW_COLS + GD_COLS + NS_COLS

PEER_HEADS = 8
PEER_NKEYS = 128
PEER_EXPERTS = PEER_NKEYS ** 2
PEER_DKEY = 256
PEER_TOPK = 16
PEER_TBLOCK = 128

V7X_VMEM_LIMIT_BYTES = 56 * 1024 * 1024


def _matmul_body(x_ref, w_ref, o_ref):
    o_ref[...] = jnp.dot(x_ref[...].astype(jnp.bfloat16), w_ref[...].astype(jnp.bfloat16),
                         preferred_element_type=jnp.float32)


def _pick_tile(n, target):
    t = min(n, target)
    while n % t:
        t //= 2
    return t


def matmul(x, w, tm=512, tn=512):
    M, K = x.shape
    N = w.shape[1]
    tm = _pick_tile(M, tm)
    tn = min(tn, N)
    return pl.pallas_call(
        _matmul_body,
        grid=(M // tm, pl.cdiv(N, tn)),
        in_specs=[pl.BlockSpec((tm, K), lambda i, j: (i, 0)),
                  pl.BlockSpec((K, tn), lambda i, j: (0, j))],
        out_specs=pl.BlockSpec((tm, tn), lambda i, j: (i, j)),
        out_shape=jax.ShapeDtypeStruct((M, N), jnp.float32),
        compiler_params=pltpu.CompilerParams(
            dimension_semantics=("parallel", "parallel"),
            vmem_limit_bytes=V7X_VMEM_LIMIT_BYTES),
        name="proj_matmul",
    )(x, w)


def matmul3(x, w):
    B, T, K = x.shape
    return matmul(x.reshape(B * T, K), w).reshape(B, T, -1)


def layer_norm(x, g, b):
    xf = x.astype(jnp.float32)
    mu = jnp.mean(xf, -1, keepdims=True)
    var = jnp.mean(jnp.square(xf - mu), -1, keepdims=True)
    return ((xf - mu) * lax.rsqrt(var + LN_EPS) * g + b).astype(x.dtype)


def l2_normalize(x):
    return x / jnp.maximum(jnp.sqrt(jnp.sum(x * x, -1, keepdims=True)), 1e-12)


def masked_softmax(s, mask):
    s = jnp.where(mask, s, -jnp.inf)
    m = jnp.max(s, -1, keepdims=True)
    m = jnp.where(jnp.isfinite(m), m, 0.0)
    e = jnp.where(mask, jnp.exp(s - m), 0.0)
    return e / jnp.maximum(jnp.sum(e, -1, keepdims=True), 1e-30)


def rel_bucket(d):
    d = jnp.maximum(d, 0)
    exact = REL_BUCKETS // 2
    logd = jnp.log(jnp.maximum(d, 1).astype(jnp.float32) / exact) / math.log(REL_MAX_DIST / exact)
    large = jnp.minimum(exact + (logd * (REL_BUCKETS - exact)).astype(jnp.int32), REL_BUCKETS - 1)
    return jnp.where(d < exact, d, large)


RW_CHUNK = 64


def _split(x):
    hi = x.astype(jnp.bfloat16)
    lo = (x - hi.astype(jnp.float32)).astype(jnp.bfloat16)
    return hi, lo


def _dot3(a, b, dims=(((1,), (0,)), ((), ()))):
    ah, al = _split(a)
    bh, bl = _split(b)
    d = lambda x, y: lax.dot_general(x, y, dims, preferred_element_type=jnp.float32)
    return d(ah, bh) + (d(ah, bl) + d(al, bh))


_NT = (((1,), (1,)), ((), ()))
_TN = (((0,), (0,)), ((), ()))


def _rwkv_chunk_body(r_ref, lw_ref, k_ref, v_ref, kk_ref, ka_ref, s0_ref, y_ref, s_ref, *, n_chunks, n_valid, n_heads):
    f32 = jnp.float32
    N, C = RW_HD, RW_CHUNK
    c_idx = pl.program_id(2)

    @pl.when(c_idx == 0)
    def _():
        s_ref[...] = s0_ref[...]

    row = lax.broadcasted_iota(jnp.int32, (C, C), 0)
    col = lax.broadcasted_iota(jnp.int32, (C, C), 1)
    tril = (row >= col).astype(f32)
    strict = row > col

    hs = range(n_heads)
    for ci in range(n_chunks):
        rows = slice(ci * C, (ci + 1) * C)
        padded = n_valid < n_chunks * C
        live = (lax.broadcasted_iota(jnp.int32, (C, N), 0) + ci * C) < n_valid

        def ld(ref, h):
            x = ref[0, rows, h * N:(h + 1) * N]
            return jnp.where(live, x, 0.0) if padded else x

        lw = [ld(lw_ref, h) for h in hs]
        G = [_dot3(tril, lw[h]) for h in hs]
        eg = [jnp.exp(G[h]) for h in hs]
        ing = [jnp.exp(-G[h]) for h in hs]
        ar = [jnp.concatenate([-ld(kk_ref, h) * jnp.exp(G[h] - lw[h]), ld(r_ref, h) * eg[h]], axis=0) for h in hs]
        bk = [jnp.concatenate([ld(ka_ref, h) * ing[h], ld(k_ref, h) * ing[h]], axis=0) for h in hs]
        v = [ld(v_ref, h) for h in hs]
        S0 = [s_ref[0, h] for h in hs]
        M = [_dot3(ar[h], bk[h], _NT) for h in hs]
        AS = [_dot3(ar[h], S0[h], _NT) for h in hs]
        P = [jnp.where(strict, M[h][0:C, 0:C], 0.0) for h in hs]
        M2 = [jnp.where(strict, M[h][0:C, C:2 * C], 0.0) for h in hs]
        M34 = [jnp.concatenate([M[h][C:2 * C, 0:C] * tril, M[h][C:2 * C, C:2 * C] * tril], axis=1) for h in hs]
        rhs = [AS[h][0:C] + _dot3(M2[h], v[h]) for h in hs]
        X = [rhs[h] + _dot3(P[h], rhs[h]) for h in hs]
        for _ in range(5):
            P = [_dot3(P[h], P[h]) for h in hs]
            X = [X[h] + _dot3(P[h], X[h]) for h in hs]
        sav = [jnp.concatenate([X[h], v[h]], axis=0) for h in hs]
        for h in hs:
            y_ref[0, rows, h * N:(h + 1) * N] = AS[h][C:2 * C] + _dot3(M34[h], sav[h])
        dS = [_dot3(sav[h], bk[h], _TN) for h in hs]
        for h in hs:
            s_ref[0, h] = (S0[h] + dS[h]) * eg[h][C - 1:C, :]


def rwkv_scan_chunked(r, lw, k, v, kk, ka, s0, chunks_per_step=2, heads_per_step=8):
    B, T, W = r.shape
    N, C = RW_HD, RW_CHUNK
    H = W // N
    HB = min(heads_per_step, H)
    TB = C * chunks_per_step if T >= C * chunks_per_step else -(-T // C) * C
    Tp = -(-T // TB) * TB
    n_valid = T if Tp != T else TB

    def prep(x):
        return jnp.pad(x, ((0, 0), (0, Tp - T), (0, 0))) if Tp != T else x

    ins = [prep(x) for x in (r, lw, k, v, kk, ka)]
    seq = pl.BlockSpec((1, TB, HB * N), lambda b, p, c: (b, c, p))
    st = pl.BlockSpec((1, HB, N, N), lambda b, p, c: (b, p, 0, 0))
    y, sT = pl.pallas_call(
        functools.partial(_rwkv_chunk_body, n_chunks=TB // C, n_valid=n_valid, n_heads=HB),
        grid=(B, H // HB, Tp // TB),
        in_specs=[seq] * 6 + [st],
        out_specs=[seq, st],
        out_shape=[jax.ShapeDtypeStruct((B, Tp, W), jnp.float32), jax.ShapeDtypeStruct((B, H, N, N), jnp.float32)],
        compiler_params=pltpu.CompilerParams(dimension_semantics=("parallel", "parallel", "arbitrary"),
                                             vmem_limit_bytes=V7X_VMEM_LIMIT_BYTES),
        name="rwkv_chunked",
    )(*ins, s0)
    return y[:, :T], sT


def rwkv7_mixer(p, shift0, s0, mu, w0, w_up, a0, a_up, g_up, k_k, k_a, r_k, ln_g, ln_b):
    f32 = jnp.float32
    B, T, _ = p.shape
    prev = jnp.concatenate([shift0[:, None].astype(p.dtype), p[:, :-1]], axis=1)
    m = p + mu * (prev - p)
    r = m[..., :RW_W]
    k = m[..., RW_W:2 * RW_W]
    v = m[..., 2 * RW_W:3 * RW_W]
    o = 3 * RW_W
    wl = m[..., o:o + RW_DECAY_R]
    o += RW_DECAY_R
    al = m[..., o:o + RW_AAA_R]
    o += RW_AAA_R
    gl = m[..., o:o + RW_GATE_R]
    w = -jax.nn.softplus(-(w0 + jnp.tanh(wl) @ w_up).astype(f32)) - 0.5
    log_decay = -jnp.exp(w)
    a = jax.nn.sigmoid((a0 + al @ a_up).astype(f32))
    g = jax.nn.sigmoid(gl) @ g_up

    def heads(t):
        return t.reshape(B, T, RW_HEADS, RW_HD).astype(f32)

    kk = l2_normalize(heads(k * k_k)).reshape(B, T, RW_W)
    k = k * (1.0 + (a - 1.0) * k_a)
    r_, k_, v_ = heads(r), heads(k), heads(v)

    y, sT = rwkv_scan_chunked(r, log_decay, k, v, kk, kk * a, s0.astype(f32))
    y = heads(y)
    ym = jnp.mean(y, -1, keepdims=True)
    yv = jnp.mean(jnp.square(y - ym), -1, keepdims=True)
    y = ((y - ym) * lax.rsqrt(yv + RW_GN_EPS)).reshape(B, T, RW_W) * ln_g + ln_b
    bonus = jnp.sum(r_ * k_ * r_k, -1, keepdims=True) * v_
    y = (y + bonus.reshape(B, T, RW_W)) * g
    return y.astype(p.dtype), p[:, -1], sT


def _dot1(a, b, dims=(((1,), (0,)), ((), ()))):
    return lax.dot_general(a.astype(jnp.bfloat16), b.astype(jnp.bfloat16), dims, preferred_element_type=jnp.float32)


def _gdn_body(q_ref, k_ref, v_ref, beta_ref, g_ref, s0_ref, o_ref, s_ref, *, n_chunks, n_valid, n_heads):
    f32 = jnp.float32
    D, C = GD_HD, GD_CHUNK
    hg = pl.program_id(1)
    c_idx = pl.program_id(2)

    @pl.when(c_idx == 0)
    def _():
        s_ref[...] = s0_ref[...]

    row = lax.broadcasted_iota(jnp.int32, (C, C), 0)
    col = lax.broadcasted_iota(jnp.int32, (C, C), 1)
    tri = row >= col
    trif = tri.astype(f32)
    strict = row > col
    eye = row == col
    ones = jnp.ones((C, C), f32)
    lane_h = lax.broadcasted_iota(jnp.int32, (C, beta_ref.shape[2]), 1)
    hs = range(n_heads)
    for ci in range(n_chunks):
        rows = slice(ci * C, (ci + 1) * C)
        padded = n_valid < n_chunks * C
        live = (lax.broadcasted_iota(jnp.int32, (C, 1), 0) + ci * C) < n_valid

        def ld(ref, h):
            x = ref[0, rows, h * D:(h + 1) * D]
            return jnp.where(live, x, 0.0) if padded else x

        g_all = g_ref[0, rows, :]
        b_all = beta_ref[0, rows, :]
        if padded:
            g_all = jnp.where(live, g_all, 0.0)
            b_all = jnp.where(live, b_all, 0.0)
        G_all = _dot3(trif, g_all)

        def colof(x, h):
            return jnp.sum(jnp.where(lane_h == hg * n_heads + h, x, 0.0), axis=1, keepdims=True)

        Gc = [colof(G_all, h) for h in hs]
        bc = [colof(b_all, h) for h in hs]
        GB = [jnp.broadcast_to(Gc[h], (C, C)) for h in hs]
        GR = [_dot3(ones, jnp.where(eye, GB[h], 0.0)) for h in hs]
        decay = [jnp.where(tri, jnp.exp(jnp.where(tri, GB[h] - GR[h], 0.0)), 0.0) for h in hs]
        eG = [jnp.exp(Gc[h]) for h in hs]
        Glast = [Gc[h][C - 1:C, :] for h in hs]
        q = [ld(q_ref, h) for h in hs]
        k = [ld(k_ref, h) for h in hs]
        v = [ld(v_ref, h) for h in hs]
        kb = [k[h] * bc[h] for h in hs]
        P = [-jnp.where(strict, _dot1(kb[h], k[h], _NT) * decay[h], 0.0) for h in hs]
        Aqk = [_dot1(q[h], k[h], _NT) * decay[h] for h in hs]
        rhs = [jnp.concatenate([v[h] * bc[h], kb[h] * eG[h]], axis=1) for h in hs]
        X = [rhs[h] + _dot3(P[h], rhs[h]) for h in hs]
        for _ in range(5):
            P = [_dot3(P[h], P[h]) for h in hs]
            X = [X[h] + _dot3(P[h], X[h]) for h in hs]
        S = [s_ref[0, h] for h in hs]
        vn = [X[h][:, 0:D] - _dot1(X[h][:, D:2 * D], S[h]) for h in hs]
        for h in hs:
            o_ref[0, rows, h * D:(h + 1) * D] = _dot1(q[h] * eG[h], S[h]) + _dot1(Aqk[h], vn[h])
        for h in hs:
            s_ref[0, h] = S[h] * jnp.exp(Glast[h]) + _dot1(k[h] * jnp.exp(Glast[h] - Gc[h]), vn[h], _TN)


def gdn_chunked(qkv, beta, g, s0, chunks_per_step=2, heads_per_step=4):
    B, T, W3 = qkv.shape
    D, C = GD_HD, GD_CHUNK
    H = W3 // (3 * D)
    HB = min(heads_per_step, H)
    TB = C * chunks_per_step if T >= C * chunks_per_step else -(-T // C) * C
    Tp = -(-T // TB) * TB
    n_valid = T if Tp != T else TB
    if Tp != T:
        padt = ((0, 0), (0, Tp - T), (0, 0))
        qkv, beta, g = jnp.pad(qkv, padt), jnp.pad(beta, padt), jnp.pad(g, padt)
    ng = H // HB

    def seq(part):
        return pl.BlockSpec((1, TB, HB * D), lambda b, p, c, part=part: (b, c, part * ng + p))

    sc = pl.BlockSpec((1, TB, H), lambda b, p, c: (b, c, 0))
    st = pl.BlockSpec((1, HB, D, D), lambda b, p, c: (b, p, 0, 0))
    o, sT = pl.pallas_call(
        functools.partial(_gdn_body, n_chunks=TB // C, n_valid=n_valid, n_heads=HB),
        grid=(B, ng, Tp // TB),
        in_specs=[seq(0), seq(1), seq(2), sc, sc, st],
        out_specs=[pl.BlockSpec((1, TB, HB * D), lambda b, p, c: (b, c, p)), st],
        out_shape=[jax.ShapeDtypeStruct((B, Tp, H * D), jnp.float32), jax.ShapeDtypeStruct((B, H, D, D), jnp.float32)],
        compiler_params=pltpu.CompilerParams(dimension_semantics=("parallel", "parallel", "arbitrary"),
                                             vmem_limit_bytes=V7X_VMEM_LIMIT_BYTES),
        name="gdn_chunked",
    )(qkv, qkv, qkv, beta, g, s0)
    return o[:, :T], sT


def gated_deltanet_mixer(p, conv0, s0, conv_w, a_log, dt_bias, norm_g):
    f32 = jnp.float32
    B, T, _ = p.shape
    qkv = p[..., :GD_QKV]
    z = p[..., GD_QKV:GD_QKV + GD_W]
    bl = p[..., GD_QKV + GD_W:GD_QKV + GD_W + GD_HEADS]
    al = p[..., GD_QKV + GD_W + GD_HEADS:]
    xc = jnp.concatenate([conv0.astype(p.dtype), qkv], axis=1)
    conv = xc[:, :T] * conv_w[0]
    for i in range(1, GD_CONV):
        conv = conv + xc[:, i:i + T] * conv_w[i]
    conv = jax.nn.silu(conv).astype(f32)
    q = l2_normalize(conv[..., :GD_W].reshape(B, T, GD_HEADS, GD_HD)) * GD_HD ** -0.5
    k = l2_normalize(conv[..., GD_W:2 * GD_W].reshape(B, T, GD_HEADS, GD_HD))
    v = conv[..., 2 * GD_W:].reshape(B, T, GD_HEADS, GD_HD)
    beta = jax.nn.sigmoid(bl.astype(f32))
    g = -jnp.exp(a_log.astype(f32)) * jax.nn.softplus((al + dt_bias).astype(f32))
    qkv_n = jnp.concatenate([q.reshape(B, T, GD_W), k.reshape(B, T, GD_W), conv[..., 2 * GD_W:]], axis=-1)
    o, sT = gdn_chunked(qkv_n, beta, g, s0.astype(f32))
    o = o.reshape(B, T, GD_HEADS, GD_HD)
    o = o * lax.rsqrt(jnp.mean(o * o, -1, keepdims=True) + 1e-6) * norm_g
    o = o.reshape(B, T, GD_W) * jax.nn.silu(z.astype(f32))
    return o.astype(p.dtype), xc[:, -(GD_CONV - 1):], sT


def nsa_project(p):
    B, T, _ = p.shape
    q = p[..., :NS_W].reshape(B, T, NS_HEADS, NS_HD)
    kv = p[..., NS_W:NS_W + 6 * NS_KVW].reshape(B, T, 6, NS_KV, NS_HD)
    gates = jax.nn.sigmoid(p[..., NS_W + 6 * NS_KVW:].astype(jnp.float32)).reshape(B, T, 3, NS_HEADS)
    return q, kv, gates


def nsa_compress(x, w1, w2):
    B, L = x.shape[:2]
    r = CMP_LEN // CMP_STRIDE
    n_sub = L // CMP_STRIDE
    n_cmp = n_sub - r + 1
    sub = x[:, :n_sub * CMP_STRIDE].reshape(B, n_sub, CMP_STRIDE, NS_KV, NS_HD)
    blk = jnp.concatenate([sub[:, i:i + n_cmp] for i in range(r)], axis=2)
    flat = jnp.moveaxis(blk, 3, 2).reshape(B, n_cmp, NS_KV, CMP_LEN * NS_HD)
    return jax.nn.gelu(flat @ w1, approximate=False) @ w2


def nsa_sel_blocks(x):
    B, L = x.shape[:2]
    ns = -(-L // SEL_BLOCK)
    x = jnp.pad(x, ((0, 0), (0, ns * SEL_BLOCK - L), (0, 0), (0, 0)))
    return jnp.transpose(x.reshape(B, ns, SEL_BLOCK, NS_KV, NS_HD), (0, 3, 1, 2, 4))


def nsa_attend(q, t, gates, kcmp, vcmp, kb, vb, kwin, vwin, pwin, rel_bias):
    f32 = jnp.float32
    B, Tq = q.shape[:2]
    qg = jnp.transpose(q.reshape(B, Tq, NS_KV, NS_GROUP, NS_HD), (0, 2, 3, 1, 4)).astype(f32) * NS_HD ** -0.5
    bias_tab = rel_bias.astype(f32).reshape(REL_BUCKETS, NS_KV, NS_GROUP)

    def shared_bias(d):
        return jnp.transpose(bias_tab[rel_bucket(d)], (2, 3, 0, 1))

    nc = kcmp.shape[1]
    ci = jnp.arange(nc)
    dc = t[:, None] - (ci * CMP_STRIDE + CMP_LEN - 1)[None, :]
    sc = jnp.einsum('bkgqd,bnkd->bkgqn', qg, kcmp.astype(f32)) + shared_bias(dc)
    pc = masked_softmax(sc, dc >= 0)
    o_cmp = jnp.einsum('bkgqn,bnkd->bkgqd', pc, vcmp.astype(f32))
    ns = kb.shape[2]
    j = jnp.arange(ns)
    overlap = ((ci[:, None] * CMP_STRIDE < (j[None, :] + 1) * SEL_BLOCK) & (ci[:, None] * CMP_STRIDE + CMP_LEN > j[None, :] * SEL_BLOCK)).astype(f32)
    ps = jnp.einsum('bkgqn,nj->bkqj', pc, overlap)
    cur = t // SEL_BLOCK
    valid = j[None, :] <= cur[:, None]
    forced = (j[None, :] == 0) | (j[None, :] == cur[:, None]) | (j[None, :] == cur[:, None] - 1)
    score = jnp.where(valid, jnp.where(forced, FORCE_SCORE, ps), -jnp.inf)
    nsel = min(SEL_TOPK, ns)
    top_s, top_i = lax.top_k(score, nsel)
    bi = jnp.arange(B)[:, None, None, None]
    ki = jnp.arange(NS_KV)[None, :, None, None]
    kg = kb[bi, ki, top_i].astype(f32)
    vg = vb[bi, ki, top_i].astype(f32)
    pos = top_i[..., None] * SEL_BLOCK + jnp.arange(SEL_BLOCK)
    ds = t[None, None, :, None, None] - pos
    ms = jnp.isfinite(top_s)[..., None] & (ds >= 0)
    sb = jnp.moveaxis(bias_tab[rel_bucket(ds), ki[..., None]], -1, 2)
    ss = jnp.einsum('bkgqd,bkqnld->bkgqnl', qg, kg) + sb
    m_all = nsel * SEL_BLOCK
    psl = masked_softmax(ss.reshape(B, NS_KV, NS_GROUP, Tq, m_all), ms[:, :, None].reshape(B, NS_KV, 1, Tq, m_all))
    o_slc = jnp.einsum('bkgqm,bkqmd->bkgqd', psl, vg.reshape(B, NS_KV, Tq, m_all, NS_HD))
    dw = t[:, None] - pwin[None, :]
    mw = (dw >= 0) & (dw <= WINDOW) & (pwin[None, :] >= 0)
    sw = jnp.einsum('bkgqd,blkd->bkgql', qg, kwin.astype(f32)) + shared_bias(dw)
    pw = masked_softmax(sw, mw)
    o_win = jnp.einsum('bkgql,blkd->bkgqd', pw, vwin.astype(f32))
    gt = jnp.moveaxis(gates, 1, -1).reshape(B, 3, NS_KV, NS_GROUP, Tq)[..., None]
    o = gt[:, 0] * o_cmp + gt[:, 1] * o_slc + gt[:, 2] * o_win
    return jnp.transpose(o, (0, 3, 1, 2, 4)).reshape(B, Tq, NS_W).astype(q.dtype)


TQ = 128
NEG_BIG = -1e30


def _stack_groups(x):
    return jnp.concatenate([x[:, g * NS_HD:(g + 1) * NS_HD] for g in range(NS_GROUP)], axis=0)


def _nsa_prompt_body(q_ref, ksel_ref, vsel_ref, kwin_ref, vwin_ref, gate_ref, kcmp_ref, vcmp_ref,
                     bsel_ref, bcmp_ref, o_ref, m_ref, l_ref, acc_ref, mask_ref, *, n_tiles):
    f32, bf16 = jnp.float32, jnp.bfloat16
    qi = pl.program_id(2)
    R = NS_GROUP * TQ
    qg = (_stack_groups(q_ref[0]) * NS_HD ** -0.5).astype(bf16)
    row = lax.broadcasted_iota(jnp.int32, (TQ, TQ), 0)
    col = lax.broadcasted_iota(jnp.int32, (TQ, TQ), 1)
    t_q = qi * TQ + row

    def nt_dot(a, b):
        return lax.dot_general(a, b, (((1,), (1,)), ((), ())), preferred_element_type=f32)

    def tile4(x):
        return jnp.concatenate([x] * NS_GROUP, axis=0)

    sc = nt_dot(qg, kcmp_ref[0, 0].astype(bf16))
    sc = sc + jnp.concatenate([bcmp_ref[g, 0] for g in range(NS_GROUP)], axis=0)
    okc = tile4((t_q - (col * CMP_STRIDE + CMP_LEN - 1)) >= 0)
    mc = jnp.max(jnp.where(okc, sc, NEG_BIG), axis=-1, keepdims=True)
    ec = jnp.where(okc, jnp.exp(sc - mc), 0.0)
    pc = ec / jnp.maximum(jnp.sum(ec, axis=-1, keepdims=True), 1e-30)
    o_cmp = jnp.dot(pc.astype(bf16), vcmp_ref[0, 0].astype(bf16), preferred_element_type=f32)

    pc_sum = pc[0:TQ] + pc[TQ:2 * TQ] + pc[2 * TQ:3 * TQ] + pc[3 * TQ:4 * TQ]
    overlap = ((row * CMP_STRIDE < (col + 1) * SEL_BLOCK) & (row * CMP_STRIDE + CMP_LEN > col * SEL_BLOCK)).astype(f32)
    ps = jnp.dot(pc_sum, overlap, preferred_element_type=f32, precision=lax.Precision.HIGHEST)
    cur = t_q // SEL_BLOCK
    ns = n_tiles * (TQ // SEL_BLOCK)
    valid = (col <= cur) & (col < ns)
    forced = (col == 0) | (col == cur) | (col == cur - 1)
    score = jnp.where(valid, jnp.where(forced, FORCE_SCORE, ps), -jnp.inf)
    rank = jnp.zeros((TQ, TQ), jnp.int32)
    for i in range(ns):
        ci = jnp.broadcast_to(score[:, i:i + 1], (TQ, TQ))
        beats = (ci > score) | ((ci == score) & (col > i))
        rank = rank + beats.astype(jnp.int32)
    sel = (valid & (rank < min(SEL_TOPK, ns))).astype(bf16)
    for kj in range(n_tiles):
        expand = (row == (kj * (TQ // SEL_BLOCK) + col // SEL_BLOCK)).astype(bf16)
        mask_ref[kj] = jnp.dot(sel, expand, preferred_element_type=f32)

    def attend(k_ref, v_ref, lo, kind):
        m_ref[...] = jnp.full((R, 1), NEG_BIG, f32)
        l_ref[...] = jnp.zeros((R, 1), f32)
        acc_ref[...] = jnp.zeros((R, NS_HD), f32)

        def step(kj, carry):
            off = pl.multiple_of(kj * TQ, TQ)
            kt = k_ref[0, pl.ds(off, TQ), :].astype(bf16)
            vt = v_ref[0, pl.ds(off, TQ), :].astype(bf16)
            delta = qi - kj
            s = nt_dot(qg, kt)
            s = s + jnp.concatenate([bsel_ref[g, delta] for g in range(NS_GROUP)], axis=0)
            d = delta * TQ + row - col
            if kind == "sel":
                ok = (d >= 0) & (mask_ref[kj] > 0.5)
            else:
                ok = (d >= 0) & (d <= WINDOW)
            ok = tile4(ok)
            m_old = m_ref[...]
            m_new = jnp.maximum(m_old, jnp.max(jnp.where(ok, s, NEG_BIG), axis=-1, keepdims=True))
            e = jnp.where(ok, jnp.exp(s - m_new), 0.0)
            scale = jnp.exp(m_old - m_new)
            l_ref[...] = l_ref[...] * scale + jnp.sum(e, axis=-1, keepdims=True)
            acc_ref[...] = acc_ref[...] * scale + jnp.dot(e.astype(bf16), vt, preferred_element_type=f32)
            m_ref[...] = m_new
            return carry

        lax.fori_loop(lo, qi + 1, step, 0)
        return acc_ref[...] / jnp.maximum(l_ref[...], 1e-30)

    o_sel = attend(ksel_ref, vsel_ref, 0, "sel")
    o_win = attend(kwin_ref, vwin_ref, jnp.maximum(qi - WINDOW // TQ, 0), "win")

    gates = jax.nn.sigmoid(gate_ref[0, 0].astype(f32))
    outs = []
    for g in range(NS_GROUP):
        sl = slice(g * TQ, (g + 1) * TQ)
        og = (gates[:, g:g + 1] * o_cmp[sl] + gates[:, NS_GROUP + g:NS_GROUP + g + 1] * o_sel[sl]
              + gates[:, 2 * NS_GROUP + g:2 * NS_GROUP + g + 1] * o_win[sl])
        outs.append(og)
    o_ref[0] = jnp.concatenate(outs, axis=-1)


def nsa_bias_tiles(rel_bias, n_tiles):
    tab = rel_bias.astype(jnp.float32)
    iq = jnp.arange(TQ)[:, None]
    ik = jnp.arange(TQ)[None, :]
    dl = jnp.arange(n_tiles)[:, None, None]
    bsel = jnp.transpose(tab[rel_bucket(dl * TQ + iq - ik)], (3, 0, 1, 2))
    bcmp = jnp.transpose(tab[rel_bucket(dl * TQ + iq - (ik * CMP_STRIDE + CMP_LEN - 1))], (3, 0, 1, 2))
    return bsel, bcmp


def nsa_prompt_attention(pn, kcmp, vcmp, bsel, bcmp):
    B, T, _ = pn.shape
    n_tiles = T // TQ
    n_cmp = kcmp.shape[1]
    assert T % TQ == 0 and n_cmp <= TQ
    padc = ((0, 0), (0, 0), (0, TQ - n_cmp), (0, 0))
    kc = jnp.pad(jnp.transpose(kcmp, (0, 2, 1, 3)), padc)
    vc = jnp.pad(jnp.transpose(vcmp, (0, 2, 1, 3)), padc)
    glog = pn[..., NS_W + 6 * NS_KVW:].reshape(B, T, 3, NS_KV, NS_GROUP)
    glog = jnp.transpose(glog, (0, 3, 1, 2, 4)).reshape(B, NS_KV, T, 3 * NS_GROUP)
    kv0 = NS_W // NS_HD

    def kv_spec(slot):
        return pl.BlockSpec((1, T, NS_HD), lambda b, k, i, s=slot: (b, 0, kv0 + s * NS_KV + k))

    R = NS_GROUP * TQ
    return pl.pallas_call(
        functools.partial(_nsa_prompt_body, n_tiles=n_tiles),
        grid=(B, NS_KV, n_tiles),
        in_specs=[
            pl.BlockSpec((1, TQ, NS_GROUP * NS_HD), lambda b, k, i: (b, i, k)),
            kv_spec(2), kv_spec(3), kv_spec(4), kv_spec(5),
            pl.BlockSpec((1, 1, TQ, 3 * NS_GROUP), lambda b, k, i: (b, k, i, 0)),
            pl.BlockSpec((1, 1, TQ, NS_HD), lambda b, k, i: (b, k, 0, 0)),
            pl.BlockSpec((1, 1, TQ, NS_HD), lambda b, k, i: (b, k, 0, 0)),
            pl.BlockSpec((NS_GROUP, n_tiles, TQ, TQ), lambda b, k, i: (k, 0, 0, 0)),
            pl.BlockSpec((NS_GROUP, 1, TQ, TQ), lambda b, k, i: (k, i, 0, 0)),
        ],
        out_specs=pl.BlockSpec((1, TQ, NS_GROUP * NS_HD), lambda b, k, i: (b, i, k)),
        out_shape=jax.ShapeDtypeStruct((B, T, NS_W), jnp.float32),
        scratch_shapes=[pltpu.VMEM((R, 1), jnp.float32), pltpu.VMEM((R, 1), jnp.float32),
                        pltpu.VMEM((R, NS_HD), jnp.float32), pltpu.VMEM((n_tiles, TQ, TQ), jnp.float32)],
        compiler_params=pltpu.CompilerParams(
            dimension_semantics=("parallel", "parallel", "arbitrary"),
            vmem_limit_bytes=V7X_VMEM_LIMIT_BYTES),
        name="nsa_prompt_attention",
    )(pn, pn, pn, pn, pn, glog, kc, vc, bsel, bcmp)


def nsa_prompt(p, phi_k1, phi_k2, phi_v1, phi_v2, bsel, bcmp):
    B, T, _ = p.shape
    kv = p[..., NS_W:NS_W + 6 * NS_KVW].reshape(B, T, 6, NS_KV, NS_HD)
    kcmp = nsa_compress(kv[:, :, 0], phi_k1, phi_k2)
    vcmp = nsa_compress(kv[:, :, 1], phi_v1, phi_v2)
    o = nsa_prompt_attention(p, kcmp, vcmp, bsel, bcmp)
    wl = min(WINDOW, T)
    return o, kv[:, :, :4], kv[:, T - wl:, 4:]


CMP_PAGE_GROUP = 8


def _nt_dot(a, b):
    return lax.dot_general(a, b, (((1,), (1,)), ((), ())), preferred_element_type=jnp.float32)


def _cmp_pages_body(pt_ref, page_ref, k1_ref, v1_ref, ab_ref, seq_ref):
    bf16 = jnp.bfloat16
    pg = pl.program_id(1)
    slot_in_group = pg % CMP_PAGE_GROUP
    row0 = pl.multiple_of(slot_in_group * PAGE_SIZE, PAGE_SIZE)
    for c in range(2 * NS_KV):
        seq_ref[c, pl.ds(row0, PAGE_SIZE), :] = page_ref[0, :, c * NS_HD:(c + 1) * NS_HD]

    @pl.when(slot_in_group == CMP_PAGE_GROUP - 1)
    def _():
        n_sub = CMP_PAGE_GROUP * PAGE_SIZE // CMP_STRIDE
        half = CMP_STRIDE * NS_HD
        for c in range(2 * NS_KV):
            w_ref = k1_ref if c < NS_KV else v1_ref
            a = jnp.zeros((n_sub, CMP_HIDDEN), jnp.float32)
            b = jnp.zeros((n_sub, CMP_HIDDEN), jnp.float32)
            for p in range(CMP_STRIDE):
                x = seq_ref[c, pl.ds(p, n_sub, stride=CMP_STRIDE), :].astype(bf16)
                a = a + jnp.dot(x, w_ref[p * NS_HD:(p + 1) * NS_HD, :].astype(bf16), preferred_element_type=jnp.float32)
                b = b + jnp.dot(x, w_ref[half + p * NS_HD:half + (p + 1) * NS_HD, :].astype(bf16), preferred_element_type=jnp.float32)
            ab_ref[0, c] = jnp.concatenate([a, b], axis=-1)


def nsa_compress_pages(cache_l, page_table, phi_k1, phi_v1):
    n_phys = cache_l.shape[0]
    DB, n_pages = page_table.shape
    assert n_pages % CMP_PAGE_GROUP == 0
    cache2 = cache_l.reshape(n_phys, PAGE_SIZE, 4 * NS_KVW)
    subs_per_group = CMP_PAGE_GROUP * PAGE_SIZE // CMP_STRIDE
    n_sub = n_pages * PAGE_SIZE // CMP_STRIDE
    grid_spec = pltpu.PrefetchScalarGridSpec(
        num_scalar_prefetch=1,
        grid=(DB, n_pages),
        in_specs=[pl.BlockSpec((1, PAGE_SIZE, 2 * NS_KVW), lambda b, g, pt: (pt[b, g], 0, 0)),
                  pl.BlockSpec((CMP_LEN * NS_HD, CMP_HIDDEN), lambda b, g, pt: (0, 0)),
                  pl.BlockSpec((CMP_LEN * NS_HD, CMP_HIDDEN), lambda b, g, pt: (0, 0))],
        out_specs=pl.BlockSpec((1, 2 * NS_KV, subs_per_group, 2 * CMP_HIDDEN),
                               lambda b, g, pt: (b, 0, g // CMP_PAGE_GROUP, 0)),
        scratch_shapes=[pltpu.VMEM((2 * NS_KV, CMP_PAGE_GROUP * PAGE_SIZE, NS_HD), jnp.float32)])
    return pl.pallas_call(
        _cmp_pages_body, grid_spec=grid_spec,
        out_shape=jax.ShapeDtypeStruct((DB, 2 * NS_KV, n_sub, 2 * CMP_HIDDEN), jnp.float32),
        compiler_params=pltpu.CompilerParams(dimension_semantics=("parallel", "arbitrary"),
                                             vmem_limit_bytes=V7X_VMEM_LIMIT_BYTES),
        name="nsa_compress_pages",
    )(page_table, cache2, phi_k1, phi_v1)


def _nsa_sample_body(pt_ref, q_ref, ksel_ref, vsel_ref, knew_ref, vnew_ref, win_ref, wnew_ref, gate_ref,
                     ab_ref, k2_ref, v2_ref, bsel_ref, bwin_ref, bcmp_ref, o_ref,
                     m_ref, l_ref, acc_ref, sel_ref, ocmp_ref, *, n_pages, n_new):
    f32, bf16 = jnp.float32, jnp.bfloat16
    pg = pl.program_id(1)
    QG = NS_GROUP * n_new
    R = NS_KV * QG
    n_sub = n_pages * PAGE_SIZE // CMP_STRIDE
    n_cmp = n_sub - 1
    ns = n_pages * (PAGE_SIZE // SEL_BLOCK) + 1
    NSP = sel_ref.shape[1]
    past = n_pages * PAGE_SIZE
    qs = (q_ref[0] * NS_HD ** -0.5).astype(bf16)

    @pl.when(pg == 0)
    def _():
        m_ref[...] = jnp.full((R, 1), NEG_BIG, f32)
        l_ref[...] = jnp.zeros((R, 1), f32)
        acc_ref[...] = jnp.zeros((R, NS_HD), f32)
        ncol = lax.broadcasted_iota(jnp.int32, (QG, n_sub), 1)
        okc = ncol < n_cmp
        orow = lax.broadcasted_iota(jnp.int32, (n_sub, NSP), 0)
        ocol = lax.broadcasted_iota(jnp.int32, (n_sub, NSP), 1)
        overlap = ((orow * CMP_STRIDE < (ocol + 1) * SEL_BLOCK) & (orow * CMP_STRIDE + CMP_LEN > ocol * SEL_BLOCK)
                   & (orow < n_cmp)).astype(f32)
        jcol = lax.broadcasted_iota(jnp.int32, (n_new, NSP), 1)
        qrow = lax.broadcasted_iota(jnp.int32, (n_new, NSP), 0)
        cur = (past + qrow) // SEL_BLOCK
        valid = (jcol <= cur) & (jcol < ns)
        forced = (jcol == 0) | (jcol == cur) | (jcol == cur - 1)
        for kv in range(NS_KV):
            def cmp_of(c, w2_ref):
                a = ab_ref[0, c, :, 0:CMP_HIDDEN]
                b = ab_ref[0, c, :, CMP_HIDDEN:2 * CMP_HIDDEN]
                b = jnp.concatenate([b[1:], b[:1]], axis=0)
                h = gelu_erf(a + b)
                return jnp.dot(h.astype(bf16), w2_ref[...].astype(bf16), preferred_element_type=f32)
            kc = cmp_of(kv, k2_ref)
            vc = cmp_of(NS_KV + kv, v2_ref)
            sc = _nt_dot(qs[kv * QG:(kv + 1) * QG], kc.astype(bf16)) + bcmp_ref[kv * QG:(kv + 1) * QG, :]
            mc = jnp.max(jnp.where(okc, sc, NEG_BIG), axis=-1, keepdims=True)
            ec = jnp.where(okc, jnp.exp(sc - mc), 0.0)
            pc = ec / jnp.maximum(jnp.sum(ec, axis=-1, keepdims=True), 1e-30)
            ocmp_ref[kv * QG:(kv + 1) * QG, :] = jnp.dot(pc.astype(bf16), vc.astype(bf16), preferred_element_type=f32)
            pc_sum = pc[0:n_new]
            for g in range(1, NS_GROUP):
                pc_sum = pc_sum + pc[g * n_new:(g + 1) * n_new]
            ps = jnp.dot(pc_sum, overlap, preferred_element_type=f32, precision=lax.Precision.HIGHEST)
            score = jnp.where(valid, jnp.where(forced, FORCE_SCORE, ps), -jnp.inf)
            rank = jnp.zeros((n_new, NSP), jnp.int32)
            for i in range(ns):
                ci = jnp.broadcast_to(score[:, i:i + 1], (n_new, NSP))
                rank = rank + ((ci > score) | ((ci == score) & (jcol > i))).astype(jnp.int32)
            sel = (valid & (rank < min(SEL_TOPK, ns))).astype(f32)
            sel_ref[kv * QG:(kv + 1) * QG, :] = jnp.concatenate([sel] * NS_GROUP, axis=0)

    is_new = pg == n_pages
    krow = lax.broadcasted_iota(jnp.int32, (NSP, PAGE_SIZE), 0)
    kcol = lax.broadcasted_iota(jnp.int32, (NSP, PAGE_SIZE), 1)
    expand = (krow == pg * (PAGE_SIZE // SEL_BLOCK) + kcol // SEL_BLOCK).astype(bf16)
    inblock = jnp.dot(sel_ref[...].astype(bf16), expand, preferred_element_type=f32) > 0.5
    rr = lax.broadcasted_iota(jnp.int32, (R, PAGE_SIZE), 0)
    cc = lax.broadcasted_iota(jnp.int32, (R, PAGE_SIZE), 1)
    causal_new = (cc <= rr % n_new) & (cc < n_new)
    ok = inblock & (jnp.logical_not(is_new) | causal_new)
    kpage = jnp.where(is_new, knew_ref[0], ksel_ref[0])
    vpage = jnp.where(is_new, vnew_ref[0], vsel_ref[0])
    s = jnp.concatenate([_nt_dot(qs[kv * QG:(kv + 1) * QG], kpage[:, kv * NS_HD:(kv + 1) * NS_HD].astype(bf16))
                         for kv in range(NS_KV)], axis=0) + bsel_ref[...]
    m_old = m_ref[...]
    m_new = jnp.maximum(m_old, jnp.max(jnp.where(ok, s, NEG_BIG), axis=-1, keepdims=True))
    e = jnp.where(ok, jnp.exp(s - m_new), 0.0)
    scale = jnp.exp(m_old - m_new)
    l_ref[...] = l_ref[...] * scale + jnp.sum(e, axis=-1, keepdims=True)
    pv = jnp.concatenate([jnp.dot(e[kv * QG:(kv + 1) * QG].astype(bf16), vpage[:, kv * NS_HD:(kv + 1) * NS_HD].astype(bf16),
                                  preferred_element_type=f32) for kv in range(NS_KV)], axis=0)
    acc_ref[...] = acc_ref[...] * scale + pv
    m_ref[...] = m_new

    @pl.when(is_new)
    def _():
        o_sel = acc_ref[...] / jnp.maximum(l_ref[...], 1e-30)
        Wb = win_ref.shape[1]
        wc = lax.broadcasted_iota(jnp.int32, (R, Wb + PAGE_SIZE), 1)
        wr = lax.broadcasted_iota(jnp.int32, (R, Wb + PAGE_SIZE), 0) % n_new
        dw = (Wb + wr) - wc
        okw = (dw >= 0) & (dw <= WINDOW) & ((wc < Wb) | (wc - Wb < n_new))
        sw = []
        for kv in range(NS_KV):
            kw = jnp.concatenate([win_ref[0, :, kv * NS_HD:(kv + 1) * NS_HD],
                                  wnew_ref[0, :, kv * NS_HD:(kv + 1) * NS_HD]], axis=0).astype(bf16)
            sw.append(_nt_dot(qs[kv * QG:(kv + 1) * QG], kw))
        sw = jnp.concatenate(sw, axis=0) + bwin_ref[...]
        mw = jnp.max(jnp.where(okw, sw, NEG_BIG), axis=-1, keepdims=True)
        ew = jnp.where(okw, jnp.exp(sw - mw), 0.0)
        pw = (ew / jnp.maximum(jnp.sum(ew, axis=-1, keepdims=True), 1e-30)).astype(bf16)
        o_win = []
        for kv in range(NS_KV):
            vw = jnp.concatenate([win_ref[0, :, NS_KVW + kv * NS_HD:NS_KVW + (kv + 1) * NS_HD],
                                  wnew_ref[0, :, NS_KVW + kv * NS_HD:NS_KVW + (kv + 1) * NS_HD]], axis=0).astype(bf16)
            o_win.append(jnp.dot(pw[kv * QG:(kv + 1) * QG], vw, preferred_element_type=f32))
        o_win = jnp.concatenate(o_win, axis=0)
        gates = jax.nn.sigmoid(gate_ref[0].astype(f32))
        o = gates[:, 0:1] * ocmp_ref[...] + gates[:, 1:2] * o_sel + gates[:, 2:3] * o_win
        o_ref[0] = jnp.concatenate([o[h * n_new:(h + 1) * n_new] for h in range(NS_HEADS)], axis=-1)


def nsa_sample_bias(rel_bias, n_new, past, wb):
    tab = rel_bias.astype(jnp.float32)
    n_sub = past // CMP_STRIDE
    tq = past + jnp.arange(n_new)[:, None]
    pos = jnp.arange(past + PAGE_SIZE)[None, :]
    R = NS_HEADS * n_new
    bsel = jnp.transpose(tab[rel_bucket(tq - pos)], (2, 0, 1)).reshape(R, past + PAGE_SIZE)
    cpos = (jnp.arange(n_sub) * CMP_STRIDE + CMP_LEN - 1)[None, :]
    bcmp = jnp.transpose(tab[rel_bucket(tq - cpos)], (2, 0, 1)).reshape(R, n_sub)
    return bsel, bsel[:, past - wb:], bcmp


def nsa_sample_attention(ps, cache_l, page_table, win_buf, ab, phi_k2, phi_v2, bias):
    bsel, bwin, bcmp = bias
    DB, Tn, _ = ps.shape
    n_phys = cache_l.shape[0]
    n_pages = page_table.shape[1]
    Wb = win_buf.shape[1]
    R = NS_HEADS * Tn
    ns = n_pages * (PAGE_SIZE // SEL_BLOCK) + 1
    NSP = -(-ns // 128) * 128
    n_sub = n_pages * PAGE_SIZE // CMP_STRIDE
    cache2 = cache_l.reshape(n_phys, PAGE_SIZE, 4 * NS_KVW)
    q = jnp.transpose(ps[..., :NS_W].reshape(DB, Tn, NS_HEADS, NS_HD), (0, 2, 1, 3)).reshape(DB, R, NS_HD)
    kvn = ps[..., NS_W:NS_W + 6 * NS_KVW].reshape(DB, Tn, 6, NS_KVW)
    padn = ((0, 0), (0, PAGE_SIZE - Tn), (0, 0))
    knew = jnp.pad(kvn[:, :, 2], padn)
    vnew = jnp.pad(kvn[:, :, 3], padn)
    wnew = jnp.pad(jnp.concatenate([kvn[:, :, 4], kvn[:, :, 5]], axis=-1), padn)
    win2 = win_buf.reshape(DB, Wb, 2 * NS_KVW)
    glog = jnp.transpose(ps[..., NS_W + 6 * NS_KVW:].reshape(DB, Tn, 3, NS_HEADS), (0, 3, 1, 2)).reshape(DB, R, 3)
    last = n_pages - 1
    grid_spec = pltpu.PrefetchScalarGridSpec(
        num_scalar_prefetch=1,
        grid=(DB, n_pages + 1),
        in_specs=[
            pl.BlockSpec((1, R, NS_HD), lambda b, g, pt: (b, 0, 0)),
            pl.BlockSpec((1, PAGE_SIZE, NS_KVW), lambda b, g, pt: (pt[b, jnp.minimum(g, last)], 0, 2)),
            pl.BlockSpec((1, PAGE_SIZE, NS_KVW), lambda b, g, pt: (pt[b, jnp.minimum(g, last)], 0, 3)),
            pl.BlockSpec((1, PAGE_SIZE, NS_KVW), lambda b, g, pt: (b, 0, 0)),
            pl.BlockSpec((1, PAGE_SIZE, NS_KVW), lambda b, g, pt: (b, 0, 0)),
            pl.BlockSpec((1, Wb, 2 * NS_KVW), lambda b, g, pt: (b, 0, 0)),
            pl.BlockSpec((1, PAGE_SIZE, 2 * NS_KVW), lambda b, g, pt: (b, 0, 0)),
            pl.BlockSpec((1, R, 3), lambda b, g, pt: (b, 0, 0)),
            pl.BlockSpec((1, 2 * NS_KV, n_sub, 2 * CMP_HIDDEN), lambda b, g, pt: (b, 0, 0, 0)),
            pl.BlockSpec((CMP_HIDDEN, NS_HD), lambda b, g, pt: (0, 0)),
            pl.BlockSpec((CMP_HIDDEN, NS_HD), lambda b, g, pt: (0, 0)),
            pl.BlockSpec((R, PAGE_SIZE), lambda b, g, pt: (0, g)),
            pl.BlockSpec((R, Wb + PAGE_SIZE), lambda b, g, pt: (0, 0)),
            pl.BlockSpec((R, n_sub), lambda b, g, pt: (0, 0)),
        ],
        out_specs=pl.BlockSpec((1, Tn, NS_W), lambda b, g, pt: (b, 0, 0)),
        scratch_shapes=[pltpu.VMEM((R, 1), jnp.float32), pltpu.VMEM((R, 1), jnp.float32),
                        pltpu.VMEM((R, NS_HD), jnp.float32), pltpu.VMEM((R, NSP), jnp.float32),
                        pltpu.VMEM((R, NS_HD), jnp.float32)])
    return pl.pallas_call(
        functools.partial(_nsa_sample_body, n_pages=n_pages, n_new=Tn),
        grid_spec=grid_spec,
        out_shape=jax.ShapeDtypeStruct((DB, Tn, NS_W), jnp.float32),
        compiler_params=pltpu.CompilerParams(dimension_semantics=("parallel", "arbitrary"),
                                             vmem_limit_bytes=V7X_VMEM_LIMIT_BYTES),
        name="nsa_sample_attention",
    )(page_table, q, cache2, cache2, knew, vnew, win2, wnew, glog, ab, phi_k2, phi_v2, bsel, bwin, bcmp)


def nsa_sample(p, cache_kv_l, page_table, win_buf, phi_k1, phi_k2, phi_v1, phi_v2, bias):
    DB, Tn, _ = p.shape
    assert Tn < CMP_STRIDE and Tn <= SEL_BLOCK
    kv = p[..., NS_W:NS_W + 6 * NS_KVW].reshape(DB, Tn, 6, NS_KV, NS_HD)
    ab = nsa_compress_pages(cache_kv_l, page_table, phi_k1, phi_v1)
    o = nsa_sample_attention(p, cache_kv_l, page_table, win_buf, ab, phi_k2, phi_v2, bias)
    win = jnp.concatenate([win_buf, kv[:, :, 4:].astype(win_buf.dtype)], axis=1)
    return o, kv[:, :, :4], win[:, Tn:]


PEER_ROUTE_TM = 128


def gelu_erf(x):
    return 0.5 * x * (1.0 + lax.erf(x * (2.0 ** -0.5)))


def _top_rows(work, n_rows, k, row_iota):
    vals, idxs = [], []
    for _ in range(k):
        m = jnp.max(work, axis=0, keepdims=True)
        idx = jnp.min(jnp.where(work == m, row_iota, n_rows), axis=0, keepdims=True)
        vals.append(m)
        idxs.append(idx)
        work = jnp.where(row_iota == idx, -jnp.inf, work)
    return vals, idxs


def _peer_route_body(q_ref, k1_ref, k2_ref, g_ref, i1_s, i2_s, w_s):
    f32, bf16 = jnp.float32, jnp.bfloat16
    tm = q_ref.shape[0]
    half = PEER_DKEY // 2
    K = PEER_TOPK
    rows = lax.broadcasted_iota(jnp.int32, (PEER_NKEYS, tm), 0)
    crow = lax.broadcasted_iota(jnp.int32, (K * K, tm), 0)
    k1 = k1_ref[...].astype(bf16)
    k2 = k2_ref[...].astype(bf16)

    def nt_dot(a, b):
        return lax.dot_general(a, b, (((1,), (1,)), ((), ())), preferred_element_type=f32)

    for h in range(PEER_HEADS):
        q1 = q_ref[:, h * PEER_DKEY:h * PEER_DKEY + half].astype(bf16)
        q2 = q_ref[:, h * PEER_DKEY + half:(h + 1) * PEER_DKEY].astype(bf16)
        v1, i1 = _top_rows(nt_dot(k1, q1), PEER_NKEYS, K, rows)
        v2, i2 = _top_rows(nt_dot(k2, q2), PEER_NKEYS, K, rows)
        v2m = jnp.concatenate(v2, axis=0)
        i2m = jnp.concatenate(i2, axis=0)
        cand = jnp.concatenate([v1[a] + v2m for a in range(K)], axis=0)
        cidx = jnp.concatenate([i1[a] * PEER_NKEYS + i2m for a in range(K)], axis=0)
        sv, pos = _top_rows(cand, K * K, K, crow)
        eidx = [jnp.max(jnp.where(crow == pos[k], cidx, 0), axis=0, keepdims=True) for k in range(K)]
        svm = jnp.concatenate(sv, axis=0)
        em = jnp.concatenate(eidx, axis=0)
        e = jnp.exp(svm - svm[0:1])
        gw = e / jnp.sum(e, axis=0, keepdims=True)
        i1_s[h * K:(h + 1) * K, :] = (em // PEER_NKEYS).astype(f32)
        i2_s[h * K:(h + 1) * K, :] = (em % PEER_NKEYS).astype(f32)
        w_s[h * K:(h + 1) * K, :] = gw
    i1_s[...] = i1_s[...].T
    i2_s[...] = i2_s[...].T
    w_s[...] = w_s[...].T
    sub = lax.broadcasted_iota(jnp.int32, (PEER_NKEYS, PEER_HEADS * K), 0).astype(f32)

    def per_token(t, carry):
        a_w = jnp.where(sub == i1_s[pl.ds(t, 1), :], w_s[pl.ds(t, 1), :], 0.0).astype(bf16)
        b_1 = jnp.where(sub == i2_s[pl.ds(t, 1), :], 1.0, 0.0).astype(bf16)
        g_ref[t] = nt_dot(a_w, b_1).astype(g_ref.dtype)
        return carry

    lax.fori_loop(0, tm, per_token, 0)


def peer_route(q, k1, k2):
    n = q.shape[0]
    tm = PEER_ROUTE_TM
    S = PEER_HEADS * PEER_TOPK
    assert n % tm == 0 and S == tm
    g = pl.pallas_call(
        _peer_route_body,
        grid=(n // tm,),
        in_specs=[pl.BlockSpec((tm, PEER_HEADS * PEER_DKEY), lambda i: (i, 0)),
                  pl.BlockSpec((PEER_NKEYS, PEER_DKEY // 2), lambda i: (0, 0)),
                  pl.BlockSpec((PEER_NKEYS, PEER_DKEY // 2), lambda i: (0, 0))],
        out_specs=pl.BlockSpec((tm, PEER_NKEYS, PEER_NKEYS), lambda i: (i, 0, 0)),
        out_shape=jax.ShapeDtypeStruct((n, PEER_NKEYS, PEER_NKEYS), jnp.bfloat16),
        scratch_shapes=[pltpu.VMEM((S, tm), jnp.float32)] * 3,
        compiler_params=pltpu.CompilerParams(dimension_semantics=("parallel",),
                                             vmem_limit_bytes=V7X_VMEM_LIMIT_BYTES),
        name="peer_route",
    )(q, k1, k2)
    return g.reshape(n, PEER_EXPERTS)


def _peer_expert_body(x_ref, g_ref, u_ref, v_ref, o_ref):
    f32, bf16 = jnp.float32, jnp.bfloat16
    e = pl.program_id(1)
    h = lax.dot_general(x_ref[...], u_ref[...], (((1,), (1,)), ((), ())), preferred_element_type=f32)
    p = (g_ref[...].astype(f32) * gelu_erf(h)).astype(bf16)
    upd = jnp.dot(p, v_ref[...], preferred_element_type=f32)

    @pl.when(e == 0)
    def _():
        o_ref[...] = upd

    @pl.when(e > 0)
    def _():
        o_ref[...] += upd


def peer_experts(x, g, u, v, tm=512, te=512):
    n, D = x.shape
    E = u.shape[0]
    tm = min(tm, n)
    assert n % tm == 0 and E % te == 0
    return pl.pallas_call(
        _peer_expert_body,
        grid=(n // tm, E // te),
        in_specs=[pl.BlockSpec((tm, D), lambda i, e: (i, 0)),
                  pl.BlockSpec((tm, te), lambda i, e: (i, e)),
                  pl.BlockSpec((te, D), lambda i, e: (e, 0)),
                  pl.BlockSpec((te, D), lambda i, e: (e, 0))],
        out_specs=pl.BlockSpec((tm, D), lambda i, e: (i, 0)),
        out_shape=jax.ShapeDtypeStruct((n, D), jnp.float32),
        compiler_params=pltpu.CompilerParams(dimension_semantics=("parallel", "arbitrary"),
                                             vmem_limit_bytes=V7X_VMEM_LIMIT_BYTES),
        name="peer_experts",
    )(x, g, u, v)


def peer_ffn(x, wq, k1, k2, u_bf, v_bf):
    Bx, T, D = x.shape
    n = Bx * T
    pad = -n % PEER_ROUTE_TM
    xt = jnp.pad(x.reshape(n, D), ((0, pad), (0, 0)))
    g = peer_route(matmul(xt, wq), k1, k2)
    out = peer_experts(xt.astype(jnp.bfloat16), g, u_bf, v_bf)
    return out[:n].reshape(Bx, T, D).astype(x.dtype)


def residual_block(x, mix, w_out, ln1_g, ln1_b, ln2_g, ln2_b, peer_wq, peer_k1, peer_k2, peer_u, peer_v):
    x = layer_norm(ALPHA * x + matmul3(mix, w_out), ln1_g, ln1_b)
    return layer_norm(ALPHA * x + peer_ffn(x, peer_wq, peer_k1, peer_k2, peer_u, peer_v), ln2_g, ln2_b)


def kernel(x_prompt, x_sample, cache_kv, cache_win, state_rwkv, state_rwkv_shift, state_gdn, state_gdn_conv, page_table, w_in, w_out, ln1_g, ln1_b, ln2_g, ln2_b, rw_mu, rw_w0, rw_w_up, rw_a0, rw_a_up, rw_g_up, rw_k_k, rw_k_a, rw_r_k, rw_ln_g, rw_ln_b, gd_conv_w, gd_a_log, gd_dt_bias, gd_norm_g, ns_phi_k1, ns_phi_k2, ns_phi_v1, ns_phi_v2, rel_bias, peer_wq, peer_k1, peer_k2, peer_u, peer_v):
    xp, xs = x_prompt, x_sample
    B = xp.shape[0]
    o_b = RW_COLS
    o_c = RW_COLS + GD_COLS
    bsel, bcmp = nsa_bias_tiles(rel_bias, SEQ // TQ)
    sbias = nsa_sample_bias(rel_bias, x_sample.shape[1], page_table.shape[1] * PAGE_SIZE, cache_win.shape[2])
    kv_p, kv_s, win_p, win_s, rw_p, rw_s, sh_p, sh_s, gd_p, gd_s, cv_p, cv_s = ([] for _ in range(12))
    for l in range(DEPTH):
        rw = (rw_mu[l], rw_w0[l], rw_w_up[l], rw_a0[l], rw_a_up[l], rw_g_up[l], rw_k_k[l], rw_k_a[l], rw_r_k[l], rw_ln_g[l], rw_ln_b[l])
        gd = (gd_conv_w[l], gd_a_log[l], gd_dt_bias[l], gd_norm_g[l])
        phi = (ns_phi_k1[l], ns_phi_k2[l], ns_phi_v1[l], ns_phi_v2[l])
        tail = (w_out[l], ln1_g[l], ln1_b[l], ln2_g[l], ln2_b[l], peer_wq[l], peer_k1[l], peer_k2[l],
                peer_u[l].astype(jnp.bfloat16), peer_v[l].astype(jnp.bfloat16))
        pp = matmul3(xp, w_in[l])
        a, sh, rs = rwkv7_mixer(pp[..., :o_b], jnp.zeros((B, RW_COLS), pp.dtype), jnp.zeros((B, RW_HEADS, RW_HD, RW_HD), jnp.float32), *rw)
        b, cv, gs = gated_deltanet_mixer(pp[..., o_b:o_c], jnp.zeros((B, GD_CONV - 1, GD_QKV), pp.dtype), jnp.zeros((B, GD_HEADS, GD_HD, GD_HD), jnp.float32), *gd)
        c, kvr, wr = nsa_prompt(pp[..., o_c:], *phi, bsel, bcmp)
        xp = residual_block(xp, jnp.concatenate([a, b, c.astype(a.dtype)], -1), *tail)
        kv_p.append(kvr)
        win_p.append(wr)
        rw_p.append(rs)
        sh_p.append(sh)
        gd_p.append(gs)
        cv_p.append(cv)
        ps = matmul3(xs, w_in[l])
        a, sh, rs = rwkv7_mixer(ps[..., :o_b], state_rwkv_shift[l], state_rwkv[l], *rw)
        b, cv, gs = gated_deltanet_mixer(ps[..., o_b:o_c], state_gdn_conv[l], state_gdn[l], *gd)
        c, kvr, wr = nsa_sample(ps[..., o_c:], cache_kv[l], page_table, cache_win[l], *phi, sbias)
        xs = residual_block(xs, jnp.concatenate([a, b, c.astype(a.dtype)], -1), *tail)
        kv_s.append(kvr)
        win_s.append(wr)
        rw_s.append(rs)
        sh_s.append(sh)
        gd_s.append(gs)
        cv_s.append(cv)
    st = jnp.stack
    return (xp, xs, st(kv_p), st(kv_s), st(win_p), st(win_s), st(rw_p), st(rw_s), st(sh_p), st(sh_s), st(gd_p), st(gd_s), st(cv_p), st(cv_s))
```

```python
import functools
import math

import jax
import jax.numpy as jnp
from jax import lax
from jax.experimental import pallas as pl
from jax.experimental.pallas import tpu as pltpu

D_MODEL = 4096
BATCH = 4
SEQ = 2048
DEPTH = 4
DEC_BATCH = 8
DEC_SEQ = 8
PAST_LEN = 8192
PAGE_SIZE = 128

ALPHA = (2 * DEPTH) ** 0.25
LN_EPS = 1e-5

RW_HD = 64
RW_W = D_MODEL // 4
RW_HEADS = RW_W // RW_HD
RW_DECAY_R = 64
RW_AAA_R = 64
RW_GATE_R = 160
RW_COLS = 3 * RW_W + RW_DECAY_R + RW_AAA_R + RW_GATE_R
RW_GN_EPS = 64e-5

GD_HD = 128
GD_W = D_MODEL // 4
GD_HEADS = GD_W // GD_HD
GD_QKV = 3 * GD_W
GD_CONV = 4
GD_CHUNK = 64
GD_COLS = GD_QKV + GD_W + 2 * GD_HEADS

NS_HD = 128
NS_W = D_MODEL // 2
NS_HEADS = NS_W // NS_HD
NS_KV = 4
NS_GROUP = NS_HEADS // NS_KV
NS_KVW = NS_KV * NS_HD
NS_COLS = NS_W + 6 * NS_KVW + 3 * NS_HEADS
CMP_LEN = 32
CMP_STRIDE = 16
CMP_HIDDEN = 128
SEL_BLOCK = 64
SEL_TOPK = 16
WINDOW = 512
NS_QBLOCK = 32
FORCE_SCORE = 1e4

REL_BUCKETS = 32
REL_MAX_DIST = 1024

D_MIX = RW_W + GD_W + NS_W
IN_COLS = RW_COLS + GD_COLS + NS_COLS

PEER_HEADS = 8
PEER_NKEYS = 128
PEER_EXPERTS = PEER_NKEYS ** 2
PEER_DKEY = 256
PEER_TOPK = 16
PEER_TBLOCK = 128

V7X_VMEM_LIMIT_BYTES = 56 * 1024 * 1024


def _matmul_body(x_ref, w_ref, o_ref):
    o_ref[...] = jnp.dot(x_ref[...].astype(jnp.bfloat16), w_ref[...].astype(jnp.bfloat16),
                         preferred_element_type=jnp.float32)


def _pick_tile(n, target):
    t = min(n, target)
    while n % t:
        t //= 2
    return t


def matmul(x, w, tm=512, tn=1024):
    M, K = x.shape
    N = w.shape[1]
    tm = _pick_tile(M, tm)
    tn = min(tn, N)
    return pl.pallas_call(
        _matmul_body,
        grid=(M // tm, pl.cdiv(N, tn)),
        in_specs=[pl.BlockSpec((tm, K), lambda i, j: (i, 0)),
                  pl.BlockSpec((K, tn), lambda i, j: (0, j))],
        out_specs=pl.BlockSpec((tm, tn), lambda i, j: (i, j)),
        out_shape=jax.ShapeDtypeStruct((M, N), jnp.float32),
        compiler_params=pltpu.CompilerParams(
            dimension_semantics=("parallel", "parallel"),
            vmem_limit_bytes=V7X_VMEM_LIMIT_BYTES),
        name="proj_matmul",
    )(x, w)


def matmul3(x, w):
    B, T, K = x.shape
    return matmul(x.reshape(B * T, K), w).reshape(B, T, -1)


def layer_norm(x, g, b):
    xf = x.astype(jnp.float32)
    mu = jnp.mean(xf, -1, keepdims=True)
    var = jnp.mean(jnp.square(xf - mu), -1, keepdims=True)
    return ((xf - mu) * lax.rsqrt(var + LN_EPS) * g + b).astype(x.dtype)


def l2_normalize(x):
    return x / jnp.maximum(jnp.sqrt(jnp.sum(x * x, -1, keepdims=True)), 1e-12)


def rel_bucket(d):
    d = jnp.maximum(d, 0)
    exact = REL_BUCKETS // 2
    logd = jnp.log(jnp.maximum(d, 1).astype(jnp.float32) / exact) / math.log(REL_MAX_DIST / exact)
    large = jnp.minimum(exact + (logd * (REL_BUCKETS - exact)).astype(jnp.int32), REL_BUCKETS - 1)
    return jnp.where(d < exact, d, large)


RW_CHUNK = 64


def _split(x):
    hi = x.astype(jnp.bfloat16)
    lo = (x - hi.astype(jnp.float32)).astype(jnp.bfloat16)
    return hi, lo


def _dot3(a, b, dims=(((1,), (0,)), ((), ()))):
    ah, al = _split(a)
    bh, bl = _split(b)
    d = lambda x, y: lax.dot_general(x, y, dims, preferred_element_type=jnp.float32)
    return d(ah, bh) + (d(ah, bl) + d(al, bh))


_NT = (((1,), (1,)), ((), ()))
_TN = (((0,), (0,)), ((), ()))


def _rwkv_chunk_body(r_ref, lw_ref, k_ref, v_ref, kk_ref, ka_ref, s0_ref, y_ref, s_ref, *, n_chunks, n_valid, n_heads):
    f32 = jnp.float32
    N, C = RW_HD, RW_CHUNK
    c_idx = pl.program_id(2)

    @pl.when(c_idx == 0)
    def _():
        s_ref[...] = s0_ref[...]

    row = lax.broadcasted_iota(jnp.int32, (C, C), 0)
    col = lax.broadcasted_iota(jnp.int32, (C, C), 1)
    tril = (row >= col).astype(f32)
    strict = row > col

    hs = range(n_heads)
    for ci in range(n_chunks):
        rows = slice(ci * C, (ci + 1) * C)
        padded = n_valid < n_chunks * C
        live = (lax.broadcasted_iota(jnp.int32, (C, N), 0) + ci * C) < n_valid

        def ld(ref, h):
            x = ref[0, rows, h * N:(h + 1) * N]
            return jnp.where(live, x, 0.0) if padded else x

        lw = [ld(lw_ref, h) for h in hs]
        G = [_dot3(tril, lw[h]) for h in hs]
        eg = [jnp.exp(G[h]) for h in hs]
        ing = [jnp.exp(-G[h]) for h in hs]
        ar = [jnp.concatenate([-ld(kk_ref, h) * jnp.exp(G[h] - lw[h]), ld(r_ref, h) * eg[h]], axis=0) for h in hs]
        bk = [jnp.concatenate([ld(ka_ref, h) * ing[h], ld(k_ref, h) * ing[h]], axis=0) for h in hs]
        v = [ld(v_ref, h) for h in hs]
        S0 = [s_ref[0, h] for h in hs]
        M = [_dot3(ar[h], bk[h], _NT) for h in hs]
        AS = [_dot3(ar[h], S0[h], _NT) for h in hs]
        P = [jnp.where(strict, M[h][0:C, 0:C], 0.0) for h in hs]
        M2 = [jnp.where(strict, M[h][0:C, C:2 * C], 0.0) for h in hs]
        M34 = [jnp.concatenate([M[h][C:2 * C, 0:C] * tril, M[h][C:2 * C, C:2 * C] * tril], axis=1) for h in hs]
        rhs = [AS[h][0:C] + _dot3(M2[h], v[h]) for h in hs]
        X = [rhs[h] + _dot3(P[h], rhs[h]) for h in hs]
        for _ in range(5):
            P = [_dot3(P[h], P[h]) for h in hs]
            X = [X[h] + _dot3(P[h], X[h]) for h in hs]
        sav = [jnp.concatenate([X[h], v[h]], axis=0) for h in hs]
        for h in hs:
            y_ref[0, rows, h * N:(h + 1) * N] = AS[h][C:2 * C] + _dot3(M34[h], sav[h])
        dS = [_dot3(sav[h], bk[h], _TN) for h in hs]
        for h in hs:
            s_ref[0, h] = (S0[h] + dS[h]) * eg[h][C - 1:C, :]


def rwkv_scan_chunked(r, lw, k, v, kk, ka, s0, chunks_per_step=2, heads_per_step=8):
    B, T, W = r.shape
    N, C = RW_HD, RW_CHUNK
    H = W // N
    HB = min(heads_per_step, H)
    TB = C * chunks_per_step if T >= C * chunks_per_step else -(-T // C) * C
    Tp = -(-T // TB) * TB
    n_valid = T if Tp != T else TB

    def prep(x):
        return jnp.pad(x, ((0, 0), (0, Tp - T), (0, 0))) if Tp != T else x

    ins = [prep(x) for x in (r, lw, k, v, kk, ka)]
    seq = pl.BlockSpec((1, TB, HB * N), lambda b, p, c: (b, c, p))
    st = pl.BlockSpec((1, HB, N, N), lambda b, p, c: (b, p, 0, 0))
    y, sT = pl.pallas_call(
        functools.partial(_rwkv_chunk_body, n_chunks=TB // C, n_valid=n_valid, n_heads=HB),
        grid=(B, H // HB, Tp // TB),
        in_specs=[seq] * 6 + [st],
        out_specs=[seq, st],
        out_shape=[jax.ShapeDtypeStruct((B, Tp, W), jnp.float32), jax.ShapeDtypeStruct((B, H, N, N), jnp.float32)],
        compiler_params=pltpu.CompilerParams(dimension_semantics=("parallel", "parallel", "arbitrary"),
                                             vmem_limit_bytes=V7X_VMEM_LIMIT_BYTES),
        name="rwkv_chunked",
    )(*ins, s0)
    return y[:, :T], sT


def rwkv7_mixer(p, shift0, s0, mu, w0, w_up, a0, a_up, g_up, k_k, k_a, r_k, ln_g, ln_b):
    f32 = jnp.float32
    B, T, _ = p.shape
    prev = jnp.concatenate([shift0[:, None].astype(p.dtype), p[:, :-1]], axis=1)
    m = p + mu * (prev - p)
    r = m[..., :RW_W]
    k = m[..., RW_W:2 * RW_W]
    v = m[..., 2 * RW_W:3 * RW_W]
    o = 3 * RW_W
    wl = m[..., o:o + RW_DECAY_R]
    o += RW_DECAY_R
    al = m[..., o:o + RW_AAA_R]
    o += RW_AAA_R
    gl = m[..., o:o + RW_GATE_R]
    w = -jax.nn.softplus(-(w0 + jnp.tanh(wl) @ w_up).astype(f32)) - 0.5
    log_decay = -jnp.exp(w)
    a = jax.nn.sigmoid((a0 + al @ a_up).astype(f32))
    g = jax.nn.sigmoid(gl) @ g_up

    def heads(t):
        return t.reshape(B, T, RW_HEADS, RW_HD).astype(f32)

    kk = l2_normalize(heads(k * k_k)).reshape(B, T, RW_W)
    k = k * (1.0 + (a - 1.0) * k_a)
    r_, k_, v_ = heads(r), heads(k), heads(v)

    y, sT = rwkv_scan_chunked(r, log_decay, k, v, kk, kk * a, s0.astype(f32))
    y = heads(y)
    ym = jnp.mean(y, -1, keepdims=True)
    yv = jnp.mean(jnp.square(y - ym), -1, keepdims=True)
    y = ((y - ym) * lax.rsqrt(yv + RW_GN_EPS)).reshape(B, T, RW_W) * ln_g + ln_b
    bonus = jnp.sum(r_ * k_ * r_k, -1, keepdims=True) * v_
    y = (y + bonus.reshape(B, T, RW_W)) * g
    return y.astype(p.dtype), p[:, -1], sT


def _dot1(a, b, dims=(((1,), (0,)), ((), ()))):
    return lax.dot_general(a.astype(jnp.bfloat16), b.astype(jnp.bfloat16), dims, preferred_element_type=jnp.float32)


def _gdn_body(q_ref, k_ref, v_ref, beta_ref, g_ref, s0_ref, o_ref, s_ref, *, n_chunks, n_valid, n_heads):
    f32 = jnp.float32
    D, C = GD_HD, GD_CHUNK
    hg = pl.program_id(1)
    c_idx = pl.program_id(2)

    @pl.when(c_idx == 0)
    def _():
        s_ref[...] = s0_ref[...]

    row = lax.broadcasted_iota(jnp.int32, (C, C), 0)
    col = lax.broadcasted_iota(jnp.int32, (C, C), 1)
    tri = row >= col
    trif = tri.astype(f32)
    strict = row > col
    eye = row == col
    ones = jnp.ones((C, C), f32)
    lane_h = lax.broadcasted_iota(jnp.int32, (C, beta_ref.shape[2]), 1)
    hs = range(n_heads)
    for ci in range(n_chunks):
        rows = slice(ci * C, (ci + 1) * C)
        padded = n_valid < n_chunks * C
        live = (lax.broadcasted_iota(jnp.int32, (C, 1), 0) + ci * C) < n_valid

        def ld(ref, h):
            x = ref[0, rows, h * D:(h + 1) * D]
            return jnp.where(live, x, 0.0) if padded else x

        g_all = g_ref[0, rows, :]
        b_all = beta_ref[0, rows, :]
        if padded:
            g_all = jnp.where(live, g_all, 0.0)
            b_all = jnp.where(live, b_all, 0.0)
        G_all = _dot3(trif, g_all)

        def colof(x, h):
            return jnp.sum(jnp.where(lane_h == hg * n_heads + h, x, 0.0), axis=1, keepdims=True)

        Gc = [colof(G_all, h) for h in hs]
        bc = [colof(b_all, h) for h in hs]
        GB = [jnp.broadcast_to(Gc[h], (C, C)) for h in hs]
        GR = [_dot3(ones, jnp.where(eye, GB[h], 0.0)) for h in hs]
        decay = [jnp.where(tri, jnp.exp(jnp.where(tri, GB[h] - GR[h], 0.0)), 0.0) for h in hs]
        eG = [jnp.exp(Gc[h]) for h in hs]
        Glast = [Gc[h][C - 1:C, :] for h in hs]
        q = [ld(q_ref, h) for h in hs]
        k = [ld(k_ref, h) for h in hs]
        v = [ld(v_ref, h) for h in hs]
        kb = [k[h] * bc[h] for h in hs]
        P = [-jnp.where(strict, _dot1(kb[h], k[h], _NT) * decay[h], 0.0) for h in hs]
        Aqk = [_dot1(q[h], k[h], _NT) * decay[h] for h in hs]
        rhs = [jnp.concatenate([v[h] * bc[h], kb[h] * eG[h]], axis=1) for h in hs]
        X = [rhs[h] + _dot3(P[h], rhs[h]) for h in hs]
        for _ in range(5):
            P = [_dot3(P[h], P[h]) for h in hs]
            X = [X[h] + _dot3(P[h], X[h]) for h in hs]
        S = [s_ref[0, h] for h in hs]
        vn = [X[h][:, 0:D] - _dot1(X[h][:, D:2 * D], S[h]) for h in hs]
        for h in hs:
            o_ref[0, rows, h * D:(h + 1) * D] = _dot1(q[h] * eG[h], S[h]) + _dot1(Aqk[h], vn[h])
        for h in hs:
            s_ref[0, h] = S[h] * jnp.exp(Glast[h]) + _dot1(k[h] * jnp.exp(Glast[h] - Gc[h]), vn[h], _TN)


def gdn_chunked(qkv, beta, g, s0, chunks_per_step=2, heads_per_step=4):
    B, T, W3 = qkv.shape
    D, C = GD_HD, GD_CHUNK
    H = W3 // (3 * D)
    HB = min(heads_per_step, H)
    TB = C * chunks_per_step if T >= C * chunks_per_step else -(-T // C) * C
    Tp = -(-T // TB) * TB
    n_valid = T if Tp != T else TB
    if Tp != T:
        padt = ((0, 0), (0, Tp - T), (0, 0))
        qkv, beta, g = jnp.pad(qkv, padt), jnp.pad(beta, padt), jnp.pad(g, padt)
    ng = H // HB

    def seq(part):
        return pl.BlockSpec((1, TB, HB * D), lambda b, p, c, part=part: (b, c, part * ng + p))

    sc = pl.BlockSpec((1, TB, H), lambda b, p, c: (b, c, 0))
    st = pl.BlockSpec((1, HB, D, D), lambda b, p, c: (b, p, 0, 0))
    o, sT = pl.pallas_call(
        functools.partial(_gdn_body, n_chunks=TB // C, n_valid=n_valid, n_heads=HB),
        grid=(B, ng, Tp // TB),
        in_specs=[seq(0), seq(1), seq(2), sc, sc, st],
        out_specs=[pl.BlockSpec((1, TB, HB * D), lambda b, p, c: (b, c, p)), st],
        out_shape=[jax.ShapeDtypeStruct((B, Tp, H * D), jnp.float32), jax.ShapeDtypeStruct((B, H, D, D), jnp.float32)],
        compiler_params=pltpu.CompilerParams(dimension_semantics=("parallel", "parallel", "arbitrary"),
                                             vmem_limit_bytes=V7X_VMEM_LIMIT_BYTES),
        name="gdn_chunked",
    )(qkv, qkv, qkv, beta, g, s0)
    return o[:, :T], sT


def gated_deltanet_mixer(p, conv0, s0, conv_w, a_log, dt_bias, norm_g):
    f32 = jnp.float32
    B, T, _ = p.shape
    qkv = p[..., :GD_QKV]
    z = p[..., GD_QKV:GD_QKV + GD_W]
    bl = p[..., GD_QKV + GD_W:GD_QKV + GD_W + GD_HEADS]
    al = p[..., GD_QKV + GD_W + GD_HEADS:]
    xc = jnp.concatenate([conv0.astype(p.dtype), qkv], axis=1)
    conv = xc[:, :T] * conv_w[0]
    for i in range(1, GD_CONV):
        conv = conv + xc[:, i:i + T] * conv_w[i]
    conv = jax.nn.silu(conv).astype(f32)
    q = l2_normalize(conv[..., :GD_W].reshape(B, T, GD_HEADS, GD_HD)) * GD_HD ** -0.5
    k = l2_normalize(conv[..., GD_W:2 * GD_W].reshape(B, T, GD_HEADS, GD_HD))
    v = conv[..., 2 * GD_W:].reshape(B, T, GD_HEADS, GD_HD)
    beta = jax.nn.sigmoid(bl.astype(f32))
    g = -jnp.exp(a_log.astype(f32)) * jax.nn.softplus((al + dt_bias).astype(f32))
    qkv_n = jnp.concatenate([q.reshape(B, T, GD_W), k.reshape(B, T, GD_W), conv[..., 2 * GD_W:]], axis=-1)
    o, sT = gdn_chunked(qkv_n, beta, g, s0.astype(f32))
    o = o.reshape(B, T, GD_HEADS, GD_HD)
    o = o * lax.rsqrt(jnp.mean(o * o, -1, keepdims=True) + 1e-6) * norm_g
    o = o.reshape(B, T, GD_W) * jax.nn.silu(z.astype(f32))
    return o.astype(p.dtype), xc[:, -(GD_CONV - 1):], sT


TQ = 128
NEG_BIG = -1e30


def _stack_groups(x):
    return jnp.concatenate([x[:, g * NS_HD:(g + 1) * NS_HD] for g in range(NS_GROUP)], axis=0)


def _nsa_prompt_body(q_ref, ksel_ref, vsel_ref, kwin_ref, vwin_ref, gate_ref, kcmp_ref, vcmp_ref,
                     bsel_ref, bcmp_ref, o_ref, m_ref, l_ref, acc_ref, mask_ref, *, n_tiles):
    f32, bf16 = jnp.float32, jnp.bfloat16
    qi = pl.program_id(2)
    R = NS_GROUP * TQ
    qg = (_stack_groups(q_ref[0]) * NS_HD ** -0.5).astype(bf16)
    row = lax.broadcasted_iota(jnp.int32, (TQ, TQ), 0)
    col = lax.broadcasted_iota(jnp.int32, (TQ, TQ), 1)
    t_q = qi * TQ + row

    def nt_dot(a, b):
        return lax.dot_general(a, b, (((1,), (1,)), ((), ())), preferred_element_type=f32)

    def tile4(x):
        return jnp.concatenate([x] * NS_GROUP, axis=0)

    sc = nt_dot(qg, kcmp_ref[0, 0].astype(bf16))
    sc = sc + jnp.concatenate([bcmp_ref[g, 0] for g in range(NS_GROUP)], axis=0)
    okc = tile4((t_q - (col * CMP_STRIDE + CMP_LEN - 1)) >= 0)
    mc = jnp.max(jnp.where(okc, sc, NEG_BIG), axis=-1, keepdims=True)
    ec = jnp.where(okc, jnp.exp(sc - mc), 0.0)
    pc = ec / jnp.maximum(jnp.sum(ec, axis=-1, keepdims=True), 1e-30)
    o_cmp = jnp.dot(pc.astype(bf16), vcmp_ref[0, 0].astype(bf16), preferred_element_type=f32)

    pc_sum = pc[0:TQ] + pc[TQ:2 * TQ] + pc[2 * TQ:3 * TQ] + pc[3 * TQ:4 * TQ]
    ns = n_tiles * (TQ // SEL_BLOCK)
    nbp = -(-ns // 8) * 8
    jrow = lax.broadcasted_iota(jnp.int32, (nbp, TQ), 0)
    lcol = lax.broadcasted_iota(jnp.int32, (nbp, TQ), 1)
    overlap_t = ((lcol * CMP_STRIDE < (jrow + 1) * SEL_BLOCK) & (lcol * CMP_STRIDE + CMP_LEN > jrow * SEL_BLOCK)).astype(f32)
    ps = lax.dot_general(overlap_t, pc_sum, (((1,), (1,)), ((), ())), preferred_element_type=f32,
                         precision=lax.Precision.HIGHEST)
    cur = (qi * TQ + lcol) // SEL_BLOCK
    valid = (jrow <= cur) & (jrow < ns)
    forced = (jrow == 0) | (jrow == cur) | (jrow == cur - 1)
    score = jnp.where(valid, jnp.where(forced, FORCE_SCORE, ps), -jnp.inf)
    rank = jnp.zeros((nbp, TQ), jnp.int32)
    for i in range(ns):
        ri = score[i:i + 1, :]
        beats = (ri > score) | ((ri == score) & (jrow > i))
        rank = rank + beats.astype(jnp.int32)
    sel_t = (valid & (rank < min(SEL_TOPK, ns))).astype(bf16)
    for kj in range(n_tiles):
        expand = (jrow == (kj * (TQ // SEL_BLOCK) + lcol // SEL_BLOCK)).astype(bf16)
        mask_ref[kj] = lax.dot_general(sel_t, expand, (((0,), (0,)), ((), ())), preferred_element_type=f32)

    def attend(k_ref, v_ref, lo, kind):
        m_ref[...] = jnp.full((R, NS_HD), NEG_BIG, f32)
        l_ref[...] = jnp.zeros((R, NS_HD), f32)
        acc_ref[...] = jnp.zeros((R, NS_HD), f32)

        def step(pj, carry):
            kj = 2 * pj
            off = pl.multiple_of(kj * TQ, 2 * TQ)
            kt = k_ref[0, pl.ds(off, 2 * TQ), :].astype(bf16)
            vt = v_ref[0, pl.ds(off, 2 * TQ), :].astype(bf16)
            delta = qi - kj
            delta1 = jnp.maximum(delta - 1, 0)
            s = nt_dot(qg, kt)
            s = s + jnp.concatenate(
                [jnp.concatenate([bsel_ref[g, delta], bsel_ref[g, delta1]], axis=1) for g in range(NS_GROUP)], axis=0)
            d0 = delta * TQ + row - col
            d = jnp.concatenate([d0, d0 - TQ], axis=1)
            if kind == "sel":
                ok = (d >= 0) & (jnp.concatenate([mask_ref[kj], mask_ref[kj + 1]], axis=1) > 0.5)
            else:
                ok = (d >= 0) & (d <= WINDOW)
            ok = tile4(ok)
            m_old = m_ref[...]
            m_new = jnp.maximum(m_old, jnp.max(jnp.where(ok, s, NEG_BIG), axis=-1, keepdims=True))
            e = jnp.where(ok, jnp.exp(s - jnp.concatenate([m_new, m_new], axis=1)), 0.0)
            scale = jnp.exp(m_old - m_new)
            l_ref[...] = l_ref[...] * scale + jnp.sum(e, axis=-1, keepdims=True)
            acc_ref[...] = acc_ref[...] * scale + jnp.dot(e.astype(bf16), vt, preferred_element_type=f32)
            m_ref[...] = m_new
            return carry

        lax.fori_loop(lo // 2, qi // 2 + 1, step, 0)
        return acc_ref[...] / jnp.maximum(l_ref[...], 1e-30)

    o_sel = attend(ksel_ref, vsel_ref, 0, "sel")
    o_win = attend(kwin_ref, vwin_ref, jnp.maximum(qi - WINDOW // TQ, 0), "win")

    gates = jax.nn.sigmoid(gate_ref[0, 0].astype(f32))
    outs = []
    for g in range(NS_GROUP):
        sl = slice(g * TQ, (g + 1) * TQ)
        og = (gates[:, g:g + 1] * o_cmp[sl] + gates[:, NS_GROUP + g:NS_GROUP + g + 1] * o_sel[sl]
              + gates[:, 2 * NS_GROUP + g:2 * NS_GROUP + g + 1] * o_win[sl])
        outs.append(og)
    o_ref[0] = jnp.concatenate(outs, axis=-1)


def bias_lookup(rel_bias, d):
    onehot = jax.nn.one_hot(rel_bucket(d), REL_BUCKETS, dtype=jnp.float32)
    return jnp.dot(onehot, rel_bias.astype(jnp.float32), precision=lax.Precision.HIGHEST)


def nsa_bias_tiles(rel_bias, n_tiles):
    iq = jnp.arange(TQ)[:, None]
    ik = jnp.arange(TQ)[None, :]
    dl = jnp.arange(n_tiles)[:, None, None]
    bsel = jnp.transpose(bias_lookup(rel_bias, dl * TQ + iq - ik), (3, 0, 1, 2))
    bcmp = jnp.transpose(bias_lookup(rel_bias, dl * TQ + iq - (ik * CMP_STRIDE + CMP_LEN - 1)), (3, 0, 1, 2))
    return bsel, bcmp


def _cmp_prompt_body(k_ref, v_ref, k1_ref, k2_ref, v1_ref, v2_ref, kc_ref, vc_ref):
    f32, bf16 = jnp.float32, jnp.bfloat16
    n_sub = k_ref.shape[1] // CMP_STRIDE
    half = CMP_STRIDE * NS_HD
    for x_ref, w1_ref, w2_ref, o_ref in ((k_ref, k1_ref, k2_ref, kc_ref), (v_ref, v1_ref, v2_ref, vc_ref)):
        a = jnp.zeros((n_sub, CMP_HIDDEN), f32)
        b = jnp.zeros((n_sub, CMP_HIDDEN), f32)
        for p in range(CMP_STRIDE):
            x = x_ref[0, pl.ds(p, n_sub, stride=CMP_STRIDE), :].astype(bf16)
            a = a + jnp.dot(x, w1_ref[p * NS_HD:(p + 1) * NS_HD, :].astype(bf16), preferred_element_type=f32)
            b = b + jnp.dot(x, w1_ref[half + p * NS_HD:half + (p + 1) * NS_HD, :].astype(bf16), preferred_element_type=f32)
        h = gelu_erf(a + jnp.concatenate([b[1:], b[:1]], axis=0))
        o_ref[0, 0] = jnp.dot(h.astype(bf16), w2_ref[...].astype(bf16), preferred_element_type=f32)


def nsa_compress_prompt(pn, phi_k1, phi_k2, phi_v1, phi_v2):
    B, T, _ = pn.shape
    n_sub = T // CMP_STRIDE
    kv0 = NS_W // NS_HD
    w1 = pl.BlockSpec((CMP_LEN * NS_HD, CMP_HIDDEN), lambda b, k: (0, 0))
    w2 = pl.BlockSpec((CMP_HIDDEN, NS_HD), lambda b, k: (0, 0))
    out = pl.BlockSpec((1, 1, n_sub, NS_HD), lambda b, k: (b, k, 0, 0))
    return pl.pallas_call(
        _cmp_prompt_body,
        grid=(B, NS_KV),
        in_specs=[pl.BlockSpec((1, T, NS_HD), lambda b, k: (b, 0, kv0 + k)),
                  pl.BlockSpec((1, T, NS_HD), lambda b, k: (b, 0, kv0 + NS_KV + k)),
                  w1, w2, w1, w2],
        out_specs=[out, out],
        out_shape=[jax.ShapeDtypeStruct((B, NS_KV, n_sub, NS_HD), jnp.float32)] * 2,
        compiler_params=pltpu.CompilerParams(dimension_semantics=("parallel", "parallel"),
                                             vmem_limit_bytes=V7X_VMEM_LIMIT_BYTES),
        name="nsa_compress_prompt",
    )(pn, pn, phi_k1, phi_k2, phi_v1, phi_v2)


def nsa_prompt_attention(pn, kc, vc, bsel, bcmp):
    B, T, _ = pn.shape
    n_tiles = T // TQ
    assert T % (2 * TQ) == 0 and kc.shape[2] == TQ
    glog = pn[..., NS_W + 6 * NS_KVW:].reshape(B, T, 3, NS_KV, NS_GROUP)
    glog = jnp.transpose(glog, (0, 3, 1, 2, 4)).reshape(B, NS_KV, T, 3 * NS_GROUP)
    kv0 = NS_W // NS_HD

    def kv_spec(slot):
        return pl.BlockSpec((1, T, NS_HD), lambda b, k, i, s=slot: (b, 0, kv0 + s * NS_KV + k))

    R = NS_GROUP * TQ
    return pl.pallas_call(
        functools.partial(_nsa_prompt_body, n_tiles=n_tiles),
        grid=(B, NS_KV, n_tiles),
        in_specs=[
            pl.BlockSpec((1, TQ, NS_GROUP * NS_HD), lambda b, k, i: (b, i, k)),
            kv_spec(2), kv_spec(3), kv_spec(4), kv_spec(5),
            pl.BlockSpec((1, 1, TQ, 3 * NS_GROUP), lambda b, k, i: (b, k, i, 0)),
            pl.BlockSpec((1, 1, TQ, NS_HD), lambda b, k, i: (b, k, 0, 0)),
            pl.BlockSpec((1, 1, TQ, NS_HD), lambda b, k, i: (b, k, 0, 0)),
            pl.BlockSpec((NS_GROUP, n_tiles, TQ, TQ), lambda b, k, i: (k, 0, 0, 0)),
            pl.BlockSpec((NS_GROUP, 1, TQ, TQ), lambda b, k, i: (k, i, 0, 0)),
        ],
        out_specs=pl.BlockSpec((1, TQ, NS_GROUP * NS_HD), lambda b, k, i: (b, i, k)),
        out_shape=jax.ShapeDtypeStruct((B, T, NS_W), jnp.float32),
        scratch_shapes=[pltpu.VMEM((R, NS_HD), jnp.float32), pltpu.VMEM((R, NS_HD), jnp.float32),
                        pltpu.VMEM((R, NS_HD), jnp.float32), pltpu.VMEM((n_tiles, TQ, TQ), jnp.float32)],
        compiler_params=pltpu.CompilerParams(
            dimension_semantics=("parallel", "parallel", "arbitrary"),
            vmem_limit_bytes=V7X_VMEM_LIMIT_BYTES),
        name="nsa_prompt_attention",
    )(pn, pn, pn, pn, pn, glog, kc, vc, bsel, bcmp)


def nsa_prompt(p, phi_k1, phi_k2, phi_v1, phi_v2, bsel, bcmp):
    B, T, _ = p.shape
    kv = p[..., NS_W:NS_W + 6 * NS_KVW].reshape(B, T, 6, NS_KV, NS_HD)
    kc, vc = nsa_compress_prompt(p, phi_k1, phi_k2, phi_v1, phi_v2)
    o = nsa_prompt_attention(p, kc, vc, bsel, bcmp)
    wl = min(WINDOW, T)
    return o, kv[:, :, :4], kv[:, T - wl:, 4:]


CMP_PAGE_GROUP = 8


def _nt_dot(a, b):
    return lax.dot_general(a, b, (((1,), (1,)), ((), ())), preferred_element_type=jnp.float32)


def _cmp_pages_body(pt_ref, page_ref, k1_ref, v1_ref, ab_ref, seq_ref):
    bf16 = jnp.bfloat16
    pg = pl.program_id(1)
    slot_in_group = pg % CMP_PAGE_GROUP
    row0 = pl.multiple_of(slot_in_group * PAGE_SIZE, PAGE_SIZE)
    for c in range(2 * NS_KV):
        seq_ref[c, pl.ds(row0, PAGE_SIZE), :] = page_ref[0, :, c * NS_HD:(c + 1) * NS_HD]

    @pl.when(slot_in_group == CMP_PAGE_GROUP - 1)
    def _():
        n_sub = CMP_PAGE_GROUP * PAGE_SIZE // CMP_STRIDE
        half = CMP_STRIDE * NS_HD
        for c in range(2 * NS_KV):
            w_ref = k1_ref if c < NS_KV else v1_ref
            a = jnp.zeros((n_sub, CMP_HIDDEN), jnp.float32)
            b = jnp.zeros((n_sub, CMP_HIDDEN), jnp.float32)
            for p in range(CMP_STRIDE):
                x = seq_ref[c, pl.ds(p, n_sub, stride=CMP_STRIDE), :].astype(bf16)
                a = a + jnp.dot(x, w_ref[p * NS_HD:(p + 1) * NS_HD, :].astype(bf16), preferred_element_type=jnp.float32)
                b = b + jnp.dot(x, w_ref[half + p * NS_HD:half + (p + 1) * NS_HD, :].astype(bf16), preferred_element_type=jnp.float32)
            ab_ref[0, c] = jnp.concatenate([a, b], axis=-1)


def nsa_compress_pages(cache_l, page_table, phi_k1, phi_v1):
    n_phys = cache_l.shape[0]
    DB, n_pages = page_table.shape
    assert n_pages % CMP_PAGE_GROUP == 0
    cache2 = cache_l.reshape(n_phys, PAGE_SIZE, 4 * NS_KVW)
    subs_per_group = CMP_PAGE_GROUP * PAGE_SIZE // CMP_STRIDE
    n_sub = n_pages * PAGE_SIZE // CMP_STRIDE
    grid_spec = pltpu.PrefetchScalarGridSpec(
        num_scalar_prefetch=1,
        grid=(DB, n_pages),
        in_specs=[pl.BlockSpec((1, PAGE_SIZE, 2 * NS_KVW), lambda b, g, pt: (pt[b, g], 0, 0)),
                  pl.BlockSpec((CMP_LEN * NS_HD, CMP_HIDDEN), lambda b, g, pt: (0, 0)),
                  pl.BlockSpec((CMP_LEN * NS_HD, CMP_HIDDEN), lambda b, g, pt: (0, 0))],
        out_specs=pl.BlockSpec((1, 2 * NS_KV, subs_per_group, 2 * CMP_HIDDEN),
                               lambda b, g, pt: (b, 0, g // CMP_PAGE_GROUP, 0)),
        scratch_shapes=[pltpu.VMEM((2 * NS_KV, CMP_PAGE_GROUP * PAGE_SIZE, NS_HD), jnp.float32)])
    return pl.pallas_call(
        _cmp_pages_body, grid_spec=grid_spec,
        out_shape=jax.ShapeDtypeStruct((DB, 2 * NS_KV, n_sub, 2 * CMP_HIDDEN), jnp.float32),
        compiler_params=pltpu.CompilerParams(dimension_semantics=("parallel", "arbitrary"),
                                             vmem_limit_bytes=V7X_VMEM_LIMIT_BYTES),
        name="nsa_compress_pages",
    )(page_table, cache2, phi_k1, phi_v1)


def _nsa_sample_body(pt_ref, q_ref, ksel_ref, vsel_ref, knew_ref, vnew_ref, win_ref, wnew_ref, gate_ref,
                     ab_ref, k2_ref, v2_ref, bsel_ref, bwin_ref, bcmp_ref, o_ref,
                     m_ref, l_ref, acc_ref, sel_ref, ocmp_ref, *, n_pages, n_new):
    f32, bf16 = jnp.float32, jnp.bfloat16
    pg = pl.program_id(1)
    QG = NS_GROUP * n_new
    R = NS_KV * QG
    n_sub = n_pages * PAGE_SIZE // CMP_STRIDE
    n_cmp = n_sub - 1
    ns = n_pages * (PAGE_SIZE // SEL_BLOCK) + 1
    NSP = sel_ref.shape[1]
    past = n_pages * PAGE_SIZE
    qs = (q_ref[0] * NS_HD ** -0.5).astype(bf16)

    @pl.when(pg == 0)
    def _():
        m_ref[...] = jnp.full((R, NS_HD), NEG_BIG, f32)
        l_ref[...] = jnp.zeros((R, NS_HD), f32)
        acc_ref[...] = jnp.zeros((R, NS_HD), f32)
        ncol = lax.broadcasted_iota(jnp.int32, (QG, n_sub), 1)
        okc = ncol < n_cmp
        orow = lax.broadcasted_iota(jnp.int32, (n_sub, NSP), 0)
        ocol = lax.broadcasted_iota(jnp.int32, (n_sub, NSP), 1)
        overlap = ((orow * CMP_STRIDE < (ocol + 1) * SEL_BLOCK) & (orow * CMP_STRIDE + CMP_LEN > ocol * SEL_BLOCK)
                   & (orow < n_cmp)).astype(f32)
        jcol = lax.broadcasted_iota(jnp.int32, (n_new, NSP), 1)
        qrow = lax.broadcasted_iota(jnp.int32, (n_new, NSP), 0)
        cur = (past + qrow) // SEL_BLOCK
        valid = (jcol <= cur) & (jcol < ns)
        forced = (jcol == 0) | (jcol == cur) | (jcol == cur - 1)
        for kv in range(NS_KV):
            def cmp_of(c, w2_ref):
                a = ab_ref[0, c, :, 0:CMP_HIDDEN]
                b = ab_ref[0, c, :, CMP_HIDDEN:2 * CMP_HIDDEN]
                b = jnp.concatenate([b[1:], b[:1]], axis=0)
                h = gelu_erf(a + b)
                return jnp.dot(h.astype(bf16), w2_ref[...].astype(bf16), preferred_element_type=f32)
            kc = cmp_of(kv, k2_ref)
            vc = cmp_of(NS_KV + kv, v2_ref)
            sc = _nt_dot(qs[kv * QG:(kv + 1) * QG], kc.astype(bf16)) + bcmp_ref[kv * QG:(kv + 1) * QG, :]
            mc = jnp.max(jnp.where(okc, sc, NEG_BIG), axis=-1, keepdims=True)
            ec = jnp.where(okc, jnp.exp(sc - mc), 0.0)
            pc = ec / jnp.maximum(jnp.sum(ec, axis=-1, keepdims=True), 1e-30)
            ocmp_ref[kv * QG:(kv + 1) * QG, :] = jnp.dot(pc.astype(bf16), vc.astype(bf16), preferred_element_type=f32)
            pc_sum = pc[0:n_new]
            for g in range(1, NS_GROUP):
                pc_sum = pc_sum + pc[g * n_new:(g + 1) * n_new]
            ps = jnp.dot(pc_sum, overlap, preferred_element_type=f32, precision=lax.Precision.HIGHEST)
            score = jnp.where(valid, jnp.where(forced, FORCE_SCORE, ps), -jnp.inf)
            rank = jnp.zeros((n_new, NSP), jnp.int32)
            for i in range(ns):
                ci = jnp.broadcast_to(score[:, i:i + 1], (n_new, NSP))
                rank = rank + ((ci > score) | ((ci == score) & (jcol > i))).astype(jnp.int32)
            sel = (valid & (rank < min(SEL_TOPK, ns))).astype(f32)
            sel_ref[kv * QG:(kv + 1) * QG, :] = jnp.concatenate([sel] * NS_GROUP, axis=0)

    is_new = pg == n_pages
    krow = lax.broadcasted_iota(jnp.int32, (NSP, PAGE_SIZE), 0)
    kcol = lax.broadcasted_iota(jnp.int32, (NSP, PAGE_SIZE), 1)
    expand = (krow == pg * (PAGE_SIZE // SEL_BLOCK) + kcol // SEL_BLOCK).astype(bf16)
    inblock = jnp.dot(sel_ref[...].astype(bf16), expand, preferred_element_type=f32) > 0.5
    rr = lax.broadcasted_iota(jnp.int32, (R, PAGE_SIZE), 0)
    cc = lax.broadcasted_iota(jnp.int32, (R, PAGE_SIZE), 1)
    causal_new = (cc <= rr % n_new) & (cc < n_new)
    ok = inblock & (jnp.logical_not(is_new) | causal_new)
    kpage = jnp.where(is_new, knew_ref[0], ksel_ref[0])
    vpage = jnp.where(is_new, vnew_ref[0], vsel_ref[0])
    s = jnp.concatenate([_nt_dot(qs[kv * QG:(kv + 1) * QG], kpage[:, kv * NS_HD:(kv + 1) * NS_HD].astype(bf16))
                         for kv in range(NS_KV)], axis=0) + bsel_ref[...]
    m_old = m_ref[...]
    m_new = jnp.maximum(m_old, jnp.max(jnp.where(ok, s, NEG_BIG), axis=-1, keepdims=True))
    e = jnp.where(ok, jnp.exp(s - m_new), 0.0)
    scale = jnp.exp(m_old - m_new)
    l_ref[...] = l_ref[...] * scale + jnp.sum(e, axis=-1, keepdims=True)
    pv = jnp.concatenate([jnp.dot(e[kv * QG:(kv + 1) * QG].astype(bf16), vpage[:, kv * NS_HD:(kv + 1) * NS_HD].astype(bf16),
                                  preferred_element_type=f32) for kv in range(NS_KV)], axis=0)
    acc_ref[...] = acc_ref[...] * scale + pv
    m_ref[...] = m_new

    @pl.when(is_new)
    def _():
        o_sel = acc_ref[...] / jnp.maximum(l_ref[...], 1e-30)
        Wb = win_ref.shape[1]
        wc = lax.broadcasted_iota(jnp.int32, (R, Wb + PAGE_SIZE), 1)
        wr = lax.broadcasted_iota(jnp.int32, (R, Wb + PAGE_SIZE), 0) % n_new
        dw = (Wb + wr) - wc
        okw = (dw >= 0) & (dw <= WINDOW) & ((wc < Wb) | (wc - Wb < n_new))
        sw = []
        for kv in range(NS_KV):
            kw = jnp.concatenate([win_ref[0, :, kv * NS_HD:(kv + 1) * NS_HD],
                                  wnew_ref[0, :, kv * NS_HD:(kv + 1) * NS_HD]], axis=0).astype(bf16)
            sw.append(_nt_dot(qs[kv * QG:(kv + 1) * QG], kw))
        sw = jnp.concatenate(sw, axis=0) + bwin_ref[...]
        mw = jnp.max(jnp.where(okw, sw, NEG_BIG), axis=-1, keepdims=True)
        ew = jnp.where(okw, jnp.exp(sw - mw), 0.0)
        pw = (ew / jnp.maximum(jnp.sum(ew, axis=-1, keepdims=True), 1e-30)).astype(bf16)
        o_win = []
        for kv in range(NS_KV):
            vw = jnp.concatenate([win_ref[0, :, NS_KVW + kv * NS_HD:NS_KVW + (kv + 1) * NS_HD],
                                  wnew_ref[0, :, NS_KVW + kv * NS_HD:NS_KVW + (kv + 1) * NS_HD]], axis=0).astype(bf16)
            o_win.append(jnp.dot(pw[kv * QG:(kv + 1) * QG], vw, preferred_element_type=f32))
        o_win = jnp.concatenate(o_win, axis=0)
        gates = jax.nn.sigmoid(gate_ref[0].astype(f32))
        o = gates[:, 0:1] * ocmp_ref[...] + gates[:, 1:2] * o_sel + gates[:, 2:3] * o_win
        o_ref[0] = jnp.concatenate([o[h * n_new:(h + 1) * n_new] for h in range(NS_HEADS)], axis=-1)


def nsa_sample_bias(rel_bias, n_new, past, wb):
    n_sub = past // CMP_STRIDE
    tq = past + jnp.arange(n_new)[:, None]
    pos = jnp.arange(past + PAGE_SIZE)[None, :]
    R = NS_HEADS * n_new
    bsel = jnp.transpose(bias_lookup(rel_bias, tq - pos), (2, 0, 1)).reshape(R, past + PAGE_SIZE)
    cpos = (jnp.arange(n_sub) * CMP_STRIDE + CMP_LEN - 1)[None, :]
    bcmp = jnp.transpose(bias_lookup(rel_bias, tq - cpos), (2, 0, 1)).reshape(R, n_sub)
    return bsel, bsel[:, past - wb:], bcmp


def nsa_sample_attention(ps, cache_l, page_table, win_buf, ab, phi_k2, phi_v2, bias):
    bsel, bwin, bcmp = bias
    DB, Tn, _ = ps.shape
    n_phys = cache_l.shape[0]
    n_pages = page_table.shape[1]
    Wb = win_buf.shape[1]
    R = NS_HEADS * Tn
    ns = n_pages * (PAGE_SIZE // SEL_BLOCK) + 1
    NSP = -(-ns // 128) * 128
    n_sub = n_pages * PAGE_SIZE // CMP_STRIDE
    cache2 = cache_l.reshape(n_phys, PAGE_SIZE, 4 * NS_KVW)
    q = jnp.transpose(ps[..., :NS_W].reshape(DB, Tn, NS_HEADS, NS_HD), (0, 2, 1, 3)).reshape(DB, R, NS_HD)
    kvn = ps[..., NS_W:NS_W + 6 * NS_KVW].reshape(DB, Tn, 6, NS_KVW)
    padn = ((0, 0), (0, PAGE_SIZE - Tn), (0, 0))
    knew = jnp.pad(kvn[:, :, 2], padn)
    vnew = jnp.pad(kvn[:, :, 3], padn)
    wnew = jnp.pad(jnp.concatenate([kvn[:, :, 4], kvn[:, :, 5]], axis=-1), padn)
    win2 = win_buf.reshape(DB, Wb, 2 * NS_KVW)
    glog = jnp.transpose(ps[..., NS_W + 6 * NS_KVW:].reshape(DB, Tn, 3, NS_HEADS), (0, 3, 1, 2)).reshape(DB, R, 3)
    last = n_pages - 1
    grid_spec = pltpu.PrefetchScalarGridSpec(
        num_scalar_prefetch=1,
        grid=(DB, n_pages + 1),
        in_specs=[
            pl.BlockSpec((1, R, NS_HD), lambda b, g, pt: (b, 0, 0)),
            pl.BlockSpec((1, PAGE_SIZE, NS_KVW), lambda b, g, pt: (pt[b, jnp.minimum(g, last)], 0, 2)),
            pl.BlockSpec((1, PAGE_SIZE, NS_KVW), lambda b, g, pt: (pt[b, jnp.minimum(g, last)], 0, 3)),
            pl.BlockSpec((1, PAGE_SIZE, NS_KVW), lambda b, g, pt: (b, 0, 0)),
            pl.BlockSpec((1, PAGE_SIZE, NS_KVW), lambda b, g, pt: (b, 0, 0)),
            pl.BlockSpec((1, Wb, 2 * NS_KVW), lambda b, g, pt: (b, 0, 0)),
            pl.BlockSpec((1, PAGE_SIZE, 2 * NS_KVW), lambda b, g, pt: (b, 0, 0)),
            pl.BlockSpec((1, R, 3), lambda b, g, pt: (b, 0, 0)),
            pl.BlockSpec((1, 2 * NS_KV, n_sub, 2 * CMP_HIDDEN), lambda b, g, pt: (b, 0, 0, 0)),
            pl.BlockSpec((CMP_HIDDEN, NS_HD), lambda b, g, pt: (0, 0)),
            pl.BlockSpec((CMP_HIDDEN, NS_HD), lambda b, g, pt: (0, 0)),
            pl.BlockSpec((R, PAGE_SIZE), lambda b, g, pt: (0, g)),
            pl.BlockSpec((R, Wb + PAGE_SIZE), lambda b, g, pt: (0, 0)),
            pl.BlockSpec((R, n_sub), lambda b, g, pt: (0, 0)),
        ],
        out_specs=pl.BlockSpec((1, Tn, NS_W), lambda b, g, pt: (b, 0, 0)),
        scratch_shapes=[pltpu.VMEM((R, NS_HD), jnp.float32), pltpu.VMEM((R, NS_HD), jnp.float32),
                        pltpu.VMEM((R, NS_HD), jnp.float32), pltpu.VMEM((R, NSP), jnp.float32),
                        pltpu.VMEM((R, NS_HD), jnp.float32)])
    return pl.pallas_call(
        functools.partial(_nsa_sample_body, n_pages=n_pages, n_new=Tn),
        grid_spec=grid_spec,
        out_shape=jax.ShapeDtypeStruct((DB, Tn, NS_W), jnp.float32),
        compiler_params=pltpu.CompilerParams(dimension_semantics=("parallel", "arbitrary"),
                                             vmem_limit_bytes=V7X_VMEM_LIMIT_BYTES),
        name="nsa_sample_attention",
    )(page_table, q, cache2, cache2, knew, vnew, win2, wnew, glog, ab, phi_k2, phi_v2, bsel, bwin, bcmp)


def nsa_sample(p, cache_kv_l, page_table, win_buf, phi_k1, phi_k2, phi_v1, phi_v2, bias):
    DB, Tn, _ = p.shape
    assert Tn < CMP_STRIDE and Tn <= SEL_BLOCK
    kv = p[..., NS_W:NS_W + 6 * NS_KVW].reshape(DB, Tn, 6, NS_KV, NS_HD)
    ab = nsa_compress_pages(cache_kv_l, page_table, phi_k1, phi_v1)
    o = nsa_sample_attention(p, cache_kv_l, page_table, win_buf, ab, phi_k2, phi_v2, bias)
    win = jnp.concatenate([win_buf, kv[:, :, 4:].astype(win_buf.dtype)], axis=1)
    return o, kv[:, :, :4], win[:, Tn:]


PEER_ROUTE_TM = 128
PEER_ROUTE_UNROLL = 4


def gelu_erf(x):
    return 0.5 * x * (1.0 + lax.erf(x * (2.0 ** -0.5)))


def _top_rows(work, n_rows, k, row_iota):
    vals, idxs = [], []
    for _ in range(k):
        m = jnp.max(work, axis=0, keepdims=True)
        idx = jnp.min(jnp.where(work == m, row_iota, n_rows), axis=0, keepdims=True)
        vals.append(m)
        idxs.append(idx)
        work = jnp.where(row_iota == idx, -jnp.inf, work)
    return vals, idxs


def _peer_route_body(q_ref, k1_ref, k2_ref, g_ref, i1_s, i2_s, w_s):
    f32, bf16 = jnp.float32, jnp.bfloat16
    tm = q_ref.shape[0]
    half = PEER_DKEY // 2
    K = PEER_TOPK
    rows = lax.broadcasted_iota(jnp.int32, (PEER_NKEYS, tm), 0)
    crow = lax.broadcasted_iota(jnp.int32, (K * K, tm), 0)
    k1 = k1_ref[...].astype(bf16)
    k2 = k2_ref[...].astype(bf16)

    def nt_dot(a, b):
        return lax.dot_general(a, b, (((1,), (1,)), ((), ())), preferred_element_type=f32)

    for h in range(PEER_HEADS):
        q1 = q_ref[:, h * PEER_DKEY:h * PEER_DKEY + half].astype(bf16)
        q2 = q_ref[:, h * PEER_DKEY + half:(h + 1) * PEER_DKEY].astype(bf16)
        v1, i1 = _top_rows(nt_dot(k1, q1), PEER_NKEYS, K, rows)
        v2, i2 = _top_rows(nt_dot(k2, q2), PEER_NKEYS, K, rows)
        v2m = jnp.concatenate(v2, axis=0)
        i2m = jnp.concatenate(i2, axis=0)
        cand = jnp.concatenate([v1[a] + v2m for a in range(K)], axis=0)
        cidx = jnp.concatenate([i1[a] * PEER_NKEYS + i2m for a in range(K)], axis=0)
        sv, pos = _top_rows(cand, K * K, K, crow)
        eidx = [jnp.max(jnp.where(crow == pos[k], cidx, 0), axis=0, keepdims=True) for k in range(K)]
        svm = jnp.concatenate(sv, axis=0)
        em = jnp.concatenate(eidx, axis=0)
        e = jnp.exp(svm - svm[0:1])
        gw = e / jnp.sum(e, axis=0, keepdims=True)
        i1_s[h * K:(h + 1) * K, :] = (em // PEER_NKEYS).astype(f32)
        i2_s[h * K:(h + 1) * K, :] = (em % PEER_NKEYS).astype(f32)
        w_s[h * K:(h + 1) * K, :] = gw
    i1_s[...] = i1_s[...].T
    i2_s[...] = i2_s[...].T
    w_s[...] = w_s[...].T
    sub = lax.broadcasted_iota(jnp.int32, (PEER_NKEYS, PEER_HEADS * K), 0).astype(f32)

    def token_group(tg, carry):
        ts = [tg * PEER_ROUTE_UNROLL + u for u in range(PEER_ROUTE_UNROLL)]
        a_w = [jnp.where(sub == i1_s[pl.ds(t, 1), :], w_s[pl.ds(t, 1), :], 0.0).astype(bf16) for t in ts]
        b_1 = [jnp.where(sub == i2_s[pl.ds(t, 1), :], 1.0, 0.0).astype(bf16) for t in ts]
        g = [nt_dot(a, b) for a, b in zip(a_w, b_1)]
        for t, gt in zip(ts, g):
            g_ref[t] = gt.astype(g_ref.dtype)
        return carry

    lax.fori_loop(0, tm // PEER_ROUTE_UNROLL, token_group, 0)


def peer_route(q, k1, k2):
    n = q.shape[0]
    tm = PEER_ROUTE_TM
    S = PEER_HEADS * PEER_TOPK
    assert n % tm == 0 and S == tm
    g = pl.pallas_call(
        _peer_route_body,
        grid=(n // tm,),
        in_specs=[pl.BlockSpec((tm, PEER_HEADS * PEER_DKEY), lambda i: (i, 0)),
                  pl.BlockSpec((PEER_NKEYS, PEER_DKEY // 2), lambda i: (0, 0)),
                  pl.BlockSpec((PEER_NKEYS, PEER_DKEY // 2), lambda i: (0, 0))],
        out_specs=pl.BlockSpec((tm, PEER_NKEYS, PEER_NKEYS), lambda i: (i, 0, 0)),
        out_shape=jax.ShapeDtypeStruct((n, PEER_NKEYS, PEER_NKEYS), jnp.bfloat16),
        scratch_shapes=[pltpu.VMEM((S, tm), jnp.float32)] * 3,
        compiler_params=pltpu.CompilerParams(dimension_semantics=("parallel",),
                                             vmem_limit_bytes=V7X_VMEM_LIMIT_BYTES),
        name="peer_route",
    )(q, k1, k2)
    return g.reshape(n, PEER_EXPERTS)


def _peer_expert_body(x_ref, g_ref, u_ref, v_ref, o_ref):
    f32, bf16 = jnp.float32, jnp.bfloat16
    e = pl.program_id(1)
    h = lax.dot_general(x_ref[...], u_ref[...], (((1,), (1,)), ((), ())), preferred_element_type=f32)
    p = (g_ref[...].astype(f32) * gelu_erf(h)).astype(bf16)
    upd = jnp.dot(p, v_ref[...], preferred_element_type=f32)

    @pl.when(e == 0)
    def _():
        o_ref[...] = upd

    @pl.when(e > 0)
    def _():
        o_ref[...] += upd


def peer_experts(x, g, u, v, tm=512, te=512):
    n, D = x.shape
    E = u.shape[0]
    tm = min(tm, n)
    assert n % tm == 0 and E % te == 0
    return pl.pallas_call(
        _peer_expert_body,
        grid=(n // tm, E // te),
        in_specs=[pl.BlockSpec((tm, D), lambda i, e: (i, 0)),
                  pl.BlockSpec((tm, te), lambda i, e: (i, e)),
                  pl.BlockSpec((te, D), lambda i, e: (e, 0)),
                  pl.BlockSpec((te, D), lambda i, e: (e, 0))],
        out_specs=pl.BlockSpec((tm, D), lambda i, e: (i, 0)),
        out_shape=jax.ShapeDtypeStruct((n, D), jnp.float32),
        compiler_params=pltpu.CompilerParams(dimension_semantics=("parallel", "arbitrary"),
                                             vmem_limit_bytes=V7X_VMEM_LIMIT_BYTES),
        name="peer_experts",
    )(x, g, u, v)


def peer_ffn(x, wq, k1, k2, u_bf, v_bf):
    Bx, T, D = x.shape
    n = Bx * T
    pad = -n % PEER_ROUTE_TM
    xt = jnp.pad(x.reshape(n, D), ((0, pad), (0, 0)))
    g = peer_route(matmul(xt, wq), k1, k2)
    out = peer_experts(xt.astype(jnp.bfloat16), g, u_bf, v_bf)
    return out[:n].reshape(Bx, T, D).astype(x.dtype)


def residual_block(x, mix, w_out, ln1_g, ln1_b, ln2_g, ln2_b, peer_wq, peer_k1, peer_k2, peer_u, peer_v):
    x = layer_norm(ALPHA * x + matmul3(mix, w_out), ln1_g, ln1_b)
    return layer_norm(ALPHA * x + peer_ffn(x, peer_wq, peer_k1, peer_k2, peer_u, peer_v), ln2_g, ln2_b)


def kernel(x_prompt, x_sample, cache_kv, cache_win, state_rwkv, state_rwkv_shift, state_gdn, state_gdn_conv, page_table, w_in, w_out, ln1_g, ln1_b, ln2_g, ln2_b, rw_mu, rw_w0, rw_w_up, rw_a0, rw_a_up, rw_g_up, rw_k_k, rw_k_a, rw_r_k, rw_ln_g, rw_ln_b, gd_conv_w, gd_a_log, gd_dt_bias, gd_norm_g, ns_phi_k1, ns_phi_k2, ns_phi_v1, ns_phi_v2, rel_bias, peer_wq, peer_k1, peer_k2, peer_u, peer_v):
    xp, xs = x_prompt, x_sample
    B = xp.shape[0]
    o_b = RW_COLS
    o_c = RW_COLS + GD_COLS
    bsel, bcmp = nsa_bias_tiles(rel_bias, SEQ // TQ)
    sbias = nsa_sample_bias(rel_bias, x_sample.shape[1], page_table.shape[1] * PAGE_SIZE, cache_win.shape[2])
    w_in_bf, w_out_bf, wq_bf = (w.astype(jnp.bfloat16) for w in (w_in, w_out, peer_wq))
    kv_p, kv_s, win_p, win_s, rw_p, rw_s, sh_p, sh_s, gd_p, gd_s, cv_p, cv_s = ([] for _ in range(12))
    for l in range(DEPTH):
        rw = (rw_mu[l], rw_w0[l], rw_w_up[l], rw_a0[l], rw_a_up[l], rw_g_up[l], rw_k_k[l], rw_k_a[l], rw_r_k[l], rw_ln_g[l], rw_ln_b[l])
        gd = (gd_conv_w[l], gd_a_log[l], gd_dt_bias[l], gd_norm_g[l])
        phi = (ns_phi_k1[l], ns_phi_k2[l], ns_phi_v1[l], ns_phi_v2[l])
        tail = (w_out_bf[l], ln1_g[l], ln1_b[l], ln2_g[l], ln2_b[l], wq_bf[l], peer_k1[l], peer_k2[l],
                peer_u[l].astype(jnp.bfloat16), peer_v[l].astype(jnp.bfloat16))
        pp = matmul3(xp, w_in_bf[l])
        a, sh, rs = rwkv7_mixer(pp[..., :o_b], jnp.zeros((B, RW_COLS), pp.dtype), jnp.zeros((B, RW_HEADS, RW_HD, RW_HD), jnp.float32), *rw)
        b, cv, gs = gated_deltanet_mixer(pp[..., o_b:o_c], jnp.zeros((B, GD_CONV - 1, GD_QKV), pp.dtype), jnp.zeros((B, GD_HEADS, GD_HD, GD_HD), jnp.float32), *gd)
        c, kvr, wr = nsa_prompt(pp[..., o_c:], *phi, bsel, bcmp)
        xp = residual_block(xp, jnp.concatenate([a, b, c.astype(a.dtype)], -1), *tail)
        kv_p.append(kvr)
        win_p.append(wr)
        rw_p.append(rs)
        sh_p.append(sh)
        gd_p.append(gs)
        cv_p.append(cv)
        ps = matmul3(xs, w_in_bf[l])
        a, sh, rs = rwkv7_mixer(ps[..., :o_b], state_rwkv_shift[l], state_rwkv[l], *rw)
        b, cv, gs = gated_deltanet_mixer(ps[..., o_b:o_c], state_gdn_conv[l], state_gdn[l], *gd)
        c, kvr, wr = nsa_sample(ps[..., o_c:], cache_kv[l], page_table, cache_win[l], *phi, sbias)
        xs = residual_block(xs, jnp.concatenate([a, b, c.astype(a.dtype)], -1), *tail)
        kv_s.append(kvr)
        win_s.append(wr)
        rw_s.append(rs)
        sh_s.append(sh)
        gd_s.append(gs)
        cv_s.append(cv)
    st = jnp.stack
    return (xp, xs, st(kv_p), st(kv_s), st(win_p), st(win_s), st(rw_p), st(rw_s), st(sh_p), st(sh_s), st(gd_p), st(gd_s), st(cv_p), st(cv_s))
```

```python
import functools
import math

import jax
import jax.numpy as jnp
from jax import lax
from jax.experimental import pallas as pl
from jax.experimental.pallas import tpu as pltpu

D_MODEL = 4096
BATCH = 4
SEQ = 2048
DEPTH = 4
DEC_BATCH = 8
DEC_SEQ = 8
PAST_LEN = 8192
PAGE_SIZE = 128

ALPHA = (2 * DEPTH) ** 0.25
LN_EPS = 1e-5

RW_HD = 64
RW_W = D_MODEL // 4
RW_HEADS = RW_W // RW_HD
RW_DECAY_R = 64
RW_AAA_R = 64
RW_GATE_R = 160
RW_COLS = 3 * RW_W + RW_DECAY_R + RW_AAA_R + RW_GATE_R
RW_GN_EPS = 64e-5

GD_HD = 128
GD_W = D_MODEL // 4
GD_HEADS = GD_W // GD_HD
GD_QKV = 3 * GD_W
GD_CONV = 4
GD_CHUNK = 64
GD_COLS = GD_QKV + GD_W + 2 * GD_HEADS

NS_HD = 128
NS_W = D_MODEL // 2
NS_HEADS = NS_W // NS_HD
NS_KV = 4
NS_GROUP = NS_HEADS // NS_KV
NS_KVW = NS_KV * NS_HD
NS_COLS = NS_W + 6 * NS_KVW + 3 * NS_HEADS
CMP_LEN = 32
CMP_STRIDE = 16
CMP_HIDDEN = 128
SEL_BLOCK = 64
SEL_TOPK = 16
WINDOW = 512
NS_QBLOCK = 32
FORCE_SCORE = 1e4

REL_BUCKETS = 32
REL_MAX_DIST = 1024

D_MIX = RW_W + GD_W + NS_W
IN_COLS = RW_COLS + GD_COLS + NS_COLS

PEER_HEADS = 8
PEER_NKEYS = 128
PEER_EXPERTS = PEER_NKEYS ** 2
PEER_DKEY = 256
PEER_TOPK = 16
PEER_TBLOCK = 128

V7X_VMEM_LIMIT_BYTES = 56 * 1024 * 1024


def _matmul_body(x_ref, w_ref, o_ref):
    o_ref[...] = jnp.dot(x_ref[...].astype(jnp.bfloat16), w_ref[...].astype(jnp.bfloat16),
                         preferred_element_type=jnp.float32)


def _pick_tile(n, target):
    t = min(n, target)
    while n % t:
        t //= 2
    return t


def matmul(x, w, layer, tm=512, tn=1024):
    M, K = x.shape
    N = w.shape[2]
    tm = _pick_tile(M, tm)
    tn = min(tn, N)
    return pl.pallas_call(
        _matmul_body,
        grid=(M // tm, pl.cdiv(N, tn)),
        in_specs=[pl.BlockSpec((tm, K), lambda i, j: (i, 0)),
                  pl.BlockSpec((None, K, tn), lambda i, j: (layer, 0, j))],
        out_specs=pl.BlockSpec((tm, tn), lambda i, j: (i, j)),
        out_shape=jax.ShapeDtypeStruct((M, N), jnp.float32),
        compiler_params=pltpu.CompilerParams(
            dimension_semantics=("parallel", "parallel"),
            vmem_limit_bytes=V7X_VMEM_LIMIT_BYTES),
        name="proj_matmul",
    )(x, w)


def matmul3(x, w, layer):
    B, T, K = x.shape
    return matmul(x.reshape(B * T, K), w, layer).reshape(B, T, -1)


def layer_norm(x, g, b):
    xf = x.astype(jnp.float32)
    mu = jnp.mean(xf, -1, keepdims=True)
    var = jnp.mean(jnp.square(xf - mu), -1, keepdims=True)
    return ((xf - mu) * lax.rsqrt(var + LN_EPS) * g + b).astype(x.dtype)


def l2_normalize(x):
    return x / jnp.maximum(jnp.sqrt(jnp.sum(x * x, -1, keepdims=True)), 1e-12)


def rel_bucket(d):
    d = jnp.maximum(d, 0)
    exact = REL_BUCKETS // 2
    logd = jnp.log(jnp.maximum(d, 1).astype(jnp.float32) / exact) / math.log(REL_MAX_DIST / exact)
    large = jnp.minimum(exact + (logd * (REL_BUCKETS - exact)).astype(jnp.int32), REL_BUCKETS - 1)
    return jnp.where(d < exact, d, large)


RW_CHUNK = 64


def _split(x):
    hi = x.astype(jnp.bfloat16)
    lo = (x - hi.astype(jnp.float32)).astype(jnp.bfloat16)
    return hi, lo


def _dot3(a, b, dims=(((1,), (0,)), ((), ()))):
    ah, al = _split(a)
    bh, bl = _split(b)
    d = lambda x, y: lax.dot_general(x, y, dims, preferred_element_type=jnp.float32)
    return d(ah, bh) + (d(ah, bl) + d(al, bh))


_NT = (((1,), (1,)), ((), ()))
_TN = (((0,), (0,)), ((), ()))


def _rwkv_chunk_body(r_ref, lw_ref, k_ref, v_ref, kk_ref, ka_ref, s0_ref, y_ref, s_ref, *, n_chunks, n_valid, n_heads):
    f32 = jnp.float32
    N, C = RW_HD, RW_CHUNK
    c_idx = pl.program_id(2)

    @pl.when(c_idx == 0)
    def _():
        s_ref[...] = s0_ref[...]

    row = lax.broadcasted_iota(jnp.int32, (C, C), 0)
    col = lax.broadcasted_iota(jnp.int32, (C, C), 1)
    tril = (row >= col).astype(f32)
    strict = row > col

    hs = range(n_heads)
    for ci in range(n_chunks):
        rows = slice(ci * C, (ci + 1) * C)
        padded = n_valid < n_chunks * C
        live = (lax.broadcasted_iota(jnp.int32, (C, N), 0) + ci * C) < n_valid

        def ld(ref, h):
            x = ref[0, rows, h * N:(h + 1) * N]
            return jnp.where(live, x, 0.0) if padded else x

        lw = [ld(lw_ref, h) for h in hs]
        G = [_dot3(tril, lw[h]) for h in hs]
        eg = [jnp.exp(G[h]) for h in hs]
        ing = [jnp.exp(-G[h]) for h in hs]
        ar = [jnp.concatenate([-ld(kk_ref, h) * jnp.exp(G[h] - lw[h]), ld(r_ref, h) * eg[h]], axis=0) for h in hs]
        bk = [jnp.concatenate([ld(ka_ref, h) * ing[h], ld(k_ref, h) * ing[h]], axis=0) for h in hs]
        v = [ld(v_ref, h) for h in hs]
        S0 = [s_ref[0, h] for h in hs]
        M = [_dot3(ar[h], bk[h], _NT) for h in hs]
        AS = [_dot3(ar[h], S0[h], _NT) for h in hs]
        P = [jnp.where(strict, M[h][0:C, 0:C], 0.0) for h in hs]
        M2 = [jnp.where(strict, M[h][0:C, C:2 * C], 0.0) for h in hs]
        M34 = [jnp.concatenate([M[h][C:2 * C, 0:C] * tril, M[h][C:2 * C, C:2 * C] * tril], axis=1) for h in hs]
        rhs = [AS[h][0:C] + _dot3(M2[h], v[h]) for h in hs]
        X = [rhs[h] + _dot3(P[h], rhs[h]) for h in hs]
        for _ in range(5):
            P = [_dot3(P[h], P[h]) for h in hs]
            X = [X[h] + _dot3(P[h], X[h]) for h in hs]
        sav = [jnp.concatenate([X[h], v[h]], axis=0) for h in hs]
        for h in hs:
            y_ref[0, rows, h * N:(h + 1) * N] = AS[h][C:2 * C] + _dot3(M34[h], sav[h])
        dS = [_dot3(sav[h], bk[h], _TN) for h in hs]
        for h in hs:
            s_ref[0, h] = (S0[h] + dS[h]) * eg[h][C - 1:C, :]


def rwkv_scan_chunked(r, lw, k, v, kk, ka, s0, chunks_per_step=2, heads_per_step=8):
    B, T, W = r.shape
    N, C = RW_HD, RW_CHUNK
    H = W // N
    HB = min(heads_per_step, H)
    TB = C * chunks_per_step if T >= C * chunks_per_step else -(-T // C) * C
    Tp = -(-T // TB) * TB
    n_valid = T if Tp != T else TB

    def prep(x):
        return jnp.pad(x, ((0, 0), (0, Tp - T), (0, 0))) if Tp != T else x

    ins = [prep(x) for x in (r, lw, k, v, kk, ka)]
    seq = pl.BlockSpec((1, TB, HB * N), lambda b, p, c: (b, c, p))
    st = pl.BlockSpec((1, HB, N, N), lambda b, p, c: (b, p, 0, 0))
    y, sT = pl.pallas_call(
        functools.partial(_rwkv_chunk_body, n_chunks=TB // C, n_valid=n_valid, n_heads=HB),
        grid=(B, H // HB, Tp // TB),
        in_specs=[seq] * 6 + [st],
        out_specs=[seq, st],
        out_shape=[jax.ShapeDtypeStruct((B, Tp, W), jnp.float32), jax.ShapeDtypeStruct((B, H, N, N), jnp.float32)],
        compiler_params=pltpu.CompilerParams(dimension_semantics=("parallel", "parallel", "arbitrary"),
                                             vmem_limit_bytes=V7X_VMEM_LIMIT_BYTES),
        name="rwkv_chunked",
    )(*ins, s0)
    return y[:, :T], sT


def rwkv7_mixer(p, shift0, s0, mu, w0, w_up, a0, a_up, g_up, k_k, k_a, r_k, ln_g, ln_b):
    f32 = jnp.float32
    B, T, _ = p.shape
    prev = jnp.concatenate([shift0[:, None].astype(p.dtype), p[:, :-1]], axis=1)
    m = p + mu * (prev - p)
    r = m[..., :RW_W]
    k = m[..., RW_W:2 * RW_W]
    v = m[..., 2 * RW_W:3 * RW_W]
    o = 3 * RW_W
    wl = m[..., o:o + RW_DECAY_R]
    o += RW_DECAY_R
    al = m[..., o:o + RW_AAA_R]
    o += RW_AAA_R
    gl = m[..., o:o + RW_GATE_R]
    w = -jax.nn.softplus(-(w0 + jnp.tanh(wl) @ w_up).astype(f32)) - 0.5
    log_decay = -jnp.exp(w)
    a = jax.nn.sigmoid((a0 + al @ a_up).astype(f32))
    g = jax.nn.sigmoid(gl) @ g_up

    def heads(t):
        return t.reshape(B, T, RW_HEADS, RW_HD).astype(f32)

    kk = l2_normalize(heads(k * k_k)).reshape(B, T, RW_W)
    k = k * (1.0 + (a - 1.0) * k_a)
    r_, k_, v_ = heads(r), heads(k), heads(v)

    y, sT = rwkv_scan_chunked(r, log_decay, k, v, kk, kk * a, s0.astype(f32))
    y = heads(y)
    ym = jnp.mean(y, -1, keepdims=True)
    yv = jnp.mean(jnp.square(y - ym), -1, keepdims=True)
    y = ((y - ym) * lax.rsqrt(yv + RW_GN_EPS)).reshape(B, T, RW_W) * ln_g + ln_b
    bonus = jnp.sum(r_ * k_ * r_k, -1, keepdims=True) * v_
    y = (y + bonus.reshape(B, T, RW_W)) * g
    return y.astype(p.dtype), p[:, -1], sT


def _dot1(a, b, dims=(((1,), (0,)), ((), ()))):
    return lax.dot_general(a.astype(jnp.bfloat16), b.astype(jnp.bfloat16), dims, preferred_element_type=jnp.float32)


def _gdn_body(q_ref, k_ref, v_ref, beta_ref, g_ref, s0_ref, o_ref, s_ref, *, n_chunks, n_valid, n_heads):
    f32 = jnp.float32
    D, C = GD_HD, GD_CHUNK
    hg = pl.program_id(1)
    c_idx = pl.program_id(2)

    @pl.when(c_idx == 0)
    def _():
        s_ref[...] = s0_ref[...]

    row = lax.broadcasted_iota(jnp.int32, (C, C), 0)
    col = lax.broadcasted_iota(jnp.int32, (C, C), 1)
    tri = row >= col
    trif = tri.astype(f32)
    strict = row > col
    eye = row == col
    ones = jnp.ones((C, C), f32)
    lane_h = lax.broadcasted_iota(jnp.int32, (C, beta_ref.shape[2]), 1)
    hs = range(n_heads)
    for ci in range(n_chunks):
        rows = slice(ci * C, (ci + 1) * C)
        padded = n_valid < n_chunks * C
        live = (lax.broadcasted_iota(jnp.int32, (C, 1), 0) + ci * C) < n_valid

        def ld(ref, h):
            x = ref[0, rows, h * D:(h + 1) * D]
            return jnp.where(live, x, 0.0) if padded else x

        g_all = g_ref[0, rows, :]
        b_all = beta_ref[0, rows, :]
        if padded:
            g_all = jnp.where(live, g_all, 0.0)
            b_all = jnp.where(live, b_all, 0.0)
        G_all = _dot3(trif, g_all)

        def colof(x, h):
            return jnp.sum(jnp.where(lane_h == hg * n_heads + h, x, 0.0), axis=1, keepdims=True)

        Gc = [colof(G_all, h) for h in hs]
        bc = [colof(b_all, h) for h in hs]
        GB = [jnp.broadcast_to(Gc[h], (C, C)) for h in hs]
        GR = [_dot3(ones, jnp.where(eye, GB[h], 0.0)) for h in hs]
        decay = [jnp.where(tri, jnp.exp(jnp.where(tri, GB[h] - GR[h], 0.0)), 0.0) for h in hs]
        eG = [jnp.exp(Gc[h]) for h in hs]
        Glast = [Gc[h][C - 1:C, :] for h in hs]
        q = [ld(q_ref, h) for h in hs]
        k = [ld(k_ref, h) for h in hs]
        v = [ld(v_ref, h) for h in hs]
        kb = [k[h] * bc[h] for h in hs]
        P = [-jnp.where(strict, _dot1(kb[h], k[h], _NT) * decay[h], 0.0) for h in hs]
        Aqk = [_dot1(q[h], k[h], _NT) * decay[h] for h in hs]
        rhs = [jnp.concatenate([v[h] * bc[h], kb[h] * eG[h]], axis=1) for h in hs]
        X = [rhs[h] + _dot3(P[h], rhs[h]) for h in hs]
        for _ in range(5):
            P = [_dot3(P[h], P[h]) for h in hs]
            X = [X[h] + _dot3(P[h], X[h]) for h in hs]
        S = [s_ref[0, h] for h in hs]
        vn = [X[h][:, 0:D] - _dot1(X[h][:, D:2 * D], S[h]) for h in hs]
        for h in hs:
            o_ref[0, rows, h * D:(h + 1) * D] = _dot1(q[h] * eG[h], S[h]) + _dot1(Aqk[h], vn[h])
        for h in hs:
            s_ref[0, h] = S[h] * jnp.exp(Glast[h]) + _dot1(k[h] * jnp.exp(Glast[h] - Gc[h]), vn[h], _TN)


def gdn_chunked(qkv, beta, g, s0, chunks_per_step=2, heads_per_step=4):
    B, T, W3 = qkv.shape
    D, C = GD_HD, GD_CHUNK
    H = W3 // (3 * D)
    HB = min(heads_per_step, H)
    TB = C * chunks_per_step if T >= C * chunks_per_step else -(-T // C) * C
    Tp = -(-T // TB) * TB
    n_valid = T if Tp != T else TB
    if Tp != T:
        padt = ((0, 0), (0, Tp - T), (0, 0))
        qkv, beta, g = jnp.pad(qkv, padt), jnp.pad(beta, padt), jnp.pad(g, padt)
    ng = H // HB

    def seq(part):
        return pl.BlockSpec((1, TB, HB * D), lambda b, p, c, part=part: (b, c, part * ng + p))

    sc = pl.BlockSpec((1, TB, H), lambda b, p, c: (b, c, 0))
    st = pl.BlockSpec((1, HB, D, D), lambda b, p, c: (b, p, 0, 0))
    o, sT = pl.pallas_call(
        functools.partial(_gdn_body, n_chunks=TB // C, n_valid=n_valid, n_heads=HB),
        grid=(B, ng, Tp // TB),
        in_specs=[seq(0), seq(1), seq(2), sc, sc, st],
        out_specs=[pl.BlockSpec((1, TB, HB * D), lambda b, p, c: (b, c, p)), st],
        out_shape=[jax.ShapeDtypeStruct((B, Tp, H * D), jnp.float32), jax.ShapeDtypeStruct((B, H, D, D), jnp.float32)],
        compiler_params=pltpu.CompilerParams(dimension_semantics=("parallel", "parallel", "arbitrary"),
                                             vmem_limit_bytes=V7X_VMEM_LIMIT_BYTES),
        name="gdn_chunked",
    )(qkv, qkv, qkv, beta, g, s0)
    return o[:, :T], sT


def gated_deltanet_mixer(p, conv0, s0, conv_w, a_log, dt_bias, norm_g):
    f32 = jnp.float32
    B, T, _ = p.shape
    qkv = p[..., :GD_QKV]
    z = p[..., GD_QKV:GD_QKV + GD_W]
    bl = p[..., GD_QKV + GD_W:GD_QKV + GD_W + GD_HEADS]
    al = p[..., GD_QKV + GD_W + GD_HEADS:]
    xc = jnp.concatenate([conv0.astype(p.dtype), qkv], axis=1)
    conv = xc[:, :T] * conv_w[0]
    for i in range(1, GD_CONV):
        conv = conv + xc[:, i:i + T] * conv_w[i]
    conv = jax.nn.silu(conv).astype(f32)
    q = l2_normalize(conv[..., :GD_W].reshape(B, T, GD_HEADS, GD_HD)) * GD_HD ** -0.5
    k = l2_normalize(conv[..., GD_W:2 * GD_W].reshape(B, T, GD_HEADS, GD_HD))
    v = conv[..., 2 * GD_W:].reshape(B, T, GD_HEADS, GD_HD)
    beta = jax.nn.sigmoid(bl.astype(f32))
    g = -jnp.exp(a_log.astype(f32)) * jax.nn.softplus((al + dt_bias).astype(f32))
    qkv_n = jnp.concatenate([q.reshape(B, T, GD_W), k.reshape(B, T, GD_W), conv[..., 2 * GD_W:]], axis=-1)
    o, sT = gdn_chunked(qkv_n, beta, g, s0.astype(f32))
    o = o.reshape(B, T, GD_HEADS, GD_HD)
    o = o * lax.rsqrt(jnp.mean(o * o, -1, keepdims=True) + 1e-6) * norm_g
    o = o.reshape(B, T, GD_W) * jax.nn.silu(z.astype(f32))
    return o.astype(p.dtype), xc[:, -(GD_CONV - 1):], sT


TQ = 128
NEG_BIG = -1e30


def _stack_groups(x):
    return jnp.concatenate([x[:, g * NS_HD:(g + 1) * NS_HD] for g in range(NS_GROUP)], axis=0)


def _nsa_prompt_body(q_ref, ksel_ref, vsel_ref, kwin_ref, vwin_ref, gate_ref, kcmp_ref, vcmp_ref,
                     bsel_ref, bcmp_ref, o_ref, m_ref, l_ref, acc_ref, mask_ref, *, n_tiles):
    f32, bf16 = jnp.float32, jnp.bfloat16
    qi = pl.program_id(2)
    R = NS_GROUP * TQ
    qg = (_stack_groups(q_ref[0]) * NS_HD ** -0.5).astype(bf16)
    row = lax.broadcasted_iota(jnp.int32, (TQ, TQ), 0)
    col = lax.broadcasted_iota(jnp.int32, (TQ, TQ), 1)
    t_q = qi * TQ + row

    def nt_dot(a, b):
        return lax.dot_general(a, b, (((1,), (1,)), ((), ())), preferred_element_type=f32)

    def tile4(x):
        return jnp.concatenate([x] * NS_GROUP, axis=0)

    sc = nt_dot(qg, kcmp_ref[0, 0].astype(bf16))
    sc = sc + jnp.concatenate([bcmp_ref[g, 0] for g in range(NS_GROUP)], axis=0)
    okc = tile4((t_q - (col * CMP_STRIDE + CMP_LEN - 1)) >= 0)
    mc = jnp.max(jnp.where(okc, sc, NEG_BIG), axis=-1, keepdims=True)
    ec = jnp.where(okc, jnp.exp(sc - mc), 0.0)
    pc = ec / jnp.maximum(jnp.sum(ec, axis=-1, keepdims=True), 1e-30)
    o_cmp = jnp.dot(pc.astype(bf16), vcmp_ref[0, 0].astype(bf16), preferred_element_type=f32)

    pc_sum = pc[0:TQ] + pc[TQ:2 * TQ] + pc[2 * TQ:3 * TQ] + pc[3 * TQ:4 * TQ]
    ns = n_tiles * (TQ // SEL_BLOCK)
    nbp = -(-ns // 8) * 8
    jrow = lax.broadcasted_iota(jnp.int32, (nbp, TQ), 0)
    lcol = lax.broadcasted_iota(jnp.int32, (nbp, TQ), 1)
    overlap_t = ((lcol * CMP_STRIDE < (jrow + 1) * SEL_BLOCK) & (lcol * CMP_STRIDE + CMP_LEN > jrow * SEL_BLOCK)).astype(f32)
    ps = lax.dot_general(overlap_t, pc_sum, (((1,), (1,)), ((), ())), preferred_element_type=f32,
                         precision=lax.Precision.HIGHEST)
    cur = (qi * TQ + lcol) // SEL_BLOCK
    valid = (jrow <= cur) & (jrow < ns)
    forced = (jrow == 0) | (jrow == cur) | (jrow == cur - 1)
    score = jnp.where(valid, jnp.where(forced, FORCE_SCORE, ps), -jnp.inf)
    rank = jnp.zeros((nbp, TQ), jnp.int32)
    for i in range(ns):
        ri = score[i:i + 1, :]
        beats = (ri > score) | ((ri == score) & (jrow > i))
        rank = rank + beats.astype(jnp.int32)
    sel_t = (valid & (rank < min(SEL_TOPK, ns))).astype(bf16)
    for kj in range(n_tiles):
        expand = (jrow == (kj * (TQ // SEL_BLOCK) + lcol // SEL_BLOCK)).astype(bf16)
        mask_ref[kj] = lax.dot_general(sel_t, expand, (((0,), (0,)), ((), ())), preferred_element_type=f32)

    def attend(k_ref, v_ref, lo, kind):
        m_ref[...] = jnp.full((R, NS_HD), NEG_BIG, f32)
        l_ref[...] = jnp.zeros((R, NS_HD), f32)
        acc_ref[...] = jnp.zeros((R, NS_HD), f32)

        def step(pj, carry):
            kj = 2 * pj
            off = pl.multiple_of(kj * TQ, 2 * TQ)
            kt = k_ref[0, pl.ds(off, 2 * TQ), :].astype(bf16)
            vt = v_ref[0, pl.ds(off, 2 * TQ), :].astype(bf16)
            delta = qi - kj
            delta1 = jnp.maximum(delta - 1, 0)
            s = nt_dot(qg, kt)
            s = s + jnp.concatenate(
                [jnp.concatenate([bsel_ref[g, delta], bsel_ref[g, delta1]], axis=1) for g in range(NS_GROUP)], axis=0)
            d0 = delta * TQ + row - col
            d = jnp.concatenate([d0, d0 - TQ], axis=1)
            if kind == "sel":
                ok = (d >= 0) & (jnp.concatenate([mask_ref[kj], mask_ref[kj + 1]], axis=1) > 0.5)
            else:
                ok = (d >= 0) & (d <= WINDOW)
            ok = tile4(ok)
            m_old = m_ref[...]
            m_new = jnp.maximum(m_old, jnp.max(jnp.where(ok, s, NEG_BIG), axis=-1, keepdims=True))
            e = jnp.where(ok, jnp.exp(s - jnp.concatenate([m_new, m_new], axis=1)), 0.0)
            scale = jnp.exp(m_old - m_new)
            l_ref[...] = l_ref[...] * scale + jnp.sum(e, axis=-1, keepdims=True)
            acc_ref[...] = acc_ref[...] * scale + jnp.dot(e.astype(bf16), vt, preferred_element_type=f32)
            m_ref[...] = m_new
            return carry

        lax.fori_loop(lo // 2, qi // 2 + 1, step, 0)
        return acc_ref[...] / jnp.maximum(l_ref[...], 1e-30)

    o_sel = attend(ksel_ref, vsel_ref, 0, "sel")
    o_win = attend(kwin_ref, vwin_ref, jnp.maximum(qi - WINDOW // TQ, 0), "win")

    gates = jax.nn.sigmoid(gate_ref[0, 0].astype(f32))
    outs = []
    for g in range(NS_GROUP):
        sl = slice(g * TQ, (g + 1) * TQ)
        og = (gates[:, g:g + 1] * o_cmp[sl] + gates[:, NS_GROUP + g:NS_GROUP + g + 1] * o_sel[sl]
              + gates[:, 2 * NS_GROUP + g:2 * NS_GROUP + g + 1] * o_win[sl])
        outs.append(og)
    o_ref[0] = jnp.concatenate(outs, axis=-1)


def bias_lookup(rel_bias, d):
    onehot = jax.nn.one_hot(rel_bucket(d), REL_BUCKETS, dtype=jnp.float32)
    return jnp.dot(onehot, rel_bias.astype(jnp.float32), precision=lax.Precision.HIGHEST)


def nsa_bias_tiles(rel_bias, n_tiles):
    iq = jnp.arange(TQ)[:, None]
    ik = jnp.arange(TQ)[None, :]
    dl = jnp.arange(n_tiles)[:, None, None]
    bsel = jnp.transpose(bias_lookup(rel_bias, dl * TQ + iq - ik), (3, 0, 1, 2))
    bcmp = jnp.transpose(bias_lookup(rel_bias, dl * TQ + iq - (ik * CMP_STRIDE + CMP_LEN - 1)), (3, 0, 1, 2))
    return bsel, bcmp


def _cmp_prompt_body(k_ref, v_ref, k1_ref, k2_ref, v1_ref, v2_ref, kc_ref, vc_ref):
    f32, bf16 = jnp.float32, jnp.bfloat16
    n_sub = k_ref.shape[1] // CMP_STRIDE
    half = CMP_STRIDE * NS_HD
    for x_ref, w1_ref, w2_ref, o_ref in ((k_ref, k1_ref, k2_ref, kc_ref), (v_ref, v1_ref, v2_ref, vc_ref)):
        a = jnp.zeros((n_sub, CMP_HIDDEN), f32)
        b = jnp.zeros((n_sub, CMP_HIDDEN), f32)
        for p in range(CMP_STRIDE):
            x = x_ref[0, pl.ds(p, n_sub, stride=CMP_STRIDE), :].astype(bf16)
            a = a + jnp.dot(x, w1_ref[p * NS_HD:(p + 1) * NS_HD, :].astype(bf16), preferred_element_type=f32)
            b = b + jnp.dot(x, w1_ref[half + p * NS_HD:half + (p + 1) * NS_HD, :].astype(bf16), preferred_element_type=f32)
        h = gelu_erf(a + jnp.concatenate([b[1:], b[:1]], axis=0))
        o_ref[0, 0] = jnp.dot(h.astype(bf16), w2_ref[...].astype(bf16), preferred_element_type=f32)


def nsa_compress_prompt(pn, phi_k1, phi_k2, phi_v1, phi_v2):
    B, T, _ = pn.shape
    n_sub = T // CMP_STRIDE
    kv0 = NS_W // NS_HD
    w1 = pl.BlockSpec((CMP_LEN * NS_HD, CMP_HIDDEN), lambda b, k: (0, 0))
    w2 = pl.BlockSpec((CMP_HIDDEN, NS_HD), lambda b, k: (0, 0))
    out = pl.BlockSpec((1, 1, n_sub, NS_HD), lambda b, k: (b, k, 0, 0))
    return pl.pallas_call(
        _cmp_prompt_body,
        grid=(B, NS_KV),
        in_specs=[pl.BlockSpec((1, T, NS_HD), lambda b, k: (b, 0, kv0 + k)),
                  pl.BlockSpec((1, T, NS_HD), lambda b, k: (b, 0, kv0 + NS_KV + k)),
                  w1, w2, w1, w2],
        out_specs=[out, out],
        out_shape=[jax.ShapeDtypeStruct((B, NS_KV, n_sub, NS_HD), jnp.float32)] * 2,
        compiler_params=pltpu.CompilerParams(dimension_semantics=("parallel", "parallel"),
                                             vmem_limit_bytes=V7X_VMEM_LIMIT_BYTES),
        name="nsa_compress_prompt",
    )(pn, pn, phi_k1, phi_k2, phi_v1, phi_v2)


def nsa_prompt_attention(pn, kc, vc, bsel, bcmp):
    B, T, _ = pn.shape
    n_tiles = T // TQ
    assert T % (2 * TQ) == 0 and kc.shape[2] == TQ
    glog = pn[..., NS_W + 6 * NS_KVW:].reshape(B, T, 3, NS_KV, NS_GROUP)
    glog = jnp.transpose(glog, (0, 3, 1, 2, 4)).reshape(B, NS_KV, T, 3 * NS_GROUP)
    kv0 = NS_W // NS_HD

    def kv_spec(slot):
        return pl.BlockSpec((1, T, NS_HD), lambda b, k, i, s=slot: (b, 0, kv0 + s * NS_KV + k))

    R = NS_GROUP * TQ
    return pl.pallas_call(
        functools.partial(_nsa_prompt_body, n_tiles=n_tiles),
        grid=(B, NS_KV, n_tiles),
        in_specs=[
            pl.BlockSpec((1, TQ, NS_GROUP * NS_HD), lambda b, k, i: (b, i, k)),
            kv_spec(2), kv_spec(3), kv_spec(4), kv_spec(5),
            pl.BlockSpec((1, 1, TQ, 3 * NS_GROUP), lambda b, k, i: (b, k, i, 0)),
            pl.BlockSpec((1, 1, TQ, NS_HD), lambda b, k, i: (b, k, 0, 0)),
            pl.BlockSpec((1, 1, TQ, NS_HD), lambda b, k, i: (b, k, 0, 0)),
            pl.BlockSpec((NS_GROUP, n_tiles, TQ, TQ), lambda b, k, i: (k, 0, 0, 0)),
            pl.BlockSpec((NS_GROUP, 1, TQ, TQ), lambda b, k, i: (k, i, 0, 0)),
        ],
        out_specs=pl.BlockSpec((1, TQ, NS_GROUP * NS_HD), lambda b, k, i: (b, i, k)),
        out_shape=jax.ShapeDtypeStruct((B, T, NS_W), jnp.float32),
        scratch_shapes=[pltpu.VMEM((R, NS_HD), jnp.float32), pltpu.VMEM((R, NS_HD), jnp.float32),
                        pltpu.VMEM((R, NS_HD), jnp.float32), pltpu.VMEM((n_tiles, TQ, TQ), jnp.float32)],
        compiler_params=pltpu.CompilerParams(
            dimension_semantics=("parallel", "parallel", "arbitrary"),
            vmem_limit_bytes=V7X_VMEM_LIMIT_BYTES),
        name="nsa_prompt_attention",
    )(pn, pn, pn, pn, pn, glog, kc, vc, bsel, bcmp)


def nsa_prompt(p, phi_k1, phi_k2, phi_v1, phi_v2, bsel, bcmp):
    B, T, _ = p.shape
    kv = p[..., NS_W:NS_W + 6 * NS_KVW].reshape(B, T, 6, NS_KV, NS_HD)
    kc, vc = nsa_compress_prompt(p, phi_k1, phi_k2, phi_v1, phi_v2)
    o = nsa_prompt_attention(p, kc, vc, bsel, bcmp)
    wl = min(WINDOW, T)
    return o, kv[:, :, :4], kv[:, T - wl:, 4:]


CMP_PAGE_GROUP = 8
ROW_VECS = 4 * NS_KV
PAGE_ROWS = PAGE_SIZE * ROW_VECS


def _nt_dot(a, b):
    return lax.dot_general(a, b, (((1,), (1,)), ((), ())), preferred_element_type=jnp.float32)


def _cmp_pages_body(pt_ref, page_ref, k1_ref, v1_ref, ab_ref, seq_ref):
    bf16 = jnp.bfloat16
    pg = pl.program_id(1)
    slot_in_group = pg % CMP_PAGE_GROUP
    row0 = pl.multiple_of(slot_in_group * PAGE_ROWS, PAGE_ROWS)
    seq_ref[pl.ds(row0, PAGE_ROWS), :] = page_ref[...]

    @pl.when(slot_in_group == CMP_PAGE_GROUP - 1)
    def _():
        n_sub = CMP_PAGE_GROUP * PAGE_SIZE // CMP_STRIDE
        half = CMP_STRIDE * NS_HD
        for c in range(2 * NS_KV):
            w_ref = k1_ref if c < NS_KV else v1_ref
            a = jnp.zeros((n_sub, CMP_HIDDEN), jnp.float32)
            b = jnp.zeros((n_sub, CMP_HIDDEN), jnp.float32)
            for p in range(CMP_STRIDE):
                x = seq_ref[pl.ds(p * ROW_VECS + c, n_sub, stride=CMP_STRIDE * ROW_VECS), :].astype(bf16)
                a = a + jnp.dot(x, w_ref[p * NS_HD:(p + 1) * NS_HD, :].astype(bf16), preferred_element_type=jnp.float32)
                b = b + jnp.dot(x, w_ref[half + p * NS_HD:half + (p + 1) * NS_HD, :].astype(bf16), preferred_element_type=jnp.float32)
            ab_ref[0, c] = jnp.concatenate([a, b], axis=-1)


def cache_rows(cache_kv):
    return cache_kv.reshape(-1, NS_HD)


def nsa_compress_pages(rows, layer, n_phys, page_table, phi_k1, phi_v1):
    DB, n_pages = page_table.shape
    assert n_pages % CMP_PAGE_GROUP == 0
    subs_per_group = CMP_PAGE_GROUP * PAGE_SIZE // CMP_STRIDE
    n_sub = n_pages * PAGE_SIZE // CMP_STRIDE
    grid_spec = pltpu.PrefetchScalarGridSpec(
        num_scalar_prefetch=1,
        grid=(DB, n_pages),
        in_specs=[pl.BlockSpec((PAGE_ROWS, NS_HD), lambda b, g, pt: (layer * n_phys + pt[b, g], 0)),
                  pl.BlockSpec((CMP_LEN * NS_HD, CMP_HIDDEN), lambda b, g, pt: (0, 0)),
                  pl.BlockSpec((CMP_LEN * NS_HD, CMP_HIDDEN), lambda b, g, pt: (0, 0))],
        out_specs=pl.BlockSpec((1, 2 * NS_KV, subs_per_group, 2 * CMP_HIDDEN),
                               lambda b, g, pt: (b, 0, g // CMP_PAGE_GROUP, 0)),
        scratch_shapes=[pltpu.VMEM((CMP_PAGE_GROUP * PAGE_ROWS, NS_HD), jnp.float32)])
    return pl.pallas_call(
        _cmp_pages_body, grid_spec=grid_spec,
        out_shape=jax.ShapeDtypeStruct((DB, 2 * NS_KV, n_sub, 2 * CMP_HIDDEN), jnp.float32),
        compiler_params=pltpu.CompilerParams(dimension_semantics=("parallel", "arbitrary"),
                                             vmem_limit_bytes=V7X_VMEM_LIMIT_BYTES),
        name="nsa_compress_pages",
    )(page_table, rows, phi_k1, phi_v1)


def _nsa_sample_body(pt_ref, q_ref, page_ref, knew_ref, vnew_ref, win_ref, wnew_ref, gate_ref,
                     ab_ref, k2_ref, v2_ref, bsel_ref, bwin_ref, bcmp_ref, o_ref,
                     m_ref, l_ref, acc_ref, sel_ref, ocmp_ref, *, n_pages, n_new):
    f32, bf16 = jnp.float32, jnp.bfloat16
    pg = pl.program_id(1)
    QG = NS_GROUP * n_new
    R = NS_KV * QG
    n_sub = n_pages * PAGE_SIZE // CMP_STRIDE
    n_cmp = n_sub - 1
    ns = n_pages * (PAGE_SIZE // SEL_BLOCK) + 1
    NSP = sel_ref.shape[1]
    past = n_pages * PAGE_SIZE
    qs = (q_ref[0] * NS_HD ** -0.5).astype(bf16)

    @pl.when(pg == 0)
    def _():
        m_ref[...] = jnp.full((R, NS_HD), NEG_BIG, f32)
        l_ref[...] = jnp.zeros((R, NS_HD), f32)
        acc_ref[...] = jnp.zeros((R, NS_HD), f32)
        ncol = lax.broadcasted_iota(jnp.int32, (QG, n_sub), 1)
        okc = ncol < n_cmp
        orow = lax.broadcasted_iota(jnp.int32, (n_sub, NSP), 0)
        ocol = lax.broadcasted_iota(jnp.int32, (n_sub, NSP), 1)
        overlap = ((orow * CMP_STRIDE < (ocol + 1) * SEL_BLOCK) & (orow * CMP_STRIDE + CMP_LEN > ocol * SEL_BLOCK)
                   & (orow < n_cmp)).astype(f32)
        jcol = lax.broadcasted_iota(jnp.int32, (n_new, NSP), 1)
        qrow = lax.broadcasted_iota(jnp.int32, (n_new, NSP), 0)
        cur = (past + qrow) // SEL_BLOCK
        valid = (jcol <= cur) & (jcol < ns)
        forced = (jcol == 0) | (jcol == cur) | (jcol == cur - 1)
        for kv in range(NS_KV):
            def cmp_of(c, w2_ref):
                a = ab_ref[0, c, :, 0:CMP_HIDDEN]
                b = ab_ref[0, c, :, CMP_HIDDEN:2 * CMP_HIDDEN]
                b = jnp.concatenate([b[1:], b[:1]], axis=0)
                h = gelu_erf(a + b)
                return jnp.dot(h.astype(bf16), w2_ref[...].astype(bf16), preferred_element_type=f32)
            kc = cmp_of(kv, k2_ref)
            vc = cmp_of(NS_KV + kv, v2_ref)
            sc = _nt_dot(qs[kv * QG:(kv + 1) * QG], kc.astype(bf16)) + bcmp_ref[kv * QG:(kv + 1) * QG, :]
            mc = jnp.max(jnp.where(okc, sc, NEG_BIG), axis=-1, keepdims=True)
            ec = jnp.where(okc, jnp.exp(sc - mc), 0.0)
            pc = ec / jnp.maximum(jnp.sum(ec, axis=-1, keepdims=True), 1e-30)
            ocmp_ref[kv * QG:(kv + 1) * QG, :] = jnp.dot(pc.astype(bf16), vc.astype(bf16), preferred_element_type=f32)
            pc_sum = pc[0:n_new]
            for g in range(1, NS_GROUP):
                pc_sum = pc_sum + pc[g * n_new:(g + 1) * n_new]
            ps = jnp.dot(pc_sum, overlap, preferred_element_type=f32, precision=lax.Precision.HIGHEST)
            score = jnp.where(valid, jnp.where(forced, FORCE_SCORE, ps), -jnp.inf)
            rank = jnp.zeros((n_new, NSP), jnp.int32)
            for i in range(ns):
                ci = jnp.broadcast_to(score[:, i:i + 1], (n_new, NSP))
                rank = rank + ((ci > score) | ((ci == score) & (jcol > i))).astype(jnp.int32)
            sel = (valid & (rank < min(SEL_TOPK, ns))).astype(f32)
            sel_ref[kv * QG:(kv + 1) * QG, :] = jnp.concatenate([sel] * NS_GROUP, axis=0)

    is_new = pg == n_pages
    krow = lax.broadcasted_iota(jnp.int32, (NSP, PAGE_SIZE), 0)
    kcol = lax.broadcasted_iota(jnp.int32, (NSP, PAGE_SIZE), 1)
    expand = (krow == pg * (PAGE_SIZE // SEL_BLOCK) + kcol // SEL_BLOCK).astype(bf16)
    inblock = jnp.dot(sel_ref[...].astype(bf16), expand, preferred_element_type=f32) > 0.5
    rr = lax.broadcasted_iota(jnp.int32, (R, PAGE_SIZE), 0)
    cc = lax.broadcasted_iota(jnp.int32, (R, PAGE_SIZE), 1)
    causal_new = (cc <= rr % n_new) & (cc < n_new)
    ok = inblock & (jnp.logical_not(is_new) | causal_new)
    def page_vec(slot, kv, new_ref):
        old = page_ref[pl.ds(slot * NS_KV + kv, PAGE_SIZE, stride=ROW_VECS), :]
        return jnp.where(is_new, new_ref[0, :, kv * NS_HD:(kv + 1) * NS_HD], old).astype(bf16)

    s = jnp.concatenate([_nt_dot(qs[kv * QG:(kv + 1) * QG], page_vec(2, kv, knew_ref))
                         for kv in range(NS_KV)], axis=0) + bsel_ref[...]
    m_old = m_ref[...]
    m_new = jnp.maximum(m_old, jnp.max(jnp.where(ok, s, NEG_BIG), axis=-1, keepdims=True))
    e = jnp.where(ok, jnp.exp(s - m_new), 0.0)
    scale = jnp.exp(m_old - m_new)
    l_ref[...] = l_ref[...] * scale + jnp.sum(e, axis=-1, keepdims=True)
    pv = jnp.concatenate([jnp.dot(e[kv * QG:(kv + 1) * QG].astype(bf16), page_vec(3, kv, vnew_ref),
                                  preferred_element_type=f32) for kv in range(NS_KV)], axis=0)
    acc_ref[...] = acc_ref[...] * scale + pv
    m_ref[...] = m_new

    @pl.when(is_new)
    def _():
        o_sel = acc_ref[...] / jnp.maximum(l_ref[...], 1e-30)
        Wb = win_ref.shape[1]
        wc = lax.broadcasted_iota(jnp.int32, (R, Wb + PAGE_SIZE), 1)
        wr = lax.broadcasted_iota(jnp.int32, (R, Wb + PAGE_SIZE), 0) % n_new
        dw = (Wb + wr) - wc
        okw = (dw >= 0) & (dw <= WINDOW) & ((wc < Wb) | (wc - Wb < n_new))
        sw = []
        for kv in range(NS_KV):
            kw = jnp.concatenate([win_ref[0, :, kv * NS_HD:(kv + 1) * NS_HD],
                                  wnew_ref[0, :, kv * NS_HD:(kv + 1) * NS_HD]], axis=0).astype(bf16)
            sw.append(_nt_dot(qs[kv * QG:(kv + 1) * QG], kw))
        sw = jnp.concatenate(sw, axis=0) + bwin_ref[...]
        mw = jnp.max(jnp.where(okw, sw, NEG_BIG), axis=-1, keepdims=True)
        ew = jnp.where(okw, jnp.exp(sw - mw), 0.0)
        pw = (ew / jnp.maximum(jnp.sum(ew, axis=-1, keepdims=True), 1e-30)).astype(bf16)
        o_win = []
        for kv in range(NS_KV):
            vw = jnp.concatenate([win_ref[0, :, NS_KVW + kv * NS_HD:NS_KVW + (kv + 1) * NS_HD],
                                  wnew_ref[0, :, NS_KVW + kv * NS_HD:NS_KVW + (kv + 1) * NS_HD]], axis=0).astype(bf16)
            o_win.append(jnp.dot(pw[kv * QG:(kv + 1) * QG], vw, preferred_element_type=f32))
        o_win = jnp.concatenate(o_win, axis=0)
        gates = jax.nn.sigmoid(gate_ref[0].astype(f32))
        o = gates[:, 0:1] * ocmp_ref[...] + gates[:, 1:2] * o_sel + gates[:, 2:3] * o_win
        o_ref[0] = jnp.concatenate([o[h * n_new:(h + 1) * n_new] for h in range(NS_HEADS)], axis=-1)


def nsa_sample_bias(rel_bias, n_new, past, wb):
    n_sub = past // CMP_STRIDE
    tq = past + jnp.arange(n_new)[:, None]
    pos = jnp.arange(past + PAGE_SIZE)[None, :]
    R = NS_HEADS * n_new
    bsel = jnp.transpose(bias_lookup(rel_bias, tq - pos), (2, 0, 1)).reshape(R, past + PAGE_SIZE)
    cpos = (jnp.arange(n_sub) * CMP_STRIDE + CMP_LEN - 1)[None, :]
    bcmp = jnp.transpose(bias_lookup(rel_bias, tq - cpos), (2, 0, 1)).reshape(R, n_sub)
    return bsel, bsel[:, past - wb:], bcmp


def nsa_sample_attention(ps, rows, layer, n_phys, page_table, win_buf, ab, phi_k2, phi_v2, bias):
    bsel, bwin, bcmp = bias
    DB, Tn, _ = ps.shape
    n_pages = page_table.shape[1]
    Wb = win_buf.shape[1]
    R = NS_HEADS * Tn
    ns = n_pages * (PAGE_SIZE // SEL_BLOCK) + 1
    NSP = -(-ns // 128) * 128
    n_sub = n_pages * PAGE_SIZE // CMP_STRIDE
    q = jnp.transpose(ps[..., :NS_W].reshape(DB, Tn, NS_HEADS, NS_HD), (0, 2, 1, 3)).reshape(DB, R, NS_HD)
    kvn = ps[..., NS_W:NS_W + 6 * NS_KVW].reshape(DB, Tn, 6, NS_KVW)
    padn = ((0, 0), (0, PAGE_SIZE - Tn), (0, 0))
    knew = jnp.pad(kvn[:, :, 2], padn)
    vnew = jnp.pad(kvn[:, :, 3], padn)
    wnew = jnp.pad(jnp.concatenate([kvn[:, :, 4], kvn[:, :, 5]], axis=-1), padn)
    win2 = win_buf.reshape(DB, Wb, 2 * NS_KVW)
    glog = jnp.transpose(ps[..., NS_W + 6 * NS_KVW:].reshape(DB, Tn, 3, NS_HEADS), (0, 3, 1, 2)).reshape(DB, R, 3)
    last = n_pages - 1
    grid_spec = pltpu.PrefetchScalarGridSpec(
        num_scalar_prefetch=1,
        grid=(DB, n_pages + 1),
        in_specs=[
            pl.BlockSpec((1, R, NS_HD), lambda b, g, pt: (b, 0, 0)),
            pl.BlockSpec((PAGE_ROWS, NS_HD), lambda b, g, pt: (layer * n_phys + pt[b, jnp.minimum(g, last)], 0)),
            pl.BlockSpec((1, PAGE_SIZE, NS_KVW), lambda b, g, pt: (b, 0, 0)),
            pl.BlockSpec((1, PAGE_SIZE, NS_KVW), lambda b, g, pt: (b, 0, 0)),
            pl.BlockSpec((1, Wb, 2 * NS_KVW), lambda b, g, pt: (b, 0, 0)),
            pl.BlockSpec((1, PAGE_SIZE, 2 * NS_KVW), lambda b, g, pt: (b, 0, 0)),
            pl.BlockSpec((1, R, 3), lambda b, g, pt: (b, 0, 0)),
            pl.BlockSpec((1, 2 * NS_KV, n_sub, 2 * CMP_HIDDEN), lambda b, g, pt: (b, 0, 0, 0)),
            pl.BlockSpec((CMP_HIDDEN, NS_HD), lambda b, g, pt: (0, 0)),
            pl.BlockSpec((CMP_HIDDEN, NS_HD), lambda b, g, pt: (0, 0)),
            pl.BlockSpec((R, PAGE_SIZE), lambda b, g, pt: (0, g)),
            pl.BlockSpec((R, Wb + PAGE_SIZE), lambda b, g, pt: (0, 0)),
            pl.BlockSpec((R, n_sub), lambda b, g, pt: (0, 0)),
        ],
        out_specs=pl.BlockSpec((1, Tn, NS_W), lambda b, g, pt: (b, 0, 0)),
        scratch_shapes=[pltpu.VMEM((R, NS_HD), jnp.float32), pltpu.VMEM((R, NS_HD), jnp.float32),
                        pltpu.VMEM((R, NS_HD), jnp.float32), pltpu.VMEM((R, NSP), jnp.float32),
                        pltpu.VMEM((R, NS_HD), jnp.float32)])
    return pl.pallas_call(
        functools.partial(_nsa_sample_body, n_pages=n_pages, n_new=Tn),
        grid_spec=grid_spec,
        out_shape=jax.ShapeDtypeStruct((DB, Tn, NS_W), jnp.float32),
        compiler_params=pltpu.CompilerParams(dimension_semantics=("parallel", "arbitrary"),
                                             vmem_limit_bytes=V7X_VMEM_LIMIT_BYTES),
        name="nsa_sample_attention",
    )(page_table, q, rows, knew, vnew, win2, wnew, glog, ab, phi_k2, phi_v2, bsel, bwin, bcmp)


def nsa_sample(p, rows, layer, n_phys, page_table, win_buf, phi_k1, phi_k2, phi_v1, phi_v2, bias):
    DB, Tn, _ = p.shape
    assert Tn < CMP_STRIDE and Tn <= SEL_BLOCK
    kv = p[..., NS_W:NS_W + 6 * NS_KVW].reshape(DB, Tn, 6, NS_KV, NS_HD)
    ab = nsa_compress_pages(rows, layer, n_phys, page_table, phi_k1, phi_v1)
    o = nsa_sample_attention(p, rows, layer, n_phys, page_table, win_buf, ab, phi_k2, phi_v2, bias)
    win = jnp.concatenate([win_buf, kv[:, :, 4:].astype(win_buf.dtype)], axis=1)
    return o, kv[:, :, :4], win[:, Tn:]


PEER_ROUTE_TM = 128
PEER_ROUTE_UNROLL = 4
PEER_ROUTE_GROUP = 16


def gelu_erf(x):
    return 0.5 * x * (1.0 + lax.erf(x * (2.0 ** -0.5)))


def _top_rows(work, n_rows, k, row_iota):
    vals, idxs = [], []
    for _ in range(k):
        m = jnp.max(work, axis=0, keepdims=True)
        idx = jnp.min(jnp.where(work == m, row_iota, n_rows), axis=0, keepdims=True)
        vals.append(m)
        idxs.append(idx)
        work = jnp.where(row_iota == idx, -jnp.inf, work)
    return vals, idxs


def _peer_route_body(q_ref, k1_ref, k2_ref, g_ref, i1_s, i2_s, w_s, gt_s):
    f32, bf16 = jnp.float32, jnp.bfloat16
    tm = q_ref.shape[0]
    half = PEER_DKEY // 2
    K = PEER_TOPK
    rows = lax.broadcasted_iota(jnp.int32, (PEER_NKEYS, tm), 0)
    crow = lax.broadcasted_iota(jnp.int32, (K * K, tm), 0)
    k1 = k1_ref[...].astype(bf16)
    k2 = k2_ref[...].astype(bf16)

    def nt_dot(a, b):
        return lax.dot_general(a, b, (((1,), (1,)), ((), ())), preferred_element_type=f32)

    for h in range(PEER_HEADS):
        q1 = q_ref[:, h * PEER_DKEY:h * PEER_DKEY + half].astype(bf16)
        q2 = q_ref[:, h * PEER_DKEY + half:(h + 1) * PEER_DKEY].astype(bf16)
        v1, i1 = _top_rows(nt_dot(k1, q1), PEER_NKEYS, K, rows)
        v2, i2 = _top_rows(nt_dot(k2, q2), PEER_NKEYS, K, rows)
        v2m = jnp.concatenate(v2, axis=0)
        i2m = jnp.concatenate(i2, axis=0)
        cand = jnp.concatenate([v1[a] + v2m for a in range(K)], axis=0)
        cidx = jnp.concatenate([i1[a] * PEER_NKEYS + i2m for a in range(K)], axis=0)
        sv, pos = _top_rows(cand, K * K, K, crow)
        eidx = [jnp.max(jnp.where(crow == pos[k], cidx, 0), axis=0, keepdims=True) for k in range(K)]
        svm = jnp.concatenate(sv, axis=0)
        em = jnp.concatenate(eidx, axis=0)
        e = jnp.exp(svm - svm[0:1])
        gw = e / jnp.sum(e, axis=0, keepdims=True)
        i1_s[h * K:(h + 1) * K, :] = (em // PEER_NKEYS).astype(f32)
        i2_s[h * K:(h + 1) * K, :] = (em % PEER_NKEYS).astype(f32)
        w_s[h * K:(h + 1) * K, :] = gw
    i1_s[...] = i1_s[...].T
    i2_s[...] = i2_s[...].T
    w_s[...] = w_s[...].T
    sub = lax.broadcasted_iota(jnp.int32, (PEER_NKEYS, PEER_HEADS * K), 0).astype(f32)

    def token_group(tg, carry):
        base = pl.multiple_of(tg * PEER_ROUTE_GROUP, PEER_ROUTE_GROUP)
        for part in range(PEER_ROUTE_GROUP // PEER_ROUTE_UNROLL):
            us = [part * PEER_ROUTE_UNROLL + u for u in range(PEER_ROUTE_UNROLL)]
            a_w = [jnp.where(sub == i1_s[pl.ds(base + u, 1), :], w_s[pl.ds(base + u, 1), :], 0.0).astype(bf16) for u in us]
            b_1 = [jnp.where(sub == i2_s[pl.ds(base + u, 1), :], 1.0, 0.0).astype(bf16) for u in us]
            g = [nt_dot(a, b) for a, b in zip(a_w, b_1)]
            for u, gt in zip(us, g):
                gt_s[u * PEER_NKEYS:(u + 1) * PEER_NKEYS, :] = gt
        for c in range(PEER_NKEYS):
            g_ref[pl.ds(base, PEER_ROUTE_GROUP), c * PEER_NKEYS:(c + 1) * PEER_NKEYS] = (
                gt_s[pl.ds(c, PEER_ROUTE_GROUP, stride=PEER_NKEYS), :].astype(g_ref.dtype))
        return carry

    lax.fori_loop(0, tm // PEER_ROUTE_GROUP, token_group, 0)


def peer_route(q, k1, k2):
    n = q.shape[0]
    tm = PEER_ROUTE_TM
    S = PEER_HEADS * PEER_TOPK
    assert n % tm == 0 and S == tm
    return pl.pallas_call(
        _peer_route_body,
        grid=(n // tm,),
        in_specs=[pl.BlockSpec((tm, PEER_HEADS * PEER_DKEY), lambda i: (i, 0)),
                  pl.BlockSpec((PEER_NKEYS, PEER_DKEY // 2), lambda i: (0, 0)),
                  pl.BlockSpec((PEER_NKEYS, PEER_DKEY // 2), lambda i: (0, 0))],
        out_specs=pl.BlockSpec((tm, PEER_EXPERTS), lambda i: (i, 0)),
        out_shape=jax.ShapeDtypeStruct((n, PEER_EXPERTS), jnp.bfloat16),
        scratch_shapes=[pltpu.VMEM((S, tm), jnp.float32)] * 3
        + [pltpu.VMEM((PEER_ROUTE_GROUP * PEER_NKEYS, PEER_NKEYS), jnp.float32)],
        compiler_params=pltpu.CompilerParams(dimension_semantics=("parallel",),
                                             vmem_limit_bytes=V7X_VMEM_LIMIT_BYTES),
        name="peer_route",
    )(q, k1, k2)


def _peer_expert_body(x_ref, g_ref, u_ref, v_ref, o_ref):
    f32, bf16 = jnp.float32, jnp.bfloat16
    e = pl.program_id(1)
    h = lax.dot_general(x_ref[...], u_ref[...], (((1,), (1,)), ((), ())), preferred_element_type=f32)
    p = (g_ref[...].astype(f32) * gelu_erf(h)).astype(bf16)
    upd = jnp.dot(p, v_ref[...], preferred_element_type=f32)

    @pl.when(e == 0)
    def _():
        o_ref[...] = upd

    @pl.when(e > 0)
    def _():
        o_ref[...] += upd


def peer_experts(x, g, u, v, layer, tm=512, te=512):
    n, D = x.shape
    E = u.shape[1]
    tm = min(tm, n)
    assert n % tm == 0 and E % te == 0
    return pl.pallas_call(
        _peer_expert_body,
        grid=(n // tm, E // te),
        in_specs=[pl.BlockSpec((tm, D), lambda i, e: (i, 0)),
                  pl.BlockSpec((tm, te), lambda i, e: (i, e)),
                  pl.BlockSpec((None, te, D), lambda i, e: (layer, e, 0)),
                  pl.BlockSpec((None, te, D), lambda i, e: (layer, e, 0))],
        out_specs=pl.BlockSpec((tm, D), lambda i, e: (i, 0)),
        out_shape=jax.ShapeDtypeStruct((n, D), jnp.float32),
        compiler_params=pltpu.CompilerParams(dimension_semantics=("parallel", "arbitrary"),
                                             vmem_limit_bytes=V7X_VMEM_LIMIT_BYTES),
        name="peer_experts",
    )(x, g, u, v)


def peer_ffn(x, layer, wq, k1, k2, u_bf, v_bf):
    Bx, T, D = x.shape
    n = Bx * T
    pad = -n % PEER_ROUTE_TM
    xt = jnp.pad(x.reshape(n, D), ((0, pad), (0, 0)))
    g = peer_route(matmul(xt, wq, layer), k1, k2)
    out = peer_experts(xt.astype(jnp.bfloat16), g, u_bf, v_bf, layer)
    return out[:n].reshape(Bx, T, D).astype(x.dtype)


def residual_block(x, mix, layer, w_out, ln1_g, ln1_b, ln2_g, ln2_b, peer_wq, peer_k1, peer_k2, peer_u, peer_v):
    x = layer_norm(ALPHA * x + matmul3(mix, w_out, layer), ln1_g, ln1_b)
    return layer_norm(ALPHA * x + peer_ffn(x, layer, peer_wq, peer_k1, peer_k2, peer_u, peer_v), ln2_g, ln2_b)


def kernel(x_prompt, x_sample, cache_kv, cache_win, state_rwkv, state_rwkv_shift, state_gdn, state_gdn_conv, page_table, w_in, w_out, ln1_g, ln1_b, ln2_g, ln2_b, rw_mu, rw_w0, rw_w_up, rw_a0, rw_a_up, rw_g_up, rw_k_k, rw_k_a, rw_r_k, rw_ln_g, rw_ln_b, gd_conv_w, gd_a_log, gd_dt_bias, gd_norm_g, ns_phi_k1, ns_phi_k2, ns_phi_v1, ns_phi_v2, rel_bias, peer_wq, peer_k1, peer_k2, peer_u, peer_v):
    xp, xs = x_prompt, x_sample
    B = xp.shape[0]
    o_b = RW_COLS
    o_c = RW_COLS + GD_COLS
    bsel, bcmp = nsa_bias_tiles(rel_bias, SEQ // TQ)
    sbias = nsa_sample_bias(rel_bias, x_sample.shape[1], page_table.shape[1] * PAGE_SIZE, cache_win.shape[2])
    w_in_bf, w_out_bf, wq_bf, u_bf, v_bf = (w.astype(jnp.bfloat16) for w in (w_in, w_out, peer_wq, peer_u, peer_v))
    kv_rows = cache_rows(cache_kv)
    kv_p, kv_s, win_p, win_s, rw_p, rw_s, sh_p, sh_s, gd_p, gd_s, cv_p, cv_s = ([] for _ in range(12))
    for l in range(DEPTH):
        rw = (rw_mu[l], rw_w0[l], rw_w_up[l], rw_a0[l], rw_a_up[l], rw_g_up[l], rw_k_k[l], rw_k_a[l], rw_r_k[l], rw_ln_g[l], rw_ln_b[l])
        gd = (gd_conv_w[l], gd_a_log[l], gd_dt_bias[l], gd_norm_g[l])
        phi = (ns_phi_k1[l], ns_phi_k2[l], ns_phi_v1[l], ns_phi_v2[l])
        tail = (l, w_out_bf, ln1_g[l], ln1_b[l], ln2_g[l], ln2_b[l], wq_bf, peer_k1[l], peer_k2[l], u_bf, v_bf)
        pp = matmul3(xp, w_in_bf, l)
        a, sh, rs = rwkv7_mixer(pp[..., :o_b], jnp.zeros((B, RW_COLS), pp.dtype), jnp.zeros((B, RW_HEADS, RW_HD, RW_HD), jnp.float32), *rw)
        b, cv, gs = gated_deltanet_mixer(pp[..., o_b:o_c], jnp.zeros((B, GD_CONV - 1, GD_QKV), pp.dtype), jnp.zeros((B, GD_HEADS, GD_HD, GD_HD), jnp.float32), *gd)
        c, kvr, wr = nsa_prompt(pp[..., o_c:], *phi, bsel, bcmp)
        xp = residual_block(xp, jnp.concatenate([a, b, c.astype(a.dtype)], -1), *tail)
        kv_p.append(kvr)
        win_p.append(wr)
        rw_p.append(rs)
        sh_p.append(sh)
        gd_p.append(gs)
        cv_p.append(cv)
        ps = matmul3(xs, w_in_bf, l)
        a, sh, rs = rwkv7_mixer(ps[..., :o_b], state_rwkv_shift[l], state_rwkv[l], *rw)
        b, cv, gs = gated_deltanet_mixer(ps[..., o_b:o_c], state_gdn_conv[l], state_gdn[l], *gd)
        c, kvr, wr = nsa_sample(ps[..., o_c:], kv_rows, l, cache_kv.shape[1], page_table, cache_win[l], *phi, sbias)
        xs = residual_block(xs, jnp.concatenate([a, b, c.astype(a.dtype)], -1), *tail)
        kv_s.append(kvr)
        win_s.append(wr)
        rw_s.append(rs)
        sh_s.append(sh)
        gd_s.append(gs)
        cv_s.append(cv)
    st = jnp.stack
    return (xp, xs, st(kv_p), st(kv_s), st(win_p), st(win_s), st(rw_p), st(rw_s), st(sh_p), st(sh_s), st(gd_p), st(gd_s), st(cv_p), st(cv_s))
```

```python
import functools
import math

import jax
import jax.numpy as jnp
from jax import lax
from jax.experimental import pallas as pl
from jax.experimental.pallas import tpu as pltpu

D_MODEL = 4096
BATCH = 4
SEQ = 2048
DEPTH = 4
DEC_BATCH = 8
DEC_SEQ = 8
PAST_LEN = 8192
PAGE_SIZE = 128

ALPHA = (2 * DEPTH) ** 0.25
LN_EPS = 1e-5

RW_HD = 64
RW_W = D_MODEL // 4
RW_HEADS = RW_W // RW_HD
RW_DECAY_R = 64
RW_AAA_R = 64
RW_GATE_R = 160
RW_COLS = 3 * RW_W + RW_DECAY_R + RW_AAA_R + RW_GATE_R
RW_GN_EPS = 64e-5

GD_HD = 128
GD_W = D_MODEL // 4
GD_HEADS = GD_W // GD_HD
GD_QKV = 3 * GD_W
GD_CONV = 4
GD_CHUNK = 64
GD_COLS = GD_QKV + GD_W + 2 * GD_HEADS

NS_HD = 128
NS_W = D_MODEL // 2
NS_HEADS = NS_W // NS_HD
NS_KV = 4
NS_GROUP = NS_HEADS // NS_KV
NS_KVW = NS_KV * NS_HD
NS_COLS = NS_W + 6 * NS_KVW + 3 * NS_HEADS
CMP_LEN = 32
CMP_STRIDE = 16
CMP_HIDDEN = 128
SEL_BLOCK = 64
SEL_TOPK = 16
WINDOW = 512
NS_QBLOCK = 32
FORCE_SCORE = 1e4

REL_BUCKETS = 32
REL_MAX_DIST = 1024

D_MIX = RW_W + GD_W + NS_W
IN_COLS = RW_COLS + GD_COLS + NS_COLS

PEER_HEADS = 8
PEER_NKEYS = 128
PEER_EXPERTS = PEER_NKEYS ** 2
PEER_DKEY = 256
PEER_TOPK = 16
PEER_TBLOCK = 128

V7X_VMEM_LIMIT_BYTES = 56 * 1024 * 1024


def _matmul_body(x_ref, w_ref, o_ref):
    o_ref[...] = jnp.dot(x_ref[...].astype(jnp.bfloat16), w_ref[...].astype(jnp.bfloat16),
                         preferred_element_type=jnp.float32)


def _pick_tile(n, target):
    t = min(n, target)
    while n % t:
        t //= 2
    return t


def matmul(x, w, layer, tm=512, tn=1024):
    M, K = x.shape
    N = w.shape[2]
    tm = _pick_tile(M, tm)
    tn = min(tn, N)
    return pl.pallas_call(
        _matmul_body,
        grid=(M // tm, pl.cdiv(N, tn)),
        in_specs=[pl.BlockSpec((tm, K), lambda i, j: (i, 0)),
                  pl.BlockSpec((None, K, tn), lambda i, j: (layer, 0, j))],
        out_specs=pl.BlockSpec((tm, tn), lambda i, j: (i, j)),
        out_shape=jax.ShapeDtypeStruct((M, N), jnp.float32),
        compiler_params=pltpu.CompilerParams(
            dimension_semantics=("parallel", "parallel"),
            vmem_limit_bytes=V7X_VMEM_LIMIT_BYTES),
        name="proj_matmul",
    )(x, w)


def matmul3(x, w, layer):
    B, T, K = x.shape
    return matmul(x.reshape(B * T, K), w, layer).reshape(B, T, -1)


def layer_norm(x, g, b):
    xf = x.astype(jnp.float32)
    mu = jnp.mean(xf, -1, keepdims=True)
    var = jnp.mean(jnp.square(xf - mu), -1, keepdims=True)
    return ((xf - mu) * lax.rsqrt(var + LN_EPS) * g + b).astype(x.dtype)


def l2_normalize(x):
    return x / jnp.maximum(jnp.sqrt(jnp.sum(x * x, -1, keepdims=True)), 1e-12)


def rel_bucket(d):
    d = jnp.maximum(d, 0)
    exact = REL_BUCKETS // 2
    logd = jnp.log(jnp.maximum(d, 1).astype(jnp.float32) / exact) / math.log(REL_MAX_DIST / exact)
    large = jnp.minimum(exact + (logd * (REL_BUCKETS - exact)).astype(jnp.int32), REL_BUCKETS - 1)
    return jnp.where(d < exact, d, large)


RW_CHUNK = 64


def _split(x):
    hi = x.astype(jnp.bfloat16)
    lo = (x - hi.astype(jnp.float32)).astype(jnp.bfloat16)
    return hi, lo


def _dot3(a, b, dims=(((1,), (0,)), ((), ()))):
    ah, al = _split(a)
    bh, bl = _split(b)
    d = lambda x, y: lax.dot_general(x, y, dims, preferred_element_type=jnp.float32)
    return d(ah, bh) + (d(ah, bl) + d(al, bh))


_NT = (((1,), (1,)), ((), ()))
_TN = (((0,), (0,)), ((), ()))


def _rwkv_chunk_body(r_ref, lw_ref, k_ref, v_ref, kk_ref, ka_ref, s0_ref, y_ref, s_ref, *, n_chunks, n_valid, n_heads):
    f32 = jnp.float32
    N, C = RW_HD, RW_CHUNK
    c_idx = pl.program_id(2)

    @pl.when(c_idx == 0)
    def _():
        s_ref[...] = s0_ref[...]

    row = lax.broadcasted_iota(jnp.int32, (C, C), 0)
    col = lax.broadcasted_iota(jnp.int32, (C, C), 1)
    tril = (row >= col).astype(f32)
    strict = row > col

    hs = range(n_heads)
    for ci in range(n_chunks):
        rows = slice(ci * C, (ci + 1) * C)
        padded = n_valid < n_chunks * C
        live = (lax.broadcasted_iota(jnp.int32, (C, N), 0) + ci * C) < n_valid

        def ld(ref, h):
            x = ref[0, rows, h * N:(h + 1) * N]
            return jnp.where(live, x, 0.0) if padded else x

        lw = [ld(lw_ref, h) for h in hs]
        G = [_dot3(tril, lw[h]) for h in hs]
        eg = [jnp.exp(G[h]) for h in hs]
        ing = [jnp.exp(-G[h]) for h in hs]
        ar = [jnp.concatenate([-ld(kk_ref, h) * jnp.exp(G[h] - lw[h]), ld(r_ref, h) * eg[h]], axis=0) for h in hs]
        bk = [jnp.concatenate([ld(ka_ref, h) * ing[h], ld(k_ref, h) * ing[h]], axis=0) for h in hs]
        v = [ld(v_ref, h) for h in hs]
        S0 = [s_ref[0, h] for h in hs]
        M = [_dot3(ar[h], bk[h], _NT) for h in hs]
        AS = [_dot3(ar[h], S0[h], _NT) for h in hs]
        P = [jnp.where(strict, M[h][0:C, 0:C], 0.0) for h in hs]
        M2 = [jnp.where(strict, M[h][0:C, C:2 * C], 0.0) for h in hs]
        M34 = [jnp.concatenate([M[h][C:2 * C, 0:C] * tril, M[h][C:2 * C, C:2 * C] * tril], axis=1) for h in hs]
        rhs = [AS[h][0:C] + _dot3(M2[h], v[h]) for h in hs]
        X = [rhs[h] + _dot3(P[h], rhs[h]) for h in hs]
        for _ in range(5):
            P = [_dot3(P[h], P[h]) for h in hs]
            X = [X[h] + _dot3(P[h], X[h]) for h in hs]
        sav = [jnp.concatenate([X[h], v[h]], axis=0) for h in hs]
        for h in hs:
            y_ref[0, rows, h * N:(h + 1) * N] = AS[h][C:2 * C] + _dot3(M34[h], sav[h])
        dS = [_dot3(sav[h], bk[h], _TN) for h in hs]
        for h in hs:
            s_ref[0, h] = (S0[h] + dS[h]) * eg[h][C - 1:C, :]


def rwkv_scan_chunked(r, lw, k, v, kk, ka, s0, chunks_per_step=2, heads_per_step=8):
    B, T, W = r.shape
    N, C = RW_HD, RW_CHUNK
    H = W // N
    HB = min(heads_per_step, H)
    TB = C * chunks_per_step if T >= C * chunks_per_step else -(-T // C) * C
    Tp = -(-T // TB) * TB
    n_valid = T if Tp != T else TB

    def prep(x):
        return jnp.pad(x, ((0, 0), (0, Tp - T), (0, 0))) if Tp != T else x

    ins = [prep(x) for x in (r, lw, k, v, kk, ka)]
    seq = pl.BlockSpec((1, TB, HB * N), lambda b, p, c: (b, c, p))
    st = pl.BlockSpec((1, HB, N, N), lambda b, p, c: (b, p, 0, 0))
    y, sT = pl.pallas_call(
        functools.partial(_rwkv_chunk_body, n_chunks=TB // C, n_valid=n_valid, n_heads=HB),
        grid=(B, H // HB, Tp // TB),
        in_specs=[seq] * 6 + [st],
        out_specs=[seq, st],
        out_shape=[jax.ShapeDtypeStruct((B, Tp, W), jnp.float32), jax.ShapeDtypeStruct((B, H, N, N), jnp.float32)],
        compiler_params=pltpu.CompilerParams(dimension_semantics=("parallel", "parallel", "arbitrary"),
                                             vmem_limit_bytes=V7X_VMEM_LIMIT_BYTES),
        name="rwkv_chunked",
    )(*ins, s0)
    return y[:, :T], sT


def rwkv7_mixer(p, shift0, s0, mu, w0, w_up, a0, a_up, g_up, k_k, k_a, r_k, ln_g, ln_b):
    f32 = jnp.float32
    B, T, _ = p.shape
    prev = jnp.concatenate([shift0[:, None].astype(p.dtype), p[:, :-1]], axis=1)
    m = p + mu * (prev - p)
    r = m[..., :RW_W]
    k = m[..., RW_W:2 * RW_W]
    v = m[..., 2 * RW_W:3 * RW_W]
    o = 3 * RW_W
    wl = m[..., o:o + RW_DECAY_R]
    o += RW_DECAY_R
    al = m[..., o:o + RW_AAA_R]
    o += RW_AAA_R
    gl = m[..., o:o + RW_GATE_R]
    w = -jax.nn.softplus(-(w0 + jnp.tanh(wl) @ w_up).astype(f32)) - 0.5
    log_decay = -jnp.exp(w)
    a = jax.nn.sigmoid((a0 + al @ a_up).astype(f32))
    g = jax.nn.sigmoid(gl) @ g_up

    def heads(t):
        return t.reshape(B, T, RW_HEADS, RW_HD).astype(f32)

    kk = l2_normalize(heads(k * k_k)).reshape(B, T, RW_W)
    k = k * (1.0 + (a - 1.0) * k_a)
    r_, k_, v_ = heads(r), heads(k), heads(v)

    y, sT = rwkv_scan_chunked(r, log_decay, k, v, kk, kk * a, s0.astype(f32))
    y = heads(y)
    ym = jnp.mean(y, -1, keepdims=True)
    yv = jnp.mean(jnp.square(y - ym), -1, keepdims=True)
    y = ((y - ym) * lax.rsqrt(yv + RW_GN_EPS)).reshape(B, T, RW_W) * ln_g + ln_b
    bonus = jnp.sum(r_ * k_ * r_k, -1, keepdims=True) * v_
    y = (y + bonus.reshape(B, T, RW_W)) * g
    return y.astype(p.dtype), p[:, -1], sT


def _dot1(a, b, dims=(((1,), (0,)), ((), ()))):
    return lax.dot_general(a.astype(jnp.bfloat16), b.astype(jnp.bfloat16), dims, preferred_element_type=jnp.float32)


def _gdn_body(q_ref, k_ref, v_ref, beta_ref, g_ref, s0_ref, o_ref, s_ref, *, n_chunks, n_valid, n_heads):
    f32 = jnp.float32
    D, C = GD_HD, GD_CHUNK
    hg = pl.program_id(1)
    c_idx = pl.program_id(2)

    @pl.when(c_idx == 0)
    def _():
        s_ref[...] = s0_ref[...]

    row = lax.broadcasted_iota(jnp.int32, (C, C), 0)
    col = lax.broadcasted_iota(jnp.int32, (C, C), 1)
    tri = row >= col
    trif = tri.astype(f32)
    strict = row > col
    eye = row == col
    ones = jnp.ones((C, C), f32)
    lane_h = lax.broadcasted_iota(jnp.int32, (C, beta_ref.shape[2]), 1)
    hs = range(n_heads)
    for ci in range(n_chunks):
        rows = slice(ci * C, (ci + 1) * C)
        padded = n_valid < n_chunks * C
        live = (lax.broadcasted_iota(jnp.int32, (C, 1), 0) + ci * C) < n_valid

        def ld(ref, h):
            x = ref[0, rows, h * D:(h + 1) * D]
            return jnp.where(live, x, 0.0) if padded else x

        g_all = g_ref[0, rows, :]
        b_all = beta_ref[0, rows, :]
        if padded:
            g_all = jnp.where(live, g_all, 0.0)
            b_all = jnp.where(live, b_all, 0.0)
        G_all = _dot3(trif, g_all)

        def colof(x, h):
            return jnp.sum(jnp.where(lane_h == hg * n_heads + h, x, 0.0), axis=1, keepdims=True)

        Gc = [colof(G_all, h) for h in hs]
        bc = [colof(b_all, h) for h in hs]
        GB = [jnp.broadcast_to(Gc[h], (C, C)) for h in hs]
        GR = [_dot3(ones, jnp.where(eye, GB[h], 0.0)) for h in hs]
        decay = [jnp.where(tri, jnp.exp(jnp.where(tri, GB[h] - GR[h], 0.0)), 0.0) for h in hs]
        eG = [jnp.exp(Gc[h]) for h in hs]
        Glast = [Gc[h][C - 1:C, :] for h in hs]
        q = [ld(q_ref, h) for h in hs]
        k = [ld(k_ref, h) for h in hs]
        v = [ld(v_ref, h) for h in hs]
        kb = [k[h] * bc[h] for h in hs]
        P = [-jnp.where(strict, _dot1(kb[h], k[h], _NT) * decay[h], 0.0) for h in hs]
        Aqk = [_dot1(q[h], k[h], _NT) * decay[h] for h in hs]
        rhs = [jnp.concatenate([v[h] * bc[h], kb[h] * eG[h]], axis=1) for h in hs]
        X = [rhs[h] + _dot3(P[h], rhs[h]) for h in hs]
        for _ in range(5):
            P = [_dot3(P[h], P[h]) for h in hs]
            X = [X[h] + _dot3(P[h], X[h]) for h in hs]
        S = [s_ref[0, h] for h in hs]
        vn = [X[h][:, 0:D] - _dot1(X[h][:, D:2 * D], S[h]) for h in hs]
        for h in hs:
            o_ref[0, rows, h * D:(h + 1) * D] = _dot1(q[h] * eG[h], S[h]) + _dot1(Aqk[h], vn[h])
        for h in hs:
            s_ref[0, h] = S[h] * jnp.exp(Glast[h]) + _dot1(k[h] * jnp.exp(Glast[h] - Gc[h]), vn[h], _TN)


def gdn_chunked(qkv, beta, g, s0, chunks_per_step=2, heads_per_step=4):
    B, T, W3 = qkv.shape
    D, C = GD_HD, GD_CHUNK
    H = W3 // (3 * D)
    HB = min(heads_per_step, H)
    TB = C * chunks_per_step if T >= C * chunks_per_step else -(-T // C) * C
    Tp = -(-T // TB) * TB
    n_valid = T if Tp != T else TB
    if Tp != T:
        padt = ((0, 0), (0, Tp - T), (0, 0))
        qkv, beta, g = jnp.pad(qkv, padt), jnp.pad(beta, padt), jnp.pad(g, padt)
    ng = H // HB

    def seq(part):
        return pl.BlockSpec((1, TB, HB * D), lambda b, p, c, part=part: (b, c, part * ng + p))

    sc = pl.BlockSpec((1, TB, H), lambda b, p, c: (b, c, 0))
    st = pl.BlockSpec((1, HB, D, D), lambda b, p, c: (b, p, 0, 0))
    o, sT = pl.pallas_call(
        functools.partial(_gdn_body, n_chunks=TB // C, n_valid=n_valid, n_heads=HB),
        grid=(B, ng, Tp // TB),
        in_specs=[seq(0), seq(1), seq(2), sc, sc, st],
        out_specs=[pl.BlockSpec((1, TB, HB * D), lambda b, p, c: (b, c, p)), st],
        out_shape=[jax.ShapeDtypeStruct((B, Tp, H * D), jnp.float32), jax.ShapeDtypeStruct((B, H, D, D), jnp.float32)],
        compiler_params=pltpu.CompilerParams(dimension_semantics=("parallel", "parallel", "arbitrary"),
                                             vmem_limit_bytes=V7X_VMEM_LIMIT_BYTES),
        name="gdn_chunked",
    )(qkv, qkv, qkv, beta, g, s0)
    return o[:, :T], sT


def gated_deltanet_mixer(p, conv0, s0, conv_w, a_log, dt_bias, norm_g):
    f32 = jnp.float32
    B, T, _ = p.shape
    qkv = p[..., :GD_QKV]
    z = p[..., GD_QKV:GD_QKV + GD_W]
    bl = p[..., GD_QKV + GD_W:GD_QKV + GD_W + GD_HEADS]
    al = p[..., GD_QKV + GD_W + GD_HEADS:]
    xc = jnp.concatenate([conv0.astype(p.dtype), qkv], axis=1)
    conv = xc[:, :T] * conv_w[0]
    for i in range(1, GD_CONV):
        conv = conv + xc[:, i:i + T] * conv_w[i]
    conv = jax.nn.silu(conv).astype(f32)
    q = l2_normalize(conv[..., :GD_W].reshape(B, T, GD_HEADS, GD_HD)) * GD_HD ** -0.5
    k = l2_normalize(conv[..., GD_W:2 * GD_W].reshape(B, T, GD_HEADS, GD_HD))
    v = conv[..., 2 * GD_W:].reshape(B, T, GD_HEADS, GD_HD)
    beta = jax.nn.sigmoid(bl.astype(f32))
    g = -jnp.exp(a_log.astype(f32)) * jax.nn.softplus((al + dt_bias).astype(f32))
    qkv_n = jnp.concatenate([q.reshape(B, T, GD_W), k.reshape(B, T, GD_W), conv[..., 2 * GD_W:]], axis=-1)
    o, sT = gdn_chunked(qkv_n, beta, g, s0.astype(f32))
    o = o.reshape(B, T, GD_HEADS, GD_HD)
    o = o * lax.rsqrt(jnp.mean(o * o, -1, keepdims=True) + 1e-6) * norm_g
    o = o.reshape(B, T, GD_W) * jax.nn.silu(z.astype(f32))
    return o.astype(p.dtype), xc[:, -(GD_CONV - 1):], sT


TQ = 128
NEG_BIG = -1e30


def _stack_groups(x):
    return jnp.concatenate([x[:, g * NS_HD:(g + 1) * NS_HD] for g in range(NS_GROUP)], axis=0)


def _nsa_prompt_body(q_ref, ksel_ref, vsel_ref, kwin_ref, vwin_ref, gate_ref, kcmp_ref, vcmp_ref,
                     bsel_ref, bcmp_ref, o_ref, m_ref, l_ref, acc_ref, mask_ref, *, n_tiles):
    f32, bf16 = jnp.float32, jnp.bfloat16
    qi = pl.program_id(2)
    R = NS_GROUP * TQ
    qg = (_stack_groups(q_ref[0]) * NS_HD ** -0.5).astype(bf16)
    row = lax.broadcasted_iota(jnp.int32, (TQ, TQ), 0)
    col = lax.broadcasted_iota(jnp.int32, (TQ, TQ), 1)
    t_q = qi * TQ + row

    def nt_dot(a, b):
        return lax.dot_general(a, b, (((1,), (1,)), ((), ())), preferred_element_type=f32)

    def tile4(x):
        return jnp.concatenate([x] * NS_GROUP, axis=0)

    sc = nt_dot(qg, kcmp_ref[0, 0].astype(bf16))
    sc = sc + jnp.concatenate([bcmp_ref[g, 0] for g in range(NS_GROUP)], axis=0)
    okc = tile4((t_q - (col * CMP_STRIDE + CMP_LEN - 1)) >= 0)
    mc = jnp.max(jnp.where(okc, sc, NEG_BIG), axis=-1, keepdims=True)
    ec = jnp.where(okc, jnp.exp(sc - mc), 0.0)
    pc = ec / jnp.maximum(jnp.sum(ec, axis=-1, keepdims=True), 1e-30)
    o_cmp = jnp.dot(pc.astype(bf16), vcmp_ref[0, 0].astype(bf16), preferred_element_type=f32)

    pc_sum = pc[0:TQ] + pc[TQ:2 * TQ] + pc[2 * TQ:3 * TQ] + pc[3 * TQ:4 * TQ]
    ns = n_tiles * (TQ // SEL_BLOCK)
    nbp = -(-ns // 8) * 8
    jrow = lax.broadcasted_iota(jnp.int32, (nbp, TQ), 0)
    lcol = lax.broadcasted_iota(jnp.int32, (nbp, TQ), 1)
    overlap_t = ((lcol * CMP_STRIDE < (jrow + 1) * SEL_BLOCK) & (lcol * CMP_STRIDE + CMP_LEN > jrow * SEL_BLOCK)).astype(f32)
    ps = lax.dot_general(overlap_t, pc_sum, (((1,), (1,)), ((), ())), preferred_element_type=f32,
                         precision=lax.Precision.HIGHEST)
    cur = (qi * TQ + lcol) // SEL_BLOCK
    valid = (jrow <= cur) & (jrow < ns)
    forced = (jrow == 0) | (jrow == cur) | (jrow == cur - 1)
    score = jnp.where(valid, jnp.where(forced, FORCE_SCORE, ps), -jnp.inf)
    rank = jnp.zeros((nbp, TQ), jnp.int32)
    for i in range(ns):
        ri = score[i:i + 1, :]
        beats = (ri > score) | ((ri == score) & (jrow > i))
        rank = rank + beats.astype(jnp.int32)
    sel_t = (valid & (rank < min(SEL_TOPK, ns))).astype(bf16)
    for kj in range(n_tiles):
        expand = (jrow == (kj * (TQ // SEL_BLOCK) + lcol // SEL_BLOCK)).astype(bf16)
        mask_ref[kj] = lax.dot_general(sel_t, expand, (((0,), (0,)), ((), ())), preferred_element_type=f32)

    def attend(k_ref, v_ref, lo, kind):
        m_ref[...] = jnp.full((R, NS_HD), NEG_BIG, f32)
        l_ref[...] = jnp.zeros((R, NS_HD), f32)
        acc_ref[...] = jnp.zeros((R, NS_HD), f32)

        def step(pj, carry):
            kj = 2 * pj
            off = pl.multiple_of(kj * TQ, 2 * TQ)
            kt = k_ref[0, pl.ds(off, 2 * TQ), :].astype(bf16)
            vt = v_ref[0, pl.ds(off, 2 * TQ), :].astype(bf16)
            delta = qi - kj
            delta1 = jnp.maximum(delta - 1, 0)
            s = nt_dot(qg, kt)
            s = s + jnp.concatenate(
                [jnp.concatenate([bsel_ref[g, delta], bsel_ref[g, delta1]], axis=1) for g in range(NS_GROUP)], axis=0)
            d0 = delta * TQ + row - col
            d = jnp.concatenate([d0, d0 - TQ], axis=1)
            if kind == "sel":
                ok = (d >= 0) & (jnp.concatenate([mask_ref[kj], mask_ref[kj + 1]], axis=1) > 0.5)
            else:
                ok = (d >= 0) & (d <= WINDOW)
            ok = tile4(ok)
            m_old = m_ref[...]
            m_new = jnp.maximum(m_old, jnp.max(jnp.where(ok, s, NEG_BIG), axis=-1, keepdims=True))
            e = jnp.where(ok, jnp.exp(s - jnp.concatenate([m_new, m_new], axis=1)), 0.0)
            scale = jnp.exp(m_old - m_new)
            l_ref[...] = l_ref[...] * scale + jnp.sum(e, axis=-1, keepdims=True)
            acc_ref[...] = acc_ref[...] * scale + jnp.dot(e.astype(bf16), vt, preferred_element_type=f32)
            m_ref[...] = m_new
            return carry

        lax.fori_loop(lo // 2, qi // 2 + 1, step, 0)
        return acc_ref[...] / jnp.maximum(l_ref[...], 1e-30)

    o_sel = attend(ksel_ref, vsel_ref, 0, "sel")
    o_win = attend(kwin_ref, vwin_ref, jnp.maximum(qi - WINDOW // TQ, 0), "win")

    gates = jax.nn.sigmoid(gate_ref[0, 0].astype(f32))
    outs = []
    for g in range(NS_GROUP):
        sl = slice(g * TQ, (g + 1) * TQ)
        og = (gates[:, g:g + 1] * o_cmp[sl] + gates[:, NS_GROUP + g:NS_GROUP + g + 1] * o_sel[sl]
              + gates[:, 2 * NS_GROUP + g:2 * NS_GROUP + g + 1] * o_win[sl])
        outs.append(og)
    o_ref[0] = jnp.concatenate(outs, axis=-1)


def bias_lookup(rel_bias, d):
    onehot = jax.nn.one_hot(rel_bucket(d), REL_BUCKETS, dtype=jnp.float32)
    return jnp.dot(onehot, rel_bias.astype(jnp.float32), precision=lax.Precision.HIGHEST)


def nsa_bias_tiles(rel_bias, n_tiles):
    iq = jnp.arange(TQ)[:, None]
    ik = jnp.arange(TQ)[None, :]
    dl = jnp.arange(n_tiles)[:, None, None]
    bsel = jnp.transpose(bias_lookup(rel_bias, dl * TQ + iq - ik), (3, 0, 1, 2))
    bcmp = jnp.transpose(bias_lookup(rel_bias, dl * TQ + iq - (ik * CMP_STRIDE + CMP_LEN - 1)), (3, 0, 1, 2))
    return bsel, bcmp


def _cmp_prompt_body(k_ref, v_ref, k1_ref, k2_ref, v1_ref, v2_ref, kc_ref, vc_ref):
    f32, bf16 = jnp.float32, jnp.bfloat16
    n_sub = k_ref.shape[1] // CMP_STRIDE
    half = CMP_STRIDE * NS_HD
    for x_ref, w1_ref, w2_ref, o_ref in ((k_ref, k1_ref, k2_ref, kc_ref), (v_ref, v1_ref, v2_ref, vc_ref)):
        a = jnp.zeros((n_sub, CMP_HIDDEN), f32)
        b = jnp.zeros((n_sub, CMP_HIDDEN), f32)
        for p in range(CMP_STRIDE):
            x = x_ref[0, pl.ds(p, n_sub, stride=CMP_STRIDE), :].astype(bf16)
            a = a + jnp.dot(x, w1_ref[p * NS_HD:(p + 1) * NS_HD, :].astype(bf16), preferred_element_type=f32)
            b = b + jnp.dot(x, w1_ref[half + p * NS_HD:half + (p + 1) * NS_HD, :].astype(bf16), preferred_element_type=f32)
        h = gelu_erf(a + jnp.concatenate([b[1:], b[:1]], axis=0))
        o_ref[0, 0] = jnp.dot(h.astype(bf16), w2_ref[...].astype(bf16), preferred_element_type=f32)


def nsa_compress_prompt(pn, phi_k1, phi_k2, phi_v1, phi_v2):
    B, T, _ = pn.shape
    n_sub = T // CMP_STRIDE
    kv0 = NS_W // NS_HD
    w1 = pl.BlockSpec((CMP_LEN * NS_HD, CMP_HIDDEN), lambda b, k: (0, 0))
    w2 = pl.BlockSpec((CMP_HIDDEN, NS_HD), lambda b, k: (0, 0))
    out = pl.BlockSpec((1, 1, n_sub, NS_HD), lambda b, k: (b, k, 0, 0))
    return pl.pallas_call(
        _cmp_prompt_body,
        grid=(B, NS_KV),
        in_specs=[pl.BlockSpec((1, T, NS_HD), lambda b, k: (b, 0, kv0 + k)),
                  pl.BlockSpec((1, T, NS_HD), lambda b, k: (b, 0, kv0 + NS_KV + k)),
                  w1, w2, w1, w2],
        out_specs=[out, out],
        out_shape=[jax.ShapeDtypeStruct((B, NS_KV, n_sub, NS_HD), jnp.float32)] * 2,
        compiler_params=pltpu.CompilerParams(dimension_semantics=("parallel", "parallel"),
                                             vmem_limit_bytes=V7X_VMEM_LIMIT_BYTES),
        name="nsa_compress_prompt",
    )(pn, pn, phi_k1, phi_k2, phi_v1, phi_v2)


def nsa_prompt_attention(pn, kc, vc, bsel, bcmp):
    B, T, _ = pn.shape
    n_tiles = T // TQ
    assert T % (2 * TQ) == 0 and kc.shape[2] == TQ
    glog = pn[..., NS_W + 6 * NS_KVW:].reshape(B, T, 3, NS_KV, NS_GROUP)
    glog = jnp.transpose(glog, (0, 3, 1, 2, 4)).reshape(B, NS_KV, T, 3 * NS_GROUP)
    kv0 = NS_W // NS_HD

    def kv_spec(slot):
        return pl.BlockSpec((1, T, NS_HD), lambda b, k, i, s=slot: (b, 0, kv0 + s * NS_KV + k))

    R = NS_GROUP * TQ
    return pl.pallas_call(
        functools.partial(_nsa_prompt_body, n_tiles=n_tiles),
        grid=(B, NS_KV, n_tiles),
        in_specs=[
            pl.BlockSpec((1, TQ, NS_GROUP * NS_HD), lambda b, k, i: (b, i, k)),
            kv_spec(2), kv_spec(3), kv_spec(4), kv_spec(5),
            pl.BlockSpec((1, 1, TQ, 3 * NS_GROUP), lambda b, k, i: (b, k, i, 0)),
            pl.BlockSpec((1, 1, TQ, NS_HD), lambda b, k, i: (b, k, 0, 0)),
            pl.BlockSpec((1, 1, TQ, NS_HD), lambda b, k, i: (b, k, 0, 0)),
            pl.BlockSpec((NS_GROUP, n_tiles, TQ, TQ), lambda b, k, i: (k, 0, 0, 0)),
            pl.BlockSpec((NS_GROUP, 1, TQ, TQ), lambda b, k, i: (k, i, 0, 0)),
        ],
        out_specs=pl.BlockSpec((1, TQ, NS_GROUP * NS_HD), lambda b, k, i: (b, i, k)),
        out_shape=jax.ShapeDtypeStruct((B, T, NS_W), jnp.float32),
        scratch_shapes=[pltpu.VMEM((R, NS_HD), jnp.float32), pltpu.VMEM((R, NS_HD), jnp.float32),
                        pltpu.VMEM((R, NS_HD), jnp.float32), pltpu.VMEM((n_tiles, TQ, TQ), jnp.float32)],
        compiler_params=pltpu.CompilerParams(
            dimension_semantics=("parallel", "parallel", "arbitrary"),
            vmem_limit_bytes=V7X_VMEM_LIMIT_BYTES),
        name="nsa_prompt_attention",
    )(pn, pn, pn, pn, pn, glog, kc, vc, bsel, bcmp)


def nsa_prompt(p, phi_k1, phi_k2, phi_v1, phi_v2, bsel, bcmp):
    B, T, _ = p.shape
    kv = p[..., NS_W:NS_W + 6 * NS_KVW].reshape(B, T, 6, NS_KV, NS_HD)
    kc, vc = nsa_compress_prompt(p, phi_k1, phi_k2, phi_v1, phi_v2)
    o = nsa_prompt_attention(p, kc, vc, bsel, bcmp)
    wl = min(WINDOW, T)
    return o, kv[:, :, :4], kv[:, T - wl:, 4:]


CMP_PAGE_GROUP = 8
ROW_VECS = 4 * NS_KV
PAGE_ROWS = PAGE_SIZE * ROW_VECS


def _nt_dot(a, b):
    return lax.dot_general(a, b, (((1,), (1,)), ((), ())), preferred_element_type=jnp.float32)


def _cmp_pages_body(pt_ref, page_ref, k1_ref, v1_ref, ab_ref, seq_ref):
    bf16 = jnp.bfloat16
    pg = pl.program_id(1)
    slot_in_group = pg % CMP_PAGE_GROUP
    row0 = pl.multiple_of(slot_in_group * PAGE_ROWS, PAGE_ROWS)
    seq_ref[pl.ds(row0, PAGE_ROWS), :] = page_ref[...]

    @pl.when(slot_in_group == CMP_PAGE_GROUP - 1)
    def _():
        n_sub = CMP_PAGE_GROUP * PAGE_SIZE // CMP_STRIDE
        half = CMP_STRIDE * NS_HD
        for c in range(2 * NS_KV):
            w_ref = k1_ref if c < NS_KV else v1_ref
            a = jnp.zeros((n_sub, CMP_HIDDEN), jnp.float32)
            b = jnp.zeros((n_sub, CMP_HIDDEN), jnp.float32)
            for p in range(CMP_STRIDE):
                x = seq_ref[pl.ds(p * ROW_VECS + c, n_sub, stride=CMP_STRIDE * ROW_VECS), :].astype(bf16)
                a = a + jnp.dot(x, w_ref[p * NS_HD:(p + 1) * NS_HD, :].astype(bf16), preferred_element_type=jnp.float32)
                b = b + jnp.dot(x, w_ref[half + p * NS_HD:half + (p + 1) * NS_HD, :].astype(bf16), preferred_element_type=jnp.float32)
            ab_ref[0, c] = jnp.concatenate([a, b], axis=-1)


def cache_rows(cache_kv):
    return cache_kv.reshape(-1, NS_HD)


def nsa_compress_pages(rows, layer, n_phys, page_table, phi_k1, phi_v1):
    DB, n_pages = page_table.shape
    assert n_pages % CMP_PAGE_GROUP == 0
    subs_per_group = CMP_PAGE_GROUP * PAGE_SIZE // CMP_STRIDE
    n_sub = n_pages * PAGE_SIZE // CMP_STRIDE
    grid_spec = pltpu.PrefetchScalarGridSpec(
        num_scalar_prefetch=1,
        grid=(DB, n_pages),
        in_specs=[pl.BlockSpec((PAGE_ROWS, NS_HD), lambda b, g, pt: (layer * n_phys + pt[b, g], 0)),
                  pl.BlockSpec((CMP_LEN * NS_HD, CMP_HIDDEN), lambda b, g, pt: (0, 0)),
                  pl.BlockSpec((CMP_LEN * NS_HD, CMP_HIDDEN), lambda b, g, pt: (0, 0))],
        out_specs=pl.BlockSpec((1, 2 * NS_KV, subs_per_group, 2 * CMP_HIDDEN),
                               lambda b, g, pt: (b, 0, g // CMP_PAGE_GROUP, 0)),
        scratch_shapes=[pltpu.VMEM((CMP_PAGE_GROUP * PAGE_ROWS, NS_HD), jnp.float32)])
    return pl.pallas_call(
        _cmp_pages_body, grid_spec=grid_spec,
        out_shape=jax.ShapeDtypeStruct((DB, 2 * NS_KV, n_sub, 2 * CMP_HIDDEN), jnp.float32),
        compiler_params=pltpu.CompilerParams(dimension_semantics=("parallel", "arbitrary"),
                                             vmem_limit_bytes=V7X_VMEM_LIMIT_BYTES),
        name="nsa_compress_pages",
    )(page_table, rows, phi_k1, phi_v1)


def _nsa_sample_body(pt_ref, q_ref, page_ref, knew_ref, vnew_ref, win_ref, wnew_ref, gate_ref,
                     ab_ref, k2_ref, v2_ref, bsel_ref, bwin_ref, bcmp_ref, o_ref,
                     m_ref, l_ref, acc_ref, sel_ref, ocmp_ref, *, n_pages, n_new):
    f32, bf16 = jnp.float32, jnp.bfloat16
    pg = pl.program_id(1)
    QG = NS_GROUP * n_new
    R = NS_KV * QG
    n_sub = n_pages * PAGE_SIZE // CMP_STRIDE
    n_cmp = n_sub - 1
    ns = n_pages * (PAGE_SIZE // SEL_BLOCK) + 1
    NSP = sel_ref.shape[1]
    past = n_pages * PAGE_SIZE
    qs = (q_ref[0] * NS_HD ** -0.5).astype(bf16)

    @pl.when(pg == 0)
    def _():
        m_ref[...] = jnp.full((R, NS_HD), NEG_BIG, f32)
        l_ref[...] = jnp.zeros((R, NS_HD), f32)
        acc_ref[...] = jnp.zeros((R, NS_HD), f32)
        ncol = lax.broadcasted_iota(jnp.int32, (QG, n_sub), 1)
        okc = ncol < n_cmp
        orow = lax.broadcasted_iota(jnp.int32, (n_sub, NSP), 0)
        ocol = lax.broadcasted_iota(jnp.int32, (n_sub, NSP), 1)
        overlap = ((orow * CMP_STRIDE < (ocol + 1) * SEL_BLOCK) & (orow * CMP_STRIDE + CMP_LEN > ocol * SEL_BLOCK)
                   & (orow < n_cmp)).astype(f32)
        jcol = lax.broadcasted_iota(jnp.int32, (n_new, NSP), 1)
        qrow = lax.broadcasted_iota(jnp.int32, (n_new, NSP), 0)
        cur = (past + qrow) // SEL_BLOCK
        valid = (jcol <= cur) & (jcol < ns)
        forced = (jcol == 0) | (jcol == cur) | (jcol == cur - 1)
        for kv in range(NS_KV):
            def cmp_of(c, w2_ref):
                a = ab_ref[0, c, :, 0:CMP_HIDDEN]
                b = ab_ref[0, c, :, CMP_HIDDEN:2 * CMP_HIDDEN]
                b = jnp.concatenate([b[1:], b[:1]], axis=0)
                h = gelu_erf(a + b)
                return jnp.dot(h.astype(bf16), w2_ref[...].astype(bf16), preferred_element_type=f32)
            kc = cmp_of(kv, k2_ref)
            vc = cmp_of(NS_KV + kv, v2_ref)
            sc = _nt_dot(qs[kv * QG:(kv + 1) * QG], kc.astype(bf16)) + bcmp_ref[kv * QG:(kv + 1) * QG, :]
            mc = jnp.max(jnp.where(okc, sc, NEG_BIG), axis=-1, keepdims=True)
            ec = jnp.where(okc, jnp.exp(sc - mc), 0.0)
            pc = ec / jnp.maximum(jnp.sum(ec, axis=-1, keepdims=True), 1e-30)
            ocmp_ref[kv * QG:(kv + 1) * QG, :] = jnp.dot(pc.astype(bf16), vc.astype(bf16), preferred_element_type=f32)
            pc_sum = pc[0:n_new]
            for g in range(1, NS_GROUP):
                pc_sum = pc_sum + pc[g * n_new:(g + 1) * n_new]
            ps = jnp.dot(pc_sum, overlap, preferred_element_type=f32, precision=lax.Precision.HIGHEST)
            score = jnp.where(valid, jnp.where(forced, FORCE_SCORE, ps), -jnp.inf)
            rank = jnp.zeros((n_new, NSP), jnp.int32)
            for i in range(ns):
                ci = jnp.broadcast_to(score[:, i:i + 1], (n_new, NSP))
                rank = rank + ((ci > score) | ((ci == score) & (jcol > i))).astype(jnp.int32)
            sel = (valid & (rank < min(SEL_TOPK, ns))).astype(f32)
            sel_ref[kv * QG:(kv + 1) * QG, :] = jnp.concatenate([sel] * NS_GROUP, axis=0)

    is_new = pg == n_pages
    krow = lax.broadcasted_iota(jnp.int32, (NSP, PAGE_SIZE), 0)
    kcol = lax.broadcasted_iota(jnp.int32, (NSP, PAGE_SIZE), 1)
    expand = (krow == pg * (PAGE_SIZE // SEL_BLOCK) + kcol // SEL_BLOCK).astype(bf16)
    inblock = jnp.dot(sel_ref[...].astype(bf16), expand, preferred_element_type=f32) > 0.5
    rr = lax.broadcasted_iota(jnp.int32, (R, PAGE_SIZE), 0)
    cc = lax.broadcasted_iota(jnp.int32, (R, PAGE_SIZE), 1)
    causal_new = (cc <= rr % n_new) & (cc < n_new)
    ok = inblock & (jnp.logical_not(is_new) | causal_new)
    def page_vec(slot, kv, new_ref):
        old = page_ref[pl.ds(slot * NS_KV + kv, PAGE_SIZE, stride=ROW_VECS), :]
        return jnp.where(is_new, new_ref[0, :, kv * NS_HD:(kv + 1) * NS_HD], old).astype(bf16)

    s = jnp.concatenate([_nt_dot(qs[kv * QG:(kv + 1) * QG], page_vec(2, kv, knew_ref))
                         for kv in range(NS_KV)], axis=0) + bsel_ref[...]
    m_old = m_ref[...]
    m_new = jnp.maximum(m_old, jnp.max(jnp.where(ok, s, NEG_BIG), axis=-1, keepdims=True))
    e = jnp.where(ok, jnp.exp(s - m_new), 0.0)
    scale = jnp.exp(m_old - m_new)
    l_ref[...] = l_ref[...] * scale + jnp.sum(e, axis=-1, keepdims=True)
    pv = jnp.concatenate([jnp.dot(e[kv * QG:(kv + 1) * QG].astype(bf16), page_vec(3, kv, vnew_ref),
                                  preferred_element_type=f32) for kv in range(NS_KV)], axis=0)
    acc_ref[...] = acc_ref[...] * scale + pv
    m_ref[...] = m_new

    @pl.when(is_new)
    def _():
        o_sel = acc_ref[...] / jnp.maximum(l_ref[...], 1e-30)
        Wb = win_ref.shape[1]
        wc = lax.broadcasted_iota(jnp.int32, (R, Wb + PAGE_SIZE), 1)
        wr = lax.broadcasted_iota(jnp.int32, (R, Wb + PAGE_SIZE), 0) % n_new
        dw = (Wb + wr) - wc
        okw = (dw >= 0) & (dw <= WINDOW) & ((wc < Wb) | (wc - Wb < n_new))
        sw = []
        for kv in range(NS_KV):
            kw = jnp.concatenate([win_ref[0, :, kv * NS_HD:(kv + 1) * NS_HD],
                                  wnew_ref[0, :, kv * NS_HD:(kv + 1) * NS_HD]], axis=0).astype(bf16)
            sw.append(_nt_dot(qs[kv * QG:(kv + 1) * QG], kw))
        sw = jnp.concatenate(sw, axis=0) + bwin_ref[...]
        mw = jnp.max(jnp.where(okw, sw, NEG_BIG), axis=-1, keepdims=True)
        ew = jnp.where(okw, jnp.exp(sw - mw), 0.0)
        pw = (ew / jnp.maximum(jnp.sum(ew, axis=-1, keepdims=True), 1e-30)).astype(bf16)
        o_win = []
        for kv in range(NS_KV):
            vw = jnp.concatenate([win_ref[0, :, NS_KVW + kv * NS_HD:NS_KVW + (kv + 1) * NS_HD],
                                  wnew_ref[0, :, NS_KVW + kv * NS_HD:NS_KVW + (kv + 1) * NS_HD]], axis=0).astype(bf16)
            o_win.append(jnp.dot(pw[kv * QG:(kv + 1) * QG], vw, preferred_element_type=f32))
        o_win = jnp.concatenate(o_win, axis=0)
        gates = jax.nn.sigmoid(gate_ref[0].astype(f32))
        o = gates[:, 0:1] * ocmp_ref[...] + gates[:, 1:2] * o_sel + gates[:, 2:3] * o_win
        o_ref[0] = jnp.concatenate([o[h * n_new:(h + 1) * n_new] for h in range(NS_HEADS)], axis=-1)


def nsa_sample_bias(rel_bias, n_new, past, wb):
    n_sub = past // CMP_STRIDE
    tq = past + jnp.arange(n_new)[:, None]
    pos = jnp.arange(past + PAGE_SIZE)[None, :]
    R = NS_HEADS * n_new
    bsel = jnp.transpose(bias_lookup(rel_bias, tq - pos), (2, 0, 1)).reshape(R, past + PAGE_SIZE)
    cpos = (jnp.arange(n_sub) * CMP_STRIDE + CMP_LEN - 1)[None, :]
    bcmp = jnp.transpose(bias_lookup(rel_bias, tq - cpos), (2, 0, 1)).reshape(R, n_sub)
    return bsel, bsel[:, past - wb:], bcmp


def nsa_sample_attention(ps, rows, layer, n_phys, page_table, win_buf, ab, phi_k2, phi_v2, bias):
    bsel, bwin, bcmp = bias
    DB, Tn, _ = ps.shape
    n_pages = page_table.shape[1]
    Wb = win_buf.shape[1]
    R = NS_HEADS * Tn
    ns = n_pages * (PAGE_SIZE // SEL_BLOCK) + 1
    NSP = -(-ns // 128) * 128
    n_sub = n_pages * PAGE_SIZE // CMP_STRIDE
    q = jnp.transpose(ps[..., :NS_W].reshape(DB, Tn, NS_HEADS, NS_HD), (0, 2, 1, 3)).reshape(DB, R, NS_HD)
    kvn = ps[..., NS_W:NS_W + 6 * NS_KVW].reshape(DB, Tn, 6, NS_KVW)
    padn = ((0, 0), (0, PAGE_SIZE - Tn), (0, 0))
    knew = jnp.pad(kvn[:, :, 2], padn)
    vnew = jnp.pad(kvn[:, :, 3], padn)
    wnew = jnp.pad(jnp.concatenate([kvn[:, :, 4], kvn[:, :, 5]], axis=-1), padn)
    win2 = win_buf.reshape(DB, Wb, 2 * NS_KVW)
    glog = jnp.transpose(ps[..., NS_W + 6 * NS_KVW:].reshape(DB, Tn, 3, NS_HEADS), (0, 3, 1, 2)).reshape(DB, R, 3)
    last = n_pages - 1
    grid_spec = pltpu.PrefetchScalarGridSpec(
        num_scalar_prefetch=1,
        grid=(DB, n_pages + 1),
        in_specs=[
            pl.BlockSpec((1, R, NS_HD), lambda b, g, pt: (b, 0, 0)),
            pl.BlockSpec((PAGE_ROWS, NS_HD), lambda b, g, pt: (layer * n_phys + pt[b, jnp.minimum(g, last)], 0)),
            pl.BlockSpec((1, PAGE_SIZE, NS_KVW), lambda b, g, pt: (b, 0, 0)),
            pl.BlockSpec((1, PAGE_SIZE, NS_KVW), lambda b, g, pt: (b, 0, 0)),
            pl.BlockSpec((1, Wb, 2 * NS_KVW), lambda b, g, pt: (b, 0, 0)),
            pl.BlockSpec((1, PAGE_SIZE, 2 * NS_KVW), lambda b, g, pt: (b, 0, 0)),
            pl.BlockSpec((1, R, 3), lambda b, g, pt: (b, 0, 0)),
            pl.BlockSpec((1, 2 * NS_KV, n_sub, 2 * CMP_HIDDEN), lambda b, g, pt: (b, 0, 0, 0)),
            pl.BlockSpec((CMP_HIDDEN, NS_HD), lambda b, g, pt: (0, 0)),
            pl.BlockSpec((CMP_HIDDEN, NS_HD), lambda b, g, pt: (0, 0)),
            pl.BlockSpec((R, PAGE_SIZE), lambda b, g, pt: (0, g)),
            pl.BlockSpec((R, Wb + PAGE_SIZE), lambda b, g, pt: (0, 0)),
            pl.BlockSpec((R, n_sub), lambda b, g, pt: (0, 0)),
        ],
        out_specs=pl.BlockSpec((1, Tn, NS_W), lambda b, g, pt: (b, 0, 0)),
        scratch_shapes=[pltpu.VMEM((R, NS_HD), jnp.float32), pltpu.VMEM((R, NS_HD), jnp.float32),
                        pltpu.VMEM((R, NS_HD), jnp.float32), pltpu.VMEM((R, NSP), jnp.float32),
                        pltpu.VMEM((R, NS_HD), jnp.float32)])
    return pl.pallas_call(
        functools.partial(_nsa_sample_body, n_pages=n_pages, n_new=Tn),
        grid_spec=grid_spec,
        out_shape=jax.ShapeDtypeStruct((DB, Tn, NS_W), jnp.float32),
        compiler_params=pltpu.CompilerParams(dimension_semantics=("parallel", "arbitrary"),
                                             vmem_limit_bytes=V7X_VMEM_LIMIT_BYTES),
        name="nsa_sample_attention",
    )(page_table, q, rows, knew, vnew, win2, wnew, glog, ab, phi_k2, phi_v2, bsel, bwin, bcmp)


def nsa_sample(p, rows, layer, n_phys, page_table, win_buf, phi_k1, phi_k2, phi_v1, phi_v2, bias):
    DB, Tn, _ = p.shape
    assert Tn < CMP_STRIDE and Tn <= SEL_BLOCK
    kv = p[..., NS_W:NS_W + 6 * NS_KVW].reshape(DB, Tn, 6, NS_KV, NS_HD)
    ab = nsa_compress_pages(rows, layer, n_phys, page_table, phi_k1, phi_v1)
    o = nsa_sample_attention(p, rows, layer, n_phys, page_table, win_buf, ab, phi_k2, phi_v2, bias)
    win = jnp.concatenate([win_buf, kv[:, :, 4:].astype(win_buf.dtype)], axis=1)
    return o, kv[:, :, :4], win[:, Tn:]


PEER_ROUTE_TM = 128
PEER_ROUTE_UNROLL = 4
PEER_ROUTE_GROUP = 16


def gelu_erf(x):
    return 0.5 * x * (1.0 + lax.erf(x * (2.0 ** -0.5)))


def _top_rows(work, n_rows, k, row_iota):
    vals, idxs = [], []
    for _ in range(k):
        m = jnp.max(work, axis=0, keepdims=True)
        idx = jnp.min(jnp.where(work == m, row_iota, n_rows), axis=0, keepdims=True)
        vals.append(m)
        idxs.append(idx)
        work = jnp.where(row_iota == idx, -jnp.inf, work)
    return vals, idxs


def _peer_route_body(q_ref, k1_ref, k2_ref, g_ref, i1_s, i2_s, w_s, gt_s):
    f32, bf16 = jnp.float32, jnp.bfloat16
    tm = q_ref.shape[0]
    half = PEER_DKEY // 2
    K = PEER_TOPK
    rows = lax.broadcasted_iota(jnp.int32, (PEER_NKEYS, tm), 0)
    n_cand = K + (K // 2 - 1) * (K // 2) + K // 2
    crow = lax.broadcasted_iota(jnp.int32, (n_cand, tm), 0)
    k1 = k1_ref[...].astype(bf16)
    k2 = k2_ref[...].astype(bf16)

    def nt_dot(a, b):
        return lax.dot_general(a, b, (((1,), (1,)), ((), ())), preferred_element_type=f32)

    for h in range(PEER_HEADS):
        q1 = q_ref[:, h * PEER_DKEY:h * PEER_DKEY + half].astype(bf16)
        q2 = q_ref[:, h * PEER_DKEY + half:(h + 1) * PEER_DKEY].astype(bf16)
        v1, i1 = _top_rows(nt_dot(k1, q1), PEER_NKEYS, K, rows)
        v2, i2 = _top_rows(nt_dot(k2, q2), PEER_NKEYS, K, rows)
        v2m = jnp.concatenate(v2, axis=0)
        i2m = jnp.concatenate(i2, axis=0)
        v1m = jnp.concatenate(v1, axis=0)
        i1m = jnp.concatenate(i1, axis=0)
        hk = K // 2
        cand = jnp.concatenate([v1[0] + v2m] + [v1[a] + v2m[0:hk] for a in range(1, hk)] + [v1m[hk:K] + v2[0]], axis=0)
        cidx = jnp.concatenate([i1[0] * PEER_NKEYS + i2m] + [i1[a] * PEER_NKEYS + i2m[0:hk] for a in range(1, hk)]
                               + [i1m[hk:K] * PEER_NKEYS + i2[0]], axis=0)
        sv, pos = _top_rows(cand, n_cand, K, crow)
        eidx = [jnp.max(jnp.where(crow == pos[k], cidx, 0), axis=0, keepdims=True) for k in range(K)]
        svm = jnp.concatenate(sv, axis=0)
        em = jnp.concatenate(eidx, axis=0)
        e = jnp.exp(svm - svm[0:1])
        gw = e / jnp.sum(e, axis=0, keepdims=True)
        i1_s[h * K:(h + 1) * K, :] = (em // PEER_NKEYS).astype(f32)
        i2_s[h * K:(h + 1) * K, :] = (em % PEER_NKEYS).astype(f32)
        w_s[h * K:(h + 1) * K, :] = gw
    i1_s[...] = i1_s[...].T
    i2_s[...] = i2_s[...].T
    w_s[...] = w_s[...].T
    sub = lax.broadcasted_iota(jnp.int32, (PEER_NKEYS, PEER_HEADS * K), 0).astype(f32)

    def token_group(tg, carry):
        base = pl.multiple_of(tg * PEER_ROUTE_GROUP, PEER_ROUTE_GROUP)
        for part in range(PEER_ROUTE_GROUP // PEER_ROUTE_UNROLL):
            us = [part * PEER_ROUTE_UNROLL + u for u in range(PEER_ROUTE_UNROLL)]
            a_w = [jnp.where(sub == i1_s[pl.ds(base + u, 1), :], w_s[pl.ds(base + u, 1), :], 0.0).astype(bf16) for u in us]
            b_1 = [jnp.where(sub == i2_s[pl.ds(base + u, 1), :], 1.0, 0.0).astype(bf16) for u in us]
            g = [nt_dot(a, b) for a, b in zip(a_w, b_1)]
            for u, gt in zip(us, g):
                gt_s[u * PEER_NKEYS:(u + 1) * PEER_NKEYS, :] = gt
        for c in range(PEER_NKEYS):
            g_ref[pl.ds(base, PEER_ROUTE_GROUP), c * PEER_NKEYS:(c + 1) * PEER_NKEYS] = (
                gt_s[pl.ds(c, PEER_ROUTE_GROUP, stride=PEER_NKEYS), :].astype(g_ref.dtype))
        return carry

    lax.fori_loop(0, tm // PEER_ROUTE_GROUP, token_group, 0)


def peer_route(q, k1, k2):
    n = q.shape[0]
    tm = PEER_ROUTE_TM
    S = PEER_HEADS * PEER_TOPK
    assert n % tm == 0 and S == tm
    return pl.pallas_call(
        _peer_route_body,
        grid=(n // tm,),
        in_specs=[pl.BlockSpec((tm, PEER_HEADS * PEER_DKEY), lambda i: (i, 0)),
                  pl.BlockSpec((PEER_NKEYS, PEER_DKEY // 2), lambda i: (0, 0)),
                  pl.BlockSpec((PEER_NKEYS, PEER_DKEY // 2), lambda i: (0, 0))],
        out_specs=pl.BlockSpec((tm, PEER_EXPERTS), lambda i: (i, 0)),
        out_shape=jax.ShapeDtypeStruct((n, PEER_EXPERTS), jnp.bfloat16),
        scratch_shapes=[pltpu.VMEM((S, tm), jnp.float32)] * 3
        + [pltpu.VMEM((PEER_ROUTE_GROUP * PEER_NKEYS, PEER_NKEYS), jnp.float32)],
        compiler_params=pltpu.CompilerParams(dimension_semantics=("parallel",),
                                             vmem_limit_bytes=V7X_VMEM_LIMIT_BYTES),
        name="peer_route",
    )(q, k1, k2)


def _peer_expert_body(x_ref, g_ref, u_ref, v_ref, o_ref):
    f32, bf16 = jnp.float32, jnp.bfloat16
    e = pl.program_id(1)
    h = lax.dot_general(x_ref[...], u_ref[...], (((1,), (1,)), ((), ())), preferred_element_type=f32)
    p = (g_ref[...].astype(f32) * gelu_erf(h)).astype(bf16)
    upd = jnp.dot(p, v_ref[...], preferred_element_type=f32)

    @pl.when(e == 0)
    def _():
        o_ref[...] = upd

    @pl.when(e > 0)
    def _():
        o_ref[...] += upd


def peer_experts(x, g, u, v, layer, tm=512, te=512):
    n, D = x.shape
    E = u.shape[1]
    tm = min(tm, n)
    assert n % tm == 0 and E % te == 0
    return pl.pallas_call(
        _peer_expert_body,
        grid=(n // tm, E // te),
        in_specs=[pl.BlockSpec((tm, D), lambda i, e: (i, 0)),
                  pl.BlockSpec((tm, te), lambda i, e: (i, e)),
                  pl.BlockSpec((None, te, D), lambda i, e: (layer, e, 0)),
                  pl.BlockSpec((None, te, D), lambda i, e: (layer, e, 0))],
        out_specs=pl.BlockSpec((tm, D), lambda i, e: (i, 0)),
        out_shape=jax.ShapeDtypeStruct((n, D), jnp.float32),
        compiler_params=pltpu.CompilerParams(dimension_semantics=("parallel", "arbitrary"),
                                             vmem_limit_bytes=V7X_VMEM_LIMIT_BYTES),
        name="peer_experts",
    )(x, g, u, v)


def peer_ffn(x, layer, wq, k1, k2, u_bf, v_bf):
    Bx, T, D = x.shape
    n = Bx * T
    pad = -n % PEER_ROUTE_TM
    xt = jnp.pad(x.reshape(n, D), ((0, pad), (0, 0)))
    g = peer_route(matmul(xt, wq, layer), k1, k2)
    out = peer_experts(xt.astype(jnp.bfloat16), g, u_bf, v_bf, layer)
    return out[:n].reshape(Bx, T, D).astype(x.dtype)


def residual_block(x, mix, layer, w_out, ln1_g, ln1_b, ln2_g, ln2_b, peer_wq, peer_k1, peer_k2, peer_u, peer_v):
    x = layer_norm(ALPHA * x + matmul3(mix, w_out, layer), ln1_g, ln1_b)
    return layer_norm(ALPHA * x + peer_ffn(x, layer, peer_wq, peer_k1, peer_k2, peer_u, peer_v), ln2_g, ln2_b)


def kernel(x_prompt, x_sample, cache_kv, cache_win, state_rwkv, state_rwkv_shift, state_gdn, state_gdn_conv, page_table, w_in, w_out, ln1_g, ln1_b, ln2_g, ln2_b, rw_mu, rw_w0, rw_w_up, rw_a0, rw_a_up, rw_g_up, rw_k_k, rw_k_a, rw_r_k, rw_ln_g, rw_ln_b, gd_conv_w, gd_a_log, gd_dt_bias, gd_norm_g, ns_phi_k1, ns_phi_k2, ns_phi_v1, ns_phi_v2, rel_bias, peer_wq, peer_k1, peer_k2, peer_u, peer_v):
    xp, xs = x_prompt, x_sample
    B = xp.shape[0]
    o_b = RW_COLS
    o_c = RW_COLS + GD_COLS
    bsel, bcmp = nsa_bias_tiles(rel_bias, SEQ // TQ)
    sbias = nsa_sample_bias(rel_bias, x_sample.shape[1], page_table.shape[1] * PAGE_SIZE, cache_win.shape[2])
    w_in_bf, w_out_bf, wq_bf, u_bf, v_bf = (w.astype(jnp.bfloat16) for w in (w_in, w_out, peer_wq, peer_u, peer_v))
    kv_rows = cache_rows(cache_kv)
    kv_p, kv_s, win_p, win_s, rw_p, rw_s, sh_p, sh_s, gd_p, gd_s, cv_p, cv_s = ([] for _ in range(12))
    for l in range(DEPTH):
        rw = (rw_mu[l], rw_w0[l], rw_w_up[l], rw_a0[l], rw_a_up[l], rw_g_up[l], rw_k_k[l], rw_k_a[l], rw_r_k[l], rw_ln_g[l], rw_ln_b[l])
        gd = (gd_conv_w[l], gd_a_log[l], gd_dt_bias[l], gd_norm_g[l])
        phi = (ns_phi_k1[l], ns_phi_k2[l], ns_phi_v1[l], ns_phi_v2[l])
        tail = (l, w_out_bf, ln1_g[l], ln1_b[l], ln2_g[l], ln2_b[l], wq_bf, peer_k1[l], peer_k2[l], u_bf, v_bf)
        pp = matmul3(xp, w_in_bf, l)
        a, sh, rs = rwkv7_mixer(pp[..., :o_b], jnp.zeros((B, RW_COLS), pp.dtype), jnp.zeros((B, RW_HEADS, RW_HD, RW_HD), jnp.float32), *rw)
        b, cv, gs = gated_deltanet_mixer(pp[..., o_b:o_c], jnp.zeros((B, GD_CONV - 1, GD_QKV), pp.dtype), jnp.zeros((B, GD_HEADS, GD_HD, GD_HD), jnp.float32), *gd)
        c, kvr, wr = nsa_prompt(pp[..., o_c:], *phi, bsel, bcmp)
        xp = residual_block(xp, jnp.concatenate([a, b, c.astype(a.dtype)], -1), *tail)
        kv_p.append(kvr)
        win_p.append(wr)
        rw_p.append(rs)
        sh_p.append(sh)
        gd_p.append(gs)
        cv_p.append(cv)
        ps = matmul3(xs, w_in_bf, l)
        a, sh, rs = rwkv7_mixer(ps[..., :o_b], state_rwkv_shift[l], state_rwkv[l], *rw)
        b, cv, gs = gated_deltanet_mixer(ps[..., o_b:o_c], state_gdn_conv[l], state_gdn[l], *gd)
        c, kvr, wr = nsa_sample(ps[..., o_c:], kv_rows, l, cache_kv.shape[1], page_table, cache_win[l], *phi, sbias)
        xs = residual_block(xs, jnp.concatenate([a, b, c.astype(a.dtype)], -1), *tail)
        kv_s.append(kvr)
        win_s.append(wr)
        rw_s.append(rs)
        sh_s.append(sh)
        gd_s.append(gs)
        cv_s.append(cv)
    st = jnp.stack
    return (xp, xs, st(kv_p), st(kv_s), st(win_p), st(win_s), st(rw_p), st(rw_s), st(sh_p), st(sh_s), st(gd_p), st(gd_s), st(cv_p), st(cv_s))
```

```python
import functools
import math

import jax
import jax.numpy as jnp
from jax import lax
from jax.experimental import pallas as pl
from jax.experimental.pallas import tpu as pltpu

D_MODEL = 4096
BATCH = 4
SEQ = 2048
DEPTH = 4
DEC_BATCH = 8
DEC_SEQ = 8
PAST_LEN = 8192
PAGE_SIZE = 128

ALPHA = (2 * DEPTH) ** 0.25
LN_EPS = 1e-5

RW_HD = 64
RW_W = D_MODEL // 4
RW_HEADS = RW_W // RW_HD
RW_DECAY_R = 64
RW_AAA_R = 64
RW_GATE_R = 160
RW_COLS = 3 * RW_W + RW_DECAY_R + RW_AAA_R + RW_GATE_R
RW_GN_EPS = 64e-5

GD_HD = 128
GD_W = D_MODEL // 4
GD_HEADS = GD_W // GD_HD
GD_QKV = 3 * GD_W
GD_CONV = 4
GD_CHUNK = 64
GD_COLS = GD_QKV + GD_W + 2 * GD_HEADS

NS_HD = 128
NS_W = D_MODEL // 2
NS_HEADS = NS_W // NS_HD
NS_KV = 4
NS_GROUP = NS_HEADS // NS_KV
NS_KVW = NS_KV * NS_HD
NS_COLS = NS_W + 6 * NS_KVW + 3 * NS_HEADS
CMP_LEN = 32
CMP_STRIDE = 16
CMP_HIDDEN = 128
SEL_BLOCK = 64
SEL_TOPK = 16
WINDOW = 512
NS_QBLOCK = 32
FORCE_SCORE = 1e4

REL_BUCKETS = 32
REL_MAX_DIST = 1024

D_MIX = RW_W + GD_W + NS_W
IN_COLS = RW_COLS + GD_COLS + NS_COLS

PEER_HEADS = 8
PEER_NKEYS = 128
PEER_EXPERTS = PEER_NKEYS ** 2
PEER_DKEY = 256
PEER_TOPK = 16
PEER_TBLOCK = 128

V7X_VMEM_LIMIT_BYTES = 56 * 1024 * 1024


def _matmul_body(x_ref, w_ref, o_ref):
    o_ref[...] = jnp.dot(x_ref[...].astype(jnp.bfloat16), w_ref[...].astype(jnp.bfloat16),
                         preferred_element_type=jnp.float32)


def _pick_tile(n, target):
    t = min(n, target)
    while n % t:
        t //= 2
    return t


def matmul(x, w, layer, tm=512, tn=1024):
    M, K = x.shape
    N = w.shape[2]
    tm = _pick_tile(M, tm)
    tn = min(tn, N)
    return pl.pallas_call(
        _matmul_body,
        grid=(M // tm, pl.cdiv(N, tn)),
        in_specs=[pl.BlockSpec((tm, K), lambda i, j: (i, 0)),
                  pl.BlockSpec((None, K, tn), lambda i, j: (layer, 0, j))],
        out_specs=pl.BlockSpec((tm, tn), lambda i, j: (i, j)),
        out_shape=jax.ShapeDtypeStruct((M, N), jnp.float32),
        compiler_params=pltpu.CompilerParams(
            dimension_semantics=("parallel", "parallel"),
            vmem_limit_bytes=V7X_VMEM_LIMIT_BYTES),
        name="proj_matmul",
    )(x, w)


def matmul3(x, w, layer):
    B, T, K = x.shape
    return matmul(x.reshape(B * T, K), w, layer).reshape(B, T, -1)


def layer_norm(x, g, b):
    xf = x.astype(jnp.float32)
    mu = jnp.mean(xf, -1, keepdims=True)
    var = jnp.mean(jnp.square(xf - mu), -1, keepdims=True)
    return ((xf - mu) * lax.rsqrt(var + LN_EPS) * g + b).astype(x.dtype)


def l2_normalize(x):
    return x / jnp.maximum(jnp.sqrt(jnp.sum(x * x, -1, keepdims=True)), 1e-12)


def rel_bucket(d):
    d = jnp.maximum(d, 0)
    exact = REL_BUCKETS // 2
    logd = jnp.log(jnp.maximum(d, 1).astype(jnp.float32) / exact) / math.log(REL_MAX_DIST / exact)
    large = jnp.minimum(exact + (logd * (REL_BUCKETS - exact)).astype(jnp.int32), REL_BUCKETS - 1)
    return jnp.where(d < exact, d, large)


RW_CHUNK = 64


def _split(x):
    hi = x.astype(jnp.bfloat16)
    lo = (x - hi.astype(jnp.float32)).astype(jnp.bfloat16)
    return hi, lo


def _dot3(a, b, dims=(((1,), (0,)), ((), ()))):
    ah, al = _split(a)
    bh, bl = _split(b)
    d = lambda x, y: lax.dot_general(x, y, dims, preferred_element_type=jnp.float32)
    return d(ah, bh) + (d(ah, bl) + d(al, bh))


_NT = (((1,), (1,)), ((), ()))
_TN = (((0,), (0,)), ((), ()))


def _rwkv_chunk_body(r_ref, lw_ref, k_ref, v_ref, kk_ref, ka_ref, s0_ref, y_ref, s_ref, *, n_chunks, n_valid, n_heads):
    f32 = jnp.float32
    N, C = RW_HD, RW_CHUNK
    c_idx = pl.program_id(2)

    @pl.when(c_idx == 0)
    def _():
        s_ref[...] = s0_ref[...]

    row = lax.broadcasted_iota(jnp.int32, (C, C), 0)
    col = lax.broadcasted_iota(jnp.int32, (C, C), 1)
    tril = (row >= col).astype(f32)
    strict = row > col

    hs = range(n_heads)
    for ci in range(n_chunks):
        rows = slice(ci * C, (ci + 1) * C)
        padded = n_valid < n_chunks * C
        live = (lax.broadcasted_iota(jnp.int32, (C, N), 0) + ci * C) < n_valid

        def ld(ref, h):
            x = ref[0, rows, h * N:(h + 1) * N]
            return jnp.where(live, x, 0.0) if padded else x

        lw = [ld(lw_ref, h) for h in hs]
        G = [_dot3(tril, lw[h]) for h in hs]
        eg = [jnp.exp(G[h]) for h in hs]
        ing = [jnp.exp(-G[h]) for h in hs]
        ar = [jnp.concatenate([-ld(kk_ref, h) * jnp.exp(G[h] - lw[h]), ld(r_ref, h) * eg[h]], axis=0) for h in hs]
        bk = [jnp.concatenate([ld(ka_ref, h) * ing[h], ld(k_ref, h) * ing[h]], axis=0) for h in hs]
        v = [ld(v_ref, h) for h in hs]
        S0 = [s_ref[0, h] for h in hs]
        M = [_dot3(ar[h], bk[h], _NT) for h in hs]
        AS = [_dot3(ar[h], S0[h], _NT) for h in hs]
        P = [jnp.where(strict, M[h][0:C, 0:C], 0.0) for h in hs]
        M2 = [jnp.where(strict, M[h][0:C, C:2 * C], 0.0) for h in hs]
        M34 = [jnp.concatenate([M[h][C:2 * C, 0:C] * tril, M[h][C:2 * C, C:2 * C] * tril], axis=1) for h in hs]
        rhs = [AS[h][0:C] + _dot3(M2[h], v[h]) for h in hs]
        X = [rhs[h] + _dot3(P[h], rhs[h]) for h in hs]
        for _ in range(5):
            P = [_dot3(P[h], P[h]) for h in hs]
            X = [X[h] + _dot3(P[h], X[h]) for h in hs]
        sav = [jnp.concatenate([X[h], v[h]], axis=0) for h in hs]
        for h in hs:
            y_ref[0, rows, h * N:(h + 1) * N] = AS[h][C:2 * C] + _dot3(M34[h], sav[h])
        dS = [_dot3(sav[h], bk[h], _TN) for h in hs]
        for h in hs:
            s_ref[0, h] = (S0[h] + dS[h]) * eg[h][C - 1:C, :]


def rwkv_scan_chunked(r, lw, k, v, kk, ka, s0, chunks_per_step=2, heads_per_step=16):
    B, T, W = r.shape
    N, C = RW_HD, RW_CHUNK
    H = W // N
    HB = min(heads_per_step, H)
    TB = C * chunks_per_step if T >= C * chunks_per_step else -(-T // C) * C
    Tp = -(-T // TB) * TB
    n_valid = T if Tp != T else TB

    def prep(x):
        return jnp.pad(x, ((0, 0), (0, Tp - T), (0, 0))) if Tp != T else x

    ins = [prep(x) for x in (r, lw, k, v, kk, ka)]
    seq = pl.BlockSpec((1, TB, HB * N), lambda b, p, c: (b, c, p))
    st = pl.BlockSpec((1, HB, N, N), lambda b, p, c: (b, p, 0, 0))
    y, sT = pl.pallas_call(
        functools.partial(_rwkv_chunk_body, n_chunks=TB // C, n_valid=n_valid, n_heads=HB),
        grid=(B, H // HB, Tp // TB),
        in_specs=[seq] * 6 + [st],
        out_specs=[seq, st],
        out_shape=[jax.ShapeDtypeStruct((B, Tp, W), jnp.float32), jax.ShapeDtypeStruct((B, H, N, N), jnp.float32)],
        compiler_params=pltpu.CompilerParams(dimension_semantics=("parallel", "parallel", "arbitrary"),
                                             vmem_limit_bytes=V7X_VMEM_LIMIT_BYTES),
        name="rwkv_chunked",
    )(*ins, s0)
    return y[:, :T], sT


def rwkv7_mixer(p, shift0, s0, mu, w0, w_up, a0, a_up, g_up, k_k, k_a, r_k, ln_g, ln_b):
    f32 = jnp.float32
    B, T, _ = p.shape
    prev = jnp.concatenate([shift0[:, None].astype(p.dtype), p[:, :-1]], axis=1)
    m = p + mu * (prev - p)
    r = m[..., :RW_W]
    k = m[..., RW_W:2 * RW_W]
    v = m[..., 2 * RW_W:3 * RW_W]
    o = 3 * RW_W
    wl = m[..., o:o + RW_DECAY_R]
    o += RW_DECAY_R
    al = m[..., o:o + RW_AAA_R]
    o += RW_AAA_R
    gl = m[..., o:o + RW_GATE_R]
    w = -jax.nn.softplus(-(w0 + jnp.tanh(wl) @ w_up).astype(f32)) - 0.5
    log_decay = -jnp.exp(w)
    a = jax.nn.sigmoid((a0 + al @ a_up).astype(f32))
    g = jax.nn.sigmoid(gl) @ g_up

    def heads(t):
        return t.reshape(B, T, RW_HEADS, RW_HD).astype(f32)

    kk = l2_normalize(heads(k * k_k)).reshape(B, T, RW_W)
    k = k * (1.0 + (a - 1.0) * k_a)
    r_, k_, v_ = heads(r), heads(k), heads(v)

    y, sT = rwkv_scan_chunked(r, log_decay, k, v, kk, kk * a, s0.astype(f32))
    y = heads(y)
    ym = jnp.mean(y, -1, keepdims=True)
    yv = jnp.mean(jnp.square(y - ym), -1, keepdims=True)
    y = ((y - ym) * lax.rsqrt(yv + RW_GN_EPS)).reshape(B, T, RW_W) * ln_g + ln_b
    bonus = jnp.sum(r_ * k_ * r_k, -1, keepdims=True) * v_
    y = (y + bonus.reshape(B, T, RW_W)) * g
    return y.astype(p.dtype), p[:, -1], sT


def _dot1(a, b, dims=(((1,), (0,)), ((), ()))):
    return lax.dot_general(a.astype(jnp.bfloat16), b.astype(jnp.bfloat16), dims, preferred_element_type=jnp.float32)


def _gdn_body(q_ref, k_ref, v_ref, beta_ref, g_ref, s0_ref, o_ref, s_ref, *, n_chunks, n_valid, n_heads):
    f32 = jnp.float32
    D, C = GD_HD, GD_CHUNK
    hg = pl.program_id(1)
    c_idx = pl.program_id(2)

    @pl.when(c_idx == 0)
    def _():
        s_ref[...] = s0_ref[...]

    row = lax.broadcasted_iota(jnp.int32, (C, C), 0)
    col = lax.broadcasted_iota(jnp.int32, (C, C), 1)
    tri = row >= col
    trif = tri.astype(f32)
    strict = row > col
    eye = row == col
    ones = jnp.ones((C, C), f32)
    lane_h = lax.broadcasted_iota(jnp.int32, (C, beta_ref.shape[2]), 1)
    hs = range(n_heads)
    for ci in range(n_chunks):
        rows = slice(ci * C, (ci + 1) * C)
        padded = n_valid < n_chunks * C
        live = (lax.broadcasted_iota(jnp.int32, (C, 1), 0) + ci * C) < n_valid

        def ld(ref, h):
            x = ref[0, rows, h * D:(h + 1) * D]
            return jnp.where(live, x, 0.0) if padded else x

        g_all = g_ref[0, rows, :]
        b_all = beta_ref[0, rows, :]
        if padded:
            g_all = jnp.where(live, g_all, 0.0)
            b_all = jnp.where(live, b_all, 0.0)
        G_all = _dot3(trif, g_all)

        def colof(x, h):
            return jnp.sum(jnp.where(lane_h == hg * n_heads + h, x, 0.0), axis=1, keepdims=True)

        Gc = [colof(G_all, h) for h in hs]
        bc = [colof(b_all, h) for h in hs]
        GB = [jnp.broadcast_to(Gc[h], (C, C)) for h in hs]
        GR = [_dot3(ones, jnp.where(eye, GB[h], 0.0)) for h in hs]
        decay = [jnp.where(tri, jnp.exp(jnp.where(tri, GB[h] - GR[h], 0.0)), 0.0) for h in hs]
        eG = [jnp.exp(Gc[h]) for h in hs]
        Glast = [Gc[h][C - 1:C, :] for h in hs]
        q = [ld(q_ref, h) for h in hs]
        k = [ld(k_ref, h) for h in hs]
        v = [ld(v_ref, h) for h in hs]
        kb = [k[h] * bc[h] for h in hs]
        P = [-jnp.where(strict, _dot1(kb[h], k[h], _NT) * decay[h], 0.0) for h in hs]
        Aqk = [_dot1(q[h], k[h], _NT) * decay[h] for h in hs]
        rhs = [jnp.concatenate([v[h] * bc[h], kb[h] * eG[h]], axis=1) for h in hs]
        X = [rhs[h] + _dot3(P[h], rhs[h]) for h in hs]
        for _ in range(5):
            P = [_dot3(P[h], P[h]) for h in hs]
            X = [X[h] + _dot3(P[h], X[h]) for h in hs]
        S = [s_ref[0, h] for h in hs]
        vn = [X[h][:, 0:D] - _dot1(X[h][:, D:2 * D], S[h]) for h in hs]
        for h in hs:
            o_ref[0, rows, h * D:(h + 1) * D] = _dot1(q[h] * eG[h], S[h]) + _dot1(Aqk[h], vn[h])
        for h in hs:
            s_ref[0, h] = S[h] * jnp.exp(Glast[h]) + _dot1(k[h] * jnp.exp(Glast[h] - Gc[h]), vn[h], _TN)


def gdn_chunked(qkv, beta, g, s0, chunks_per_step=2, heads_per_step=8):
    B, T, W3 = qkv.shape
    D, C = GD_HD, GD_CHUNK
    H = W3 // (3 * D)
    HB = min(heads_per_step, H)
    TB = C * chunks_per_step if T >= C * chunks_per_step else -(-T // C) * C
    Tp = -(-T // TB) * TB
    n_valid = T if Tp != T else TB
    if Tp != T:
        padt = ((0, 0), (0, Tp - T), (0, 0))
        qkv, beta, g = jnp.pad(qkv, padt), jnp.pad(beta, padt), jnp.pad(g, padt)
    ng = H // HB

    def seq(part):
        return pl.BlockSpec((1, TB, HB * D), lambda b, p, c, part=part: (b, c, part * ng + p))

    sc = pl.BlockSpec((1, TB, H), lambda b, p, c: (b, c, 0))
    st = pl.BlockSpec((1, HB, D, D), lambda b, p, c: (b, p, 0, 0))
    o, sT = pl.pallas_call(
        functools.partial(_gdn_body, n_chunks=TB // C, n_valid=n_valid, n_heads=HB),
        grid=(B, ng, Tp // TB),
        in_specs=[seq(0), seq(1), seq(2), sc, sc, st],
        out_specs=[pl.BlockSpec((1, TB, HB * D), lambda b, p, c: (b, c, p)), st],
        out_shape=[jax.ShapeDtypeStruct((B, Tp, H * D), jnp.float32), jax.ShapeDtypeStruct((B, H, D, D), jnp.float32)],
        compiler_params=pltpu.CompilerParams(dimension_semantics=("parallel", "parallel", "arbitrary"),
                                             vmem_limit_bytes=V7X_VMEM_LIMIT_BYTES),
        name="gdn_chunked",
    )(qkv, qkv, qkv, beta, g, s0)
    return o[:, :T], sT


def gated_deltanet_mixer(p, conv0, s0, conv_w, a_log, dt_bias, norm_g):
    f32 = jnp.float32
    B, T, _ = p.shape
    qkv = p[..., :GD_QKV]
    z = p[..., GD_QKV:GD_QKV + GD_W]
    bl = p[..., GD_QKV + GD_W:GD_QKV + GD_W + GD_HEADS]
    al = p[..., GD_QKV + GD_W + GD_HEADS:]
    xc = jnp.concatenate([conv0.astype(p.dtype), qkv], axis=1)
    conv = xc[:, :T] * conv_w[0]
    for i in range(1, GD_CONV):
        conv = conv + xc[:, i:i + T] * conv_w[i]
    conv = jax.nn.silu(conv).astype(f32)
    q = l2_normalize(conv[..., :GD_W].reshape(B, T, GD_HEADS, GD_HD)) * GD_HD ** -0.5
    k = l2_normalize(conv[..., GD_W:2 * GD_W].reshape(B, T, GD_HEADS, GD_HD))
    v = conv[..., 2 * GD_W:].reshape(B, T, GD_HEADS, GD_HD)
    beta = jax.nn.sigmoid(bl.astype(f32))
    g = -jnp.exp(a_log.astype(f32)) * jax.nn.softplus((al + dt_bias).astype(f32))
    qkv_n = jnp.concatenate([q.reshape(B, T, GD_W), k.reshape(B, T, GD_W), conv[..., 2 * GD_W:]], axis=-1)
    o, sT = gdn_chunked(qkv_n, beta, g, s0.astype(f32))
    o = o.reshape(B, T, GD_HEADS, GD_HD)
    o = o * lax.rsqrt(jnp.mean(o * o, -1, keepdims=True) + 1e-6) * norm_g
    o = o.reshape(B, T, GD_W) * jax.nn.silu(z.astype(f32))
    return o.astype(p.dtype), xc[:, -(GD_CONV - 1):], sT


TQ = 128
NEG_BIG = -1e30


def _stack_groups(x):
    return jnp.concatenate([x[:, g * NS_HD:(g + 1) * NS_HD] for g in range(NS_GROUP)], axis=0)


def _nsa_prompt_body(q_ref, ksel_ref, vsel_ref, kwin_ref, vwin_ref, gate_ref, kcmp_ref, vcmp_ref,
                     bsel_ref, bcmp_ref, o_ref, m_ref, l_ref, acc_ref, mask_ref, *, n_tiles):
    f32, bf16 = jnp.float32, jnp.bfloat16
    qi = pl.program_id(2)
    R = NS_GROUP * TQ
    qg = (_stack_groups(q_ref[0]) * NS_HD ** -0.5).astype(bf16)
    row = lax.broadcasted_iota(jnp.int32, (TQ, TQ), 0)
    col = lax.broadcasted_iota(jnp.int32, (TQ, TQ), 1)
    t_q = qi * TQ + row

    def nt_dot(a, b):
        return lax.dot_general(a, b, (((1,), (1,)), ((), ())), preferred_element_type=f32)

    def tile4(x):
        return jnp.concatenate([x] * NS_GROUP, axis=0)

    sc = nt_dot(qg, kcmp_ref[0, 0].astype(bf16))
    sc = sc + jnp.concatenate([bcmp_ref[g, 0] for g in range(NS_GROUP)], axis=0)
    okc = tile4((t_q - (col * CMP_STRIDE + CMP_LEN - 1)) >= 0)
    mc = jnp.max(jnp.where(okc, sc, NEG_BIG), axis=-1, keepdims=True)
    ec = jnp.where(okc, jnp.exp(sc - mc), 0.0)
    pc = ec / jnp.maximum(jnp.sum(ec, axis=-1, keepdims=True), 1e-30)
    o_cmp = jnp.dot(pc.astype(bf16), vcmp_ref[0, 0].astype(bf16), preferred_element_type=f32)

    pc_sum = pc[0:TQ] + pc[TQ:2 * TQ] + pc[2 * TQ:3 * TQ] + pc[3 * TQ:4 * TQ]
    ns = n_tiles * (TQ // SEL_BLOCK)
    nbp = -(-ns // 8) * 8
    jrow = lax.broadcasted_iota(jnp.int32, (nbp, TQ), 0)
    lcol = lax.broadcasted_iota(jnp.int32, (nbp, TQ), 1)
    overlap_t = ((lcol * CMP_STRIDE < (jrow + 1) * SEL_BLOCK) & (lcol * CMP_STRIDE + CMP_LEN > jrow * SEL_BLOCK)).astype(f32)
    ps = lax.dot_general(overlap_t, pc_sum, (((1,), (1,)), ((), ())), preferred_element_type=f32,
                         precision=lax.Precision.HIGHEST)
    cur = (qi * TQ + lcol) // SEL_BLOCK
    valid = (jrow <= cur) & (jrow < ns)
    forced = (jrow == 0) | (jrow == cur) | (jrow == cur - 1)
    score = jnp.where(valid, jnp.where(forced, FORCE_SCORE, ps), -jnp.inf)
    rank = jnp.zeros((nbp, TQ), jnp.int32)
    for i in range(ns):
        ri = score[i:i + 1, :]
        beats = (ri > score) | ((ri == score) & (jrow > i))
        rank = rank + beats.astype(jnp.int32)
    sel_t = (valid & (rank < min(SEL_TOPK, ns))).astype(bf16)
    for kj in range(n_tiles):
        expand = (jrow == (kj * (TQ // SEL_BLOCK) + lcol // SEL_BLOCK)).astype(bf16)
        mask_ref[kj] = lax.dot_general(sel_t, expand, (((0,), (0,)), ((), ())), preferred_element_type=f32)

    def attend(k_ref, v_ref, lo, kind):
        m_ref[...] = jnp.full((R, NS_HD), NEG_BIG, f32)
        l_ref[...] = jnp.zeros((R, NS_HD), f32)
        acc_ref[...] = jnp.zeros((R, NS_HD), f32)

        def step(pj, carry):
            kj = 2 * pj
            off = pl.multiple_of(kj * TQ, 2 * TQ)
            kt = k_ref[0, pl.ds(off, 2 * TQ), :].astype(bf16)
            vt = v_ref[0, pl.ds(off, 2 * TQ), :].astype(bf16)
            delta = qi - kj
            delta1 = jnp.maximum(delta - 1, 0)
            s = nt_dot(qg, kt)
            s = s + jnp.concatenate(
                [jnp.concatenate([bsel_ref[g, delta], bsel_ref[g, delta1]], axis=1) for g in range(NS_GROUP)], axis=0)
            d0 = delta * TQ + row - col
            d = jnp.concatenate([d0, d0 - TQ], axis=1)
            if kind == "sel":
                ok = (d >= 0) & (jnp.concatenate([mask_ref[kj], mask_ref[kj + 1]], axis=1) > 0.5)
            else:
                ok = (d >= 0) & (d <= WINDOW)
            ok = tile4(ok)
            m_old = m_ref[...]
            m_new = jnp.maximum(m_old, jnp.max(jnp.where(ok, s, NEG_BIG), axis=-1, keepdims=True))
            e = jnp.where(ok, jnp.exp(s - jnp.concatenate([m_new, m_new], axis=1)), 0.0)
            scale = jnp.exp(m_old - m_new)
            l_ref[...] = l_ref[...] * scale + jnp.sum(e, axis=-1, keepdims=True)
            acc_ref[...] = acc_ref[...] * scale + jnp.dot(e.astype(bf16), vt, preferred_element_type=f32)
            m_ref[...] = m_new
            return carry

        lax.fori_loop(lo // 2, qi // 2 + 1, step, 0)
        return acc_ref[...] / jnp.maximum(l_ref[...], 1e-30)

    o_sel = attend(ksel_ref, vsel_ref, 0, "sel")
    o_win = attend(kwin_ref, vwin_ref, jnp.maximum(qi - WINDOW // TQ, 0), "win")

    gates = jax.nn.sigmoid(gate_ref[0, 0].astype(f32))
    outs = []
    for g in range(NS_GROUP):
        sl = slice(g * TQ, (g + 1) * TQ)
        og = (gates[:, g:g + 1] * o_cmp[sl] + gates[:, NS_GROUP + g:NS_GROUP + g + 1] * o_sel[sl]
              + gates[:, 2 * NS_GROUP + g:2 * NS_GROUP + g + 1] * o_win[sl])
        outs.append(og)
    o_ref[0] = jnp.concatenate(outs, axis=-1)


def bias_lookup(rel_bias, d):
    onehot = jax.nn.one_hot(rel_bucket(d), REL_BUCKETS, dtype=jnp.float32)
    return jnp.dot(onehot, rel_bias.astype(jnp.float32), precision=lax.Precision.HIGHEST)


def nsa_bias_tiles(rel_bias, n_tiles):
    iq = jnp.arange(TQ)[:, None]
    ik = jnp.arange(TQ)[None, :]
    dl = jnp.arange(n_tiles)[:, None, None]
    bsel = jnp.transpose(bias_lookup(rel_bias, dl * TQ + iq - ik), (3, 0, 1, 2))
    bcmp = jnp.transpose(bias_lookup(rel_bias, dl * TQ + iq - (ik * CMP_STRIDE + CMP_LEN - 1)), (3, 0, 1, 2))
    return bsel, bcmp


def _cmp_prompt_body(k_ref, v_ref, k1_ref, k2_ref, v1_ref, v2_ref, kc_ref, vc_ref):
    f32, bf16 = jnp.float32, jnp.bfloat16
    n_sub = k_ref.shape[1] // CMP_STRIDE
    half = CMP_STRIDE * NS_HD
    for x_ref, w1_ref, w2_ref, o_ref in ((k_ref, k1_ref, k2_ref, kc_ref), (v_ref, v1_ref, v2_ref, vc_ref)):
        a = jnp.zeros((n_sub, CMP_HIDDEN), f32)
        b = jnp.zeros((n_sub, CMP_HIDDEN), f32)
        for p in range(CMP_STRIDE):
            x = x_ref[0, pl.ds(p, n_sub, stride=CMP_STRIDE), :].astype(bf16)
            a = a + jnp.dot(x, w1_ref[p * NS_HD:(p + 1) * NS_HD, :].astype(bf16), preferred_element_type=f32)
            b = b + jnp.dot(x, w1_ref[half + p * NS_HD:half + (p + 1) * NS_HD, :].astype(bf16), preferred_element_type=f32)
        h = gelu_erf(a + jnp.concatenate([b[1:], b[:1]], axis=0))
        o_ref[0, 0] = jnp.dot(h.astype(bf16), w2_ref[...].astype(bf16), preferred_element_type=f32)


def nsa_compress_prompt(pn, phi_k1, phi_k2, phi_v1, phi_v2):
    B, T, _ = pn.shape
    n_sub = T // CMP_STRIDE
    kv0 = NS_W // NS_HD
    w1 = pl.BlockSpec((CMP_LEN * NS_HD, CMP_HIDDEN), lambda b, k: (0, 0))
    w2 = pl.BlockSpec((CMP_HIDDEN, NS_HD), lambda b, k: (0, 0))
    out = pl.BlockSpec((1, 1, n_sub, NS_HD), lambda b, k: (b, k, 0, 0))
    return pl.pallas_call(
        _cmp_prompt_body,
        grid=(B, NS_KV),
        in_specs=[pl.BlockSpec((1, T, NS_HD), lambda b, k: (b, 0, kv0 + k)),
                  pl.BlockSpec((1, T, NS_HD), lambda b, k: (b, 0, kv0 + NS_KV + k)),
                  w1, w2, w1, w2],
        out_specs=[out, out],
        out_shape=[jax.ShapeDtypeStruct((B, NS_KV, n_sub, NS_HD), jnp.float32)] * 2,
        compiler_params=pltpu.CompilerParams(dimension_semantics=("parallel", "parallel"),
                                             vmem_limit_bytes=V7X_VMEM_LIMIT_BYTES),
        name="nsa_compress_prompt",
    )(pn, pn, phi_k1, phi_k2, phi_v1, phi_v2)


def nsa_prompt_attention(pn, kc, vc, bsel, bcmp):
    B, T, _ = pn.shape
    n_tiles = T // TQ
    assert T % (2 * TQ) == 0 and kc.shape[2] == TQ
    glog = pn[..., NS_W + 6 * NS_KVW:].reshape(B, T, 3, NS_KV, NS_GROUP)
    glog = jnp.transpose(glog, (0, 3, 1, 2, 4)).reshape(B, NS_KV, T, 3 * NS_GROUP)
    kv0 = NS_W // NS_HD

    def kv_spec(slot):
        return pl.BlockSpec((1, T, NS_HD), lambda b, k, i, s=slot: (b, 0, kv0 + s * NS_KV + k))

    R = NS_GROUP * TQ
    return pl.pallas_call(
        functools.partial(_nsa_prompt_body, n_tiles=n_tiles),
        grid=(B, NS_KV, n_tiles),
        in_specs=[
            pl.BlockSpec((1, TQ, NS_GROUP * NS_HD), lambda b, k, i: (b, i, k)),
            kv_spec(2), kv_spec(3), kv_spec(4), kv_spec(5),
            pl.BlockSpec((1, 1, TQ, 3 * NS_GROUP), lambda b, k, i: (b, k, i, 0)),
            pl.BlockSpec((1, 1, TQ, NS_HD), lambda b, k, i: (b, k, 0, 0)),
            pl.BlockSpec((1, 1, TQ, NS_HD), lambda b, k, i: (b, k, 0, 0)),
            pl.BlockSpec((NS_GROUP, n_tiles, TQ, TQ), lambda b, k, i: (k, 0, 0, 0)),
            pl.BlockSpec((NS_GROUP, 1, TQ, TQ), lambda b, k, i: (k, i, 0, 0)),
        ],
        out_specs=pl.BlockSpec((1, TQ, NS_GROUP * NS_HD), lambda b, k, i: (b, i, k)),
        out_shape=jax.ShapeDtypeStruct((B, T, NS_W), jnp.float32),
        scratch_shapes=[pltpu.VMEM((R, NS_HD), jnp.float32), pltpu.VMEM((R, NS_HD), jnp.float32),
                        pltpu.VMEM((R, NS_HD), jnp.float32), pltpu.VMEM((n_tiles, TQ, TQ), jnp.float32)],
        compiler_params=pltpu.CompilerParams(
            dimension_semantics=("parallel", "parallel", "arbitrary"),
            vmem_limit_bytes=V7X_VMEM_LIMIT_BYTES),
        name="nsa_prompt_attention",
    )(pn, pn, pn, pn, pn, glog, kc, vc, bsel, bcmp)


def nsa_prompt(p, phi_k1, phi_k2, phi_v1, phi_v2, bsel, bcmp):
    B, T, _ = p.shape
    kv = p[..., NS_W:NS_W + 6 * NS_KVW].reshape(B, T, 6, NS_KV, NS_HD)
    kc, vc = nsa_compress_prompt(p, phi_k1, phi_k2, phi_v1, phi_v2)
    o = nsa_prompt_attention(p, kc, vc, bsel, bcmp)
    wl = min(WINDOW, T)
    return o, kv[:, :, :4], kv[:, T - wl:, 4:]


CMP_PAGE_GROUP = 8
ROW_VECS = 4 * NS_KV
PAGE_ROWS = PAGE_SIZE * ROW_VECS


def _nt_dot(a, b):
    return lax.dot_general(a, b, (((1,), (1,)), ((), ())), preferred_element_type=jnp.float32)


def _cmp_pages_body(pt_ref, page_ref, k1_ref, v1_ref, ab_ref, seq_ref):
    bf16 = jnp.bfloat16
    pg = pl.program_id(1)
    slot_in_group = pg % CMP_PAGE_GROUP
    row0 = pl.multiple_of(slot_in_group * PAGE_ROWS, PAGE_ROWS)
    seq_ref[pl.ds(row0, PAGE_ROWS), :] = page_ref[...]

    @pl.when(slot_in_group == CMP_PAGE_GROUP - 1)
    def _():
        n_sub = CMP_PAGE_GROUP * PAGE_SIZE // CMP_STRIDE
        half = CMP_STRIDE * NS_HD
        for c in range(2 * NS_KV):
            w_ref = k1_ref if c < NS_KV else v1_ref
            a = jnp.zeros((n_sub, CMP_HIDDEN), jnp.float32)
            b = jnp.zeros((n_sub, CMP_HIDDEN), jnp.float32)
            for p in range(CMP_STRIDE):
                x = seq_ref[pl.ds(p * ROW_VECS + c, n_sub, stride=CMP_STRIDE * ROW_VECS), :].astype(bf16)
                a = a + jnp.dot(x, w_ref[p * NS_HD:(p + 1) * NS_HD, :].astype(bf16), preferred_element_type=jnp.float32)
                b = b + jnp.dot(x, w_ref[half + p * NS_HD:half + (p + 1) * NS_HD, :].astype(bf16), preferred_element_type=jnp.float32)
            ab_ref[0, c] = jnp.concatenate([a, b], axis=-1)


def cache_rows(cache_kv):
    return cache_kv.reshape(-1, NS_HD)


def nsa_compress_pages(rows, layer, n_phys, page_table, phi_k1, phi_v1):
    DB, n_pages = page_table.shape
    assert n_pages % CMP_PAGE_GROUP == 0
    subs_per_group = CMP_PAGE_GROUP * PAGE_SIZE // CMP_STRIDE
    n_sub = n_pages * PAGE_SIZE // CMP_STRIDE
    grid_spec = pltpu.PrefetchScalarGridSpec(
        num_scalar_prefetch=1,
        grid=(DB, n_pages),
        in_specs=[pl.BlockSpec((PAGE_ROWS, NS_HD), lambda b, g, pt: (layer * n_phys + pt[b, g], 0)),
                  pl.BlockSpec((CMP_LEN * NS_HD, CMP_HIDDEN), lambda b, g, pt: (0, 0)),
                  pl.BlockSpec((CMP_LEN * NS_HD, CMP_HIDDEN), lambda b, g, pt: (0, 0))],
        out_specs=pl.BlockSpec((1, 2 * NS_KV, subs_per_group, 2 * CMP_HIDDEN),
                               lambda b, g, pt: (b, 0, g // CMP_PAGE_GROUP, 0)),
        scratch_shapes=[pltpu.VMEM((CMP_PAGE_GROUP * PAGE_ROWS, NS_HD), jnp.float32)])
    return pl.pallas_call(
        _cmp_pages_body, grid_spec=grid_spec,
        out_shape=jax.ShapeDtypeStruct((DB, 2 * NS_KV, n_sub, 2 * CMP_HIDDEN), jnp.float32),
        compiler_params=pltpu.CompilerParams(dimension_semantics=("parallel", "arbitrary"),
                                             vmem_limit_bytes=V7X_VMEM_LIMIT_BYTES),
        name="nsa_compress_pages",
    )(page_table, rows, phi_k1, phi_v1)


def _nsa_sample_body(pt_ref, q_ref, page_ref, knew_ref, vnew_ref, win_ref, wnew_ref, gate_ref,
                     ab_ref, k2_ref, v2_ref, bsel_ref, bwin_ref, bcmp_ref, o_ref,
                     m_ref, l_ref, acc_ref, sel_ref, ocmp_ref, *, n_pages, n_new):
    f32, bf16 = jnp.float32, jnp.bfloat16
    pg = pl.program_id(1)
    QG = NS_GROUP * n_new
    R = NS_KV * QG
    n_sub = n_pages * PAGE_SIZE // CMP_STRIDE
    n_cmp = n_sub - 1
    ns = n_pages * (PAGE_SIZE // SEL_BLOCK) + 1
    NSP = sel_ref.shape[1]
    past = n_pages * PAGE_SIZE
    qs = (q_ref[0] * NS_HD ** -0.5).astype(bf16)

    @pl.when(pg == 0)
    def _():
        m_ref[...] = jnp.full((R, NS_HD), NEG_BIG, f32)
        l_ref[...] = jnp.zeros((R, NS_HD), f32)
        acc_ref[...] = jnp.zeros((R, NS_HD), f32)
        ncol = lax.broadcasted_iota(jnp.int32, (QG, n_sub), 1)
        okc = ncol < n_cmp
        orow = lax.broadcasted_iota(jnp.int32, (n_sub, NSP), 0)
        ocol = lax.broadcasted_iota(jnp.int32, (n_sub, NSP), 1)
        overlap = ((orow * CMP_STRIDE < (ocol + 1) * SEL_BLOCK) & (orow * CMP_STRIDE + CMP_LEN > ocol * SEL_BLOCK)
                   & (orow < n_cmp)).astype(f32)
        jcol = lax.broadcasted_iota(jnp.int32, (n_new, NSP), 1)
        qrow = lax.broadcasted_iota(jnp.int32, (n_new, NSP), 0)
        cur = (past + qrow) // SEL_BLOCK
        valid = (jcol <= cur) & (jcol < ns)
        forced = (jcol == 0) | (jcol == cur) | (jcol == cur - 1)
        for kv in range(NS_KV):
            def cmp_of(c, w2_ref):
                a = ab_ref[0, c, :, 0:CMP_HIDDEN]
                b = ab_ref[0, c, :, CMP_HIDDEN:2 * CMP_HIDDEN]
                b = jnp.concatenate([b[1:], b[:1]], axis=0)
                h = gelu_erf(a + b)
                return jnp.dot(h.astype(bf16), w2_ref[...].astype(bf16), preferred_element_type=f32)
            kc = cmp_of(kv, k2_ref)
            vc = cmp_of(NS_KV + kv, v2_ref)
            sc = _nt_dot(qs[kv * QG:(kv + 1) * QG], kc.astype(bf16)) + bcmp_ref[kv * QG:(kv + 1) * QG, :]
            mc = jnp.max(jnp.where(okc, sc, NEG_BIG), axis=-1, keepdims=True)
            ec = jnp.where(okc, jnp.exp(sc - mc), 0.0)
            pc = ec / jnp.maximum(jnp.sum(ec, axis=-1, keepdims=True), 1e-30)
            ocmp_ref[kv * QG:(kv + 1) * QG, :] = jnp.dot(pc.astype(bf16), vc.astype(bf16), preferred_element_type=f32)
            pc_sum = pc[0:n_new]
            for g in range(1, NS_GROUP):
                pc_sum = pc_sum + pc[g * n_new:(g + 1) * n_new]
            ps = jnp.dot(pc_sum, overlap, preferred_element_type=f32, precision=lax.Precision.HIGHEST)
            score = jnp.where(valid, jnp.where(forced, FORCE_SCORE, ps), -jnp.inf)
            rank = jnp.zeros((n_new, NSP), jnp.int32)
            for i in range(ns):
                ci = jnp.broadcast_to(score[:, i:i + 1], (n_new, NSP))
                rank = rank + ((ci > score) | ((ci == score) & (jcol > i))).astype(jnp.int32)
            sel = (valid & (rank < min(SEL_TOPK, ns))).astype(f32)
            sel_ref[kv * QG:(kv + 1) * QG, :] = jnp.concatenate([sel] * NS_GROUP, axis=0)

    is_new = pg == n_pages
    krow = lax.broadcasted_iota(jnp.int32, (NSP, PAGE_SIZE), 0)
    kcol = lax.broadcasted_iota(jnp.int32, (NSP, PAGE_SIZE), 1)
    expand = (krow == pg * (PAGE_SIZE // SEL_BLOCK) + kcol // SEL_BLOCK).astype(bf16)
    inblock = jnp.dot(sel_ref[...].astype(bf16), expand, preferred_element_type=f32) > 0.5
    rr = lax.broadcasted_iota(jnp.int32, (R, PAGE_SIZE), 0)
    cc = lax.broadcasted_iota(jnp.int32, (R, PAGE_SIZE), 1)
    causal_new = (cc <= rr % n_new) & (cc < n_new)
    ok = inblock & (jnp.logical_not(is_new) | causal_new)
    def page_vec(slot, kv, new_ref):
        old = page_ref[pl.ds(slot * NS_KV + kv, PAGE_SIZE, stride=ROW_VECS), :]
        return jnp.where(is_new, new_ref[0, :, kv * NS_HD:(kv + 1) * NS_HD], old).astype(bf16)

    s = jnp.concatenate([_nt_dot(qs[kv * QG:(kv + 1) * QG], page_vec(2, kv, knew_ref))
                         for kv in range(NS_KV)], axis=0) + bsel_ref[...]
    m_old = m_ref[...]
    m_new = jnp.maximum(m_old, jnp.max(jnp.where(ok, s, NEG_BIG), axis=-1, keepdims=True))
    e = jnp.where(ok, jnp.exp(s - m_new), 0.0)
    scale = jnp.exp(m_old - m_new)
    l_ref[...] = l_ref[...] * scale + jnp.sum(e, axis=-1, keepdims=True)
    pv = jnp.concatenate([jnp.dot(e[kv * QG:(kv + 1) * QG].astype(bf16), page_vec(3, kv, vnew_ref),
                                  preferred_element_type=f32) for kv in range(NS_KV)], axis=0)
    acc_ref[...] = acc_ref[...] * scale + pv
    m_ref[...] = m_new

    @pl.when(is_new)
    def _():
        o_sel = acc_ref[...] / jnp.maximum(l_ref[...], 1e-30)
        Wb = win_ref.shape[1]
        wc = lax.broadcasted_iota(jnp.int32, (R, Wb + PAGE_SIZE), 1)
        wr = lax.broadcasted_iota(jnp.int32, (R, Wb + PAGE_SIZE), 0) % n_new
        dw = (Wb + wr) - wc
        okw = (dw >= 0) & (dw <= WINDOW) & ((wc < Wb) | (wc - Wb < n_new))
        sw = []
        for kv in range(NS_KV):
            kw = jnp.concatenate([win_ref[0, :, kv * NS_HD:(kv + 1) * NS_HD],
                                  wnew_ref[0, :, kv * NS_HD:(kv + 1) * NS_HD]], axis=0).astype(bf16)
            sw.append(_nt_dot(qs[kv * QG:(kv + 1) * QG], kw))
        sw = jnp.concatenate(sw, axis=0) + bwin_ref[...]
        mw = jnp.max(jnp.where(okw, sw, NEG_BIG), axis=-1, keepdims=True)
        ew = jnp.where(okw, jnp.exp(sw - mw), 0.0)
        pw = (ew / jnp.maximum(jnp.sum(ew, axis=-1, keepdims=True), 1e-30)).astype(bf16)
        o_win = []
        for kv in range(NS_KV):
            vw = jnp.concatenate([win_ref[0, :, NS_KVW + kv * NS_HD:NS_KVW + (kv + 1) * NS_HD],
                                  wnew_ref[0, :, NS_KVW + kv * NS_HD:NS_KVW + (kv + 1) * NS_HD]], axis=0).astype(bf16)
            o_win.append(jnp.dot(pw[kv * QG:(kv + 1) * QG], vw, preferred_element_type=f32))
        o_win = jnp.concatenate(o_win, axis=0)
        gates = jax.nn.sigmoid(gate_ref[0].astype(f32))
        o = gates[:, 0:1] * ocmp_ref[...] + gates[:, 1:2] * o_sel + gates[:, 2:3] * o_win
        o_ref[0] = jnp.concatenate([o[h * n_new:(h + 1) * n_new] for h in range(NS_HEADS)], axis=-1)


def nsa_sample_bias(rel_bias, n_new, past, wb):
    n_sub = past // CMP_STRIDE
    tq = past + jnp.arange(n_new)[:, None]
    pos = jnp.arange(past + PAGE_SIZE)[None, :]
    R = NS_HEADS * n_new
    bsel = jnp.transpose(bias_lookup(rel_bias, tq - pos), (2, 0, 1)).reshape(R, past + PAGE_SIZE)
    cpos = (jnp.arange(n_sub) * CMP_STRIDE + CMP_LEN - 1)[None, :]
    bcmp = jnp.transpose(bias_lookup(rel_bias, tq - cpos), (2, 0, 1)).reshape(R, n_sub)
    return bsel, bsel[:, past - wb:], bcmp


def nsa_sample_attention(ps, rows, layer, n_phys, page_table, win_buf, ab, phi_k2, phi_v2, bias):
    bsel, bwin, bcmp = bias
    DB, Tn, _ = ps.shape
    n_pages = page_table.shape[1]
    Wb = win_buf.shape[1]
    R = NS_HEADS * Tn
    ns = n_pages * (PAGE_SIZE // SEL_BLOCK) + 1
    NSP = -(-ns // 128) * 128
    n_sub = n_pages * PAGE_SIZE // CMP_STRIDE
    q = jnp.transpose(ps[..., :NS_W].reshape(DB, Tn, NS_HEADS, NS_HD), (0, 2, 1, 3)).reshape(DB, R, NS_HD)
    kvn = ps[..., NS_W:NS_W + 6 * NS_KVW].reshape(DB, Tn, 6, NS_KVW)
    padn = ((0, 0), (0, PAGE_SIZE - Tn), (0, 0))
    knew = jnp.pad(kvn[:, :, 2], padn)
    vnew = jnp.pad(kvn[:, :, 3], padn)
    wnew = jnp.pad(jnp.concatenate([kvn[:, :, 4], kvn[:, :, 5]], axis=-1), padn)
    win2 = win_buf.reshape(DB, Wb, 2 * NS_KVW)
    glog = jnp.transpose(ps[..., NS_W + 6 * NS_KVW:].reshape(DB, Tn, 3, NS_HEADS), (0, 3, 1, 2)).reshape(DB, R, 3)
    last = n_pages - 1
    grid_spec = pltpu.PrefetchScalarGridSpec(
        num_scalar_prefetch=1,
        grid=(DB, n_pages + 1),
        in_specs=[
            pl.BlockSpec((1, R, NS_HD), lambda b, g, pt: (b, 0, 0)),
            pl.BlockSpec((PAGE_ROWS, NS_HD), lambda b, g, pt: (layer * n_phys + pt[b, jnp.minimum(g, last)], 0)),
            pl.BlockSpec((1, PAGE_SIZE, NS_KVW), lambda b, g, pt: (b, 0, 0)),
            pl.BlockSpec((1, PAGE_SIZE, NS_KVW), lambda b, g, pt: (b, 0, 0)),
            pl.BlockSpec((1, Wb, 2 * NS_KVW), lambda b, g, pt: (b, 0, 0)),
            pl.BlockSpec((1, PAGE_SIZE, 2 * NS_KVW), lambda b, g, pt: (b, 0, 0)),
            pl.BlockSpec((1, R, 3), lambda b, g, pt: (b, 0, 0)),
            pl.BlockSpec((1, 2 * NS_KV, n_sub, 2 * CMP_HIDDEN), lambda b, g, pt: (b, 0, 0, 0)),
            pl.BlockSpec((CMP_HIDDEN, NS_HD), lambda b, g, pt: (0, 0)),
            pl.BlockSpec((CMP_HIDDEN, NS_HD), lambda b, g, pt: (0, 0)),
            pl.BlockSpec((R, PAGE_SIZE), lambda b, g, pt: (0, g)),
            pl.BlockSpec((R, Wb + PAGE_SIZE), lambda b, g, pt: (0, 0)),
            pl.BlockSpec((R, n_sub), lambda b, g, pt: (0, 0)),
        ],
        out_specs=pl.BlockSpec((1, Tn, NS_W), lambda b, g, pt: (b, 0, 0)),
        scratch_shapes=[pltpu.VMEM((R, NS_HD), jnp.float32), pltpu.VMEM((R, NS_HD), jnp.float32),
                        pltpu.VMEM((R, NS_HD), jnp.float32), pltpu.VMEM((R, NSP), jnp.float32),
                        pltpu.VMEM((R, NS_HD), jnp.float32)])
    return pl.pallas_call(
        functools.partial(_nsa_sample_body, n_pages=n_pages, n_new=Tn),
        grid_spec=grid_spec,
        out_shape=jax.ShapeDtypeStruct((DB, Tn, NS_W), jnp.float32),
        compiler_params=pltpu.CompilerParams(dimension_semantics=("parallel", "arbitrary"),
                                             vmem_limit_bytes=V7X_VMEM_LIMIT_BYTES),
        name="nsa_sample_attention",
    )(page_table, q, rows, knew, vnew, win2, wnew, glog, ab, phi_k2, phi_v2, bsel, bwin, bcmp)


def nsa_sample(p, rows, layer, n_phys, page_table, win_buf, phi_k1, phi_k2, phi_v1, phi_v2, bias):
    DB, Tn, _ = p.shape
    assert Tn < CMP_STRIDE and Tn <= SEL_BLOCK
    kv = p[..., NS_W:NS_W + 6 * NS_KVW].reshape(DB, Tn, 6, NS_KV, NS_HD)
    ab = nsa_compress_pages(rows, layer, n_phys, page_table, phi_k1, phi_v1)
    o = nsa_sample_attention(p, rows, layer, n_phys, page_table, win_buf, ab, phi_k2, phi_v2, bias)
    win = jnp.concatenate([win_buf, kv[:, :, 4:].astype(win_buf.dtype)], axis=1)
    return o, kv[:, :, :4], win[:, Tn:]


PEER_ROUTE_TM = 128
PEER_ROUTE_UNROLL = 4
PEER_ROUTE_GROUP = 16


def gelu_erf(x):
    return 0.5 * x * (1.0 + lax.erf(x * (2.0 ** -0.5)))


def _top_rows(work, n_rows, k, row_iota):
    vals, idxs = [], []
    for _ in range(k):
        m = jnp.max(work, axis=0, keepdims=True)
        idx = jnp.min(jnp.where(work == m, row_iota, n_rows), axis=0, keepdims=True)
        vals.append(m)
        idxs.append(idx)
        work = jnp.where(row_iota == idx, -jnp.inf, work)
    return vals, idxs


def _peer_route_body(q_ref, k1_ref, k2_ref, g_ref, i1_s, i2_s, w_s, gt_s):
    f32, bf16 = jnp.float32, jnp.bfloat16
    tm = q_ref.shape[0]
    half = PEER_DKEY // 2
    K = PEER_TOPK
    rows = lax.broadcasted_iota(jnp.int32, (PEER_NKEYS, tm), 0)
    n_cand = K + (K // 2 - 1) * (K // 2) + K // 2
    crow = lax.broadcasted_iota(jnp.int32, (n_cand, tm), 0)
    k1 = k1_ref[...].astype(bf16)
    k2 = k2_ref[...].astype(bf16)

    def nt_dot(a, b):
        return lax.dot_general(a, b, (((1,), (1,)), ((), ())), preferred_element_type=f32)

    for h in range(PEER_HEADS):
        q1 = q_ref[:, h * PEER_DKEY:h * PEER_DKEY + half].astype(bf16)
        q2 = q_ref[:, h * PEER_DKEY + half:(h + 1) * PEER_DKEY].astype(bf16)
        v1, i1 = _top_rows(nt_dot(k1, q1), PEER_NKEYS, K, rows)
        v2, i2 = _top_rows(nt_dot(k2, q2), PEER_NKEYS, K, rows)
        v2m = jnp.concatenate(v2, axis=0)
        i2m = jnp.concatenate(i2, axis=0)
        v1m = jnp.concatenate(v1, axis=0)
        i1m = jnp.concatenate(i1, axis=0)
        hk = K // 2
        cand = jnp.concatenate([v1[0] + v2m] + [v1[a] + v2m[0:hk] for a in range(1, hk)] + [v1m[hk:K] + v2[0]], axis=0)
        cidx = jnp.concatenate([i1[0] * PEER_NKEYS + i2m] + [i1[a] * PEER_NKEYS + i2m[0:hk] for a in range(1, hk)]
                               + [i1m[hk:K] * PEER_NKEYS + i2[0]], axis=0)
        sv, pos = _top_rows(cand, n_cand, K, crow)
        eidx = [jnp.max(jnp.where(crow == pos[k], cidx, 0), axis=0, keepdims=True) for k in range(K)]
        svm = jnp.concatenate(sv, axis=0)
        em = jnp.concatenate(eidx, axis=0)
        e = jnp.exp(svm - svm[0:1])
        gw = e / jnp.sum(e, axis=0, keepdims=True)
        i1_s[h * K:(h + 1) * K, :] = (em // PEER_NKEYS).astype(f32)
        i2_s[h * K:(h + 1) * K, :] = (em % PEER_NKEYS).astype(f32)
        w_s[h * K:(h + 1) * K, :] = gw
    i1_s[...] = i1_s[...].T
    i2_s[...] = i2_s[...].T
    w_s[...] = w_s[...].T
    sub = lax.broadcasted_iota(jnp.int32, (PEER_NKEYS, PEER_HEADS * K), 0).astype(f32)

    def token_group(tg, carry):
        base = pl.multiple_of(tg * PEER_ROUTE_GROUP, PEER_ROUTE_GROUP)
        for part in range(PEER_ROUTE_GROUP // PEER_ROUTE_UNROLL):
            us = [part * PEER_ROUTE_UNROLL + u for u in range(PEER_ROUTE_UNROLL)]
            a_w = [jnp.where(sub == i1_s[pl.ds(base + u, 1), :], w_s[pl.ds(base + u, 1), :], 0.0).astype(bf16) for u in us]
            b_1 = [jnp.where(sub == i2_s[pl.ds(base + u, 1), :], 1.0, 0.0).astype(bf16) for u in us]
            g = [nt_dot(a, b) for a, b in zip(a_w, b_1)]
            for u, gt in zip(us, g):
                gt_s[u * PEER_NKEYS:(u + 1) * PEER_NKEYS, :] = gt
        for c in range(PEER_NKEYS):
            g_ref[pl.ds(base, PEER_ROUTE_GROUP), c * PEER_NKEYS:(c + 1) * PEER_NKEYS] = (
                gt_s[pl.ds(c, PEER_ROUTE_GROUP, stride=PEER_NKEYS), :].astype(g_ref.dtype))
        return carry

    lax.fori_loop(0, tm // PEER_ROUTE_GROUP, token_group, 0)


def peer_route(q, k1, k2):
    n = q.shape[0]
    tm = PEER_ROUTE_TM
    S = PEER_HEADS * PEER_TOPK
    assert n % tm == 0 and S == tm
    return pl.pallas_call(
        _peer_route_body,
        grid=(n // tm,),
        in_specs=[pl.BlockSpec((tm, PEER_HEADS * PEER_DKEY), lambda i: (i, 0)),
                  pl.BlockSpec((PEER_NKEYS, PEER_DKEY // 2), lambda i: (0, 0)),
                  pl.BlockSpec((PEER_NKEYS, PEER_DKEY // 2), lambda i: (0, 0))],
        out_specs=pl.BlockSpec((tm, PEER_EXPERTS), lambda i: (i, 0)),
        out_shape=jax.ShapeDtypeStruct((n, PEER_EXPERTS), jnp.bfloat16),
        scratch_shapes=[pltpu.VMEM((S, tm), jnp.float32)] * 3
        + [pltpu.VMEM((PEER_ROUTE_GROUP * PEER_NKEYS, PEER_NKEYS), jnp.float32)],
        compiler_params=pltpu.CompilerParams(dimension_semantics=("parallel",),
                                             vmem_limit_bytes=V7X_VMEM_LIMIT_BYTES),
        name="peer_route",
    )(q, k1, k2)


def _peer_expert_body(x_ref, g_ref, u_ref, v_ref, o_ref):
    f32, bf16 = jnp.float32, jnp.bfloat16
    e = pl.program_id(1)
    h = lax.dot_general(x_ref[...], u_ref[...], (((1,), (1,)), ((), ())), preferred_element_type=f32)
    p = (g_ref[...].astype(f32) * gelu_erf(h)).astype(bf16)
    upd = jnp.dot(p, v_ref[...], preferred_element_type=f32)

    @pl.when(e == 0)
    def _():
        o_ref[...] = upd

    @pl.when(e > 0)
    def _():
        o_ref[...] += upd


def peer_experts(x, g, u, v, layer, tm=512, te=512):
    n, D = x.shape
    E = u.shape[1]
    tm = min(tm, n)
    assert n % tm == 0 and E % te == 0
    return pl.pallas_call(
        _peer_expert_body,
        grid=(n // tm, E // te),
        in_specs=[pl.BlockSpec((tm, D), lambda i, e: (i, 0)),
                  pl.BlockSpec((tm, te), lambda i, e: (i, e)),
                  pl.BlockSpec((None, te, D), lambda i, e: (layer, e, 0)),
                  pl.BlockSpec((None, te, D), lambda i, e: (layer, e, 0))],
        out_specs=pl.BlockSpec((tm, D), lambda i, e: (i, 0)),
        out_shape=jax.ShapeDtypeStruct((n, D), jnp.float32),
        compiler_params=pltpu.CompilerParams(dimension_semantics=("parallel", "arbitrary"),
                                             vmem_limit_bytes=V7X_VMEM_LIMIT_BYTES),
        name="peer_experts",
    )(x, g, u, v)


def peer_ffn(x, layer, wq, k1, k2, u_bf, v_bf):
    Bx, T, D = x.shape
    n = Bx * T
    pad = -n % PEER_ROUTE_TM
    xt = jnp.pad(x.reshape(n, D), ((0, pad), (0, 0)))
    g = peer_route(matmul(xt, wq, layer), k1, k2)
    out = peer_experts(xt.astype(jnp.bfloat16), g, u_bf, v_bf, layer)
    return out[:n].reshape(Bx, T, D).astype(x.dtype)


def residual_block(x, mix, layer, w_out, ln1_g, ln1_b, ln2_g, ln2_b, peer_wq, peer_k1, peer_k2, peer_u, peer_v):
    x = layer_norm(ALPHA * x + matmul3(mix, w_out, layer), ln1_g, ln1_b)
    return layer_norm(ALPHA * x + peer_ffn(x, layer, peer_wq, peer_k1, peer_k2, peer_u, peer_v), ln2_g, ln2_b)


def kernel(x_prompt, x_sample, cache_kv, cache_win, state_rwkv, state_rwkv_shift, state_gdn, state_gdn_conv, page_table, w_in, w_out, ln1_g, ln1_b, ln2_g, ln2_b, rw_mu, rw_w0, rw_w_up, rw_a0, rw_a_up, rw_g_up, rw_k_k, rw_k_a, rw_r_k, rw_ln_g, rw_ln_b, gd_conv_w, gd_a_log, gd_dt_bias, gd_norm_g, ns_phi_k1, ns_phi_k2, ns_phi_v1, ns_phi_v2, rel_bias, peer_wq, peer_k1, peer_k2, peer_u, peer_v):
    xp, xs = x_prompt, x_sample
    B = xp.shape[0]
    o_b = RW_COLS
    o_c = RW_COLS + GD_COLS
    bsel, bcmp = nsa_bias_tiles(rel_bias, SEQ // TQ)
    sbias = nsa_sample_bias(rel_bias, x_sample.shape[1], page_table.shape[1] * PAGE_SIZE, cache_win.shape[2])
    w_in_bf, w_out_bf, wq_bf, u_bf, v_bf = (w.astype(jnp.bfloat16) for w in (w_in, w_out, peer_wq, peer_u, peer_v))
    kv_rows = cache_rows(cache_kv)
    kv_p, kv_s, win_p, win_s, rw_p, rw_s, sh_p, sh_s, gd_p, gd_s, cv_p, cv_s = ([] for _ in range(12))
    for l in range(DEPTH):
        rw = (rw_mu[l], rw_w0[l], rw_w_up[l], rw_a0[l], rw_a_up[l], rw_g_up[l], rw_k_k[l], rw_k_a[l], rw_r_k[l], rw_ln_g[l], rw_ln_b[l])
        gd = (gd_conv_w[l], gd_a_log[l], gd_dt_bias[l], gd_norm_g[l])
        phi = (ns_phi_k1[l], ns_phi_k2[l], ns_phi_v1[l], ns_phi_v2[l])
        tail = (l, w_out_bf, ln1_g[l], ln1_b[l], ln2_g[l], ln2_b[l], wq_bf, peer_k1[l], peer_k2[l], u_bf, v_bf)
        pp = matmul3(xp, w_in_bf, l)
        a, sh, rs = rwkv7_mixer(pp[..., :o_b], jnp.zeros((B, RW_COLS), pp.dtype), jnp.zeros((B, RW_HEADS, RW_HD, RW_HD), jnp.float32), *rw)
        b, cv, gs = gated_deltanet_mixer(pp[..., o_b:o_c], jnp.zeros((B, GD_CONV - 1, GD_QKV), pp.dtype), jnp.zeros((B, GD_HEADS, GD_HD, GD_HD), jnp.float32), *gd)
        c, kvr, wr = nsa_prompt(pp[..., o_c:], *phi, bsel, bcmp)
        xp = residual_block(xp, jnp.concatenate([a, b, c.astype(a.dtype)], -1), *tail)
        kv_p.append(kvr)
        win_p.append(wr)
        rw_p.append(rs)
        sh_p.append(sh)
        gd_p.append(gs)
        cv_p.append(cv)
        ps = matmul3(xs, w_in_bf, l)
        a, sh, rs = rwkv7_mixer(ps[..., :o_b], state_rwkv_shift[l], state_rwkv[l], *rw)
        b, cv, gs = gated_deltanet_mixer(ps[..., o_b:o_c], state_gdn_conv[l], state_gdn[l], *gd)
        c, kvr, wr = nsa_sample(ps[..., o_c:], kv_rows, l, cache_kv.shape[1], page_table, cache_win[l], *phi, sbias)
        xs = residual_block(xs, jnp.concatenate([a, b, c.astype(a.dtype)], -1), *tail)
        kv_s.append(kvr)
        win_s.append(wr)
        rw_s.append(rs)
        sh_s.append(sh)
        gd_s.append(gs)
        cv_s.append(cv)
    st = jnp.stack
    return (xp, xs, st(kv_p), st(kv_s), st(win_p), st(win_s), st(rw_p), st(rw_s), st(sh_p), st(sh_s), st(gd_p), st(gd_s), st(cv_p), st(cv_s))
```

```python
import functools
import math

import jax
import jax.numpy as jnp
from jax import lax
from jax.experimental import pallas as pl
from jax.experimental.pallas import tpu as pltpu

D_MODEL = 4096
BATCH = 4
SEQ = 2048
DEPTH = 4
DEC_BATCH = 8
DEC_SEQ = 8
PAST_LEN = 8192
PAGE_SIZE = 128

ALPHA = (2 * DEPTH) ** 0.25
LN_EPS = 1e-5

RW_HD = 64
RW_W = D_MODEL // 4
RW_HEADS = RW_W // RW_HD
RW_DECAY_R = 64
RW_AAA_R = 64
RW_GATE_R = 160
RW_COLS = 3 * RW_W + RW_DECAY_R + RW_AAA_R + RW_GATE_R
RW_GN_EPS = 64e-5

GD_HD = 128
GD_W = D_MODEL // 4
GD_HEADS = GD_W // GD_HD
GD_QKV = 3 * GD_W
GD_CONV = 4
GD_CHUNK = 64
GD_COLS = GD_QKV + GD_W + 2 * GD_HEADS

NS_HD = 128
NS_W = D_MODEL // 2
NS_HEADS = NS_W // NS_HD
NS_KV = 4
NS_GROUP = NS_HEADS // NS_KV
NS_KVW = NS_KV * NS_HD
NS_COLS = NS_W + 6 * NS_KVW + 3 * NS_HEADS
CMP_LEN = 32
CMP_STRIDE = 16
CMP_HIDDEN = 128
SEL_BLOCK = 64
SEL_TOPK = 16
WINDOW = 512
NS_QBLOCK = 32
FORCE_SCORE = 1e4

REL_BUCKETS = 32
REL_MAX_DIST = 1024

D_MIX = RW_W + GD_W + NS_W
IN_COLS = RW_COLS + GD_COLS + NS_COLS

PEER_HEADS = 8
PEER_NKEYS = 128
PEER_EXPERTS = PEER_NKEYS ** 2
PEER_DKEY = 256
PEER_TOPK = 16
PEER_TBLOCK = 128

V7X_VMEM_LIMIT_BYTES = 56 * 1024 * 1024


def _matmul_body(x_ref, w_ref, o_ref):
    o_ref[...] = jnp.dot(x_ref[...].astype(jnp.bfloat16), w_ref[...].astype(jnp.bfloat16),
                         preferred_element_type=jnp.float32)


def _pick_tile(n, target):
    t = min(n, target)
    while n % t:
        t //= 2
    return t


def matmul(x, w, layer, tm=512, tn=1024):
    M, K = x.shape
    N = w.shape[2]
    tm = _pick_tile(M, tm)
    tn = min(tn, N)
    return pl.pallas_call(
        _matmul_body,
        grid=(M // tm, pl.cdiv(N, tn)),
        in_specs=[pl.BlockSpec((tm, K), lambda i, j: (i, 0)),
                  pl.BlockSpec((None, K, tn), lambda i, j: (layer, 0, j))],
        out_specs=pl.BlockSpec((tm, tn), lambda i, j: (i, j)),
        out_shape=jax.ShapeDtypeStruct((M, N), jnp.float32),
        compiler_params=pltpu.CompilerParams(
            dimension_semantics=("parallel", "parallel"),
            vmem_limit_bytes=V7X_VMEM_LIMIT_BYTES),
        name="proj_matmul",
    )(x, w)


def matmul3(x, w, layer):
    B, T, K = x.shape
    return matmul(x.reshape(B * T, K), w, layer).reshape(B, T, -1)


LN_ROWS = 256


def _deepnorm_body(x_ref, y_ref, g_ref, b_ref, o_ref):
    z = ALPHA * x_ref[...] + y_ref[...]
    mu = jnp.mean(z, -1, keepdims=True)
    var = jnp.mean(jnp.square(z - mu), -1, keepdims=True)
    o_ref[...] = (z - mu) * lax.rsqrt(var + LN_EPS) * g_ref[...] + b_ref[...]


def deepnorm(x, y, g, b):
    B, T, D = x.shape
    n = B * T
    tr = _pick_tile(n, LN_ROWS)
    row = pl.BlockSpec((tr, D), lambda i: (i, 0))
    vec = pl.BlockSpec((1, D), lambda i: (0, 0))
    out = pl.pallas_call(
        _deepnorm_body,
        grid=(n // tr,),
        in_specs=[row, row, vec, vec],
        out_specs=row,
        out_shape=jax.ShapeDtypeStruct((n, D), jnp.float32),
        compiler_params=pltpu.CompilerParams(dimension_semantics=("parallel",),
                                             vmem_limit_bytes=V7X_VMEM_LIMIT_BYTES),
        name="deepnorm",
    )(x.reshape(n, D), y.reshape(n, D), g.reshape(1, D), b.reshape(1, D))
    return out.reshape(B, T, D)


def l2_normalize(x):
    return x / jnp.maximum(jnp.sqrt(jnp.sum(x * x, -1, keepdims=True)), 1e-12)


def rel_bucket(d):
    d = jnp.maximum(d, 0)
    exact = REL_BUCKETS // 2
    logd = jnp.log(jnp.maximum(d, 1).astype(jnp.float32) / exact) / math.log(REL_MAX_DIST / exact)
    large = jnp.minimum(exact + (logd * (REL_BUCKETS - exact)).astype(jnp.int32), REL_BUCKETS - 1)
    return jnp.where(d < exact, d, large)


RW_CHUNK = 64


def _split(x):
    hi = x.astype(jnp.bfloat16)
    lo = (x - hi.astype(jnp.float32)).astype(jnp.bfloat16)
    return hi, lo


def _dot3(a, b, dims=(((1,), (0,)), ((), ()))):
    ah, al = _split(a)
    bh, bl = _split(b)
    d = lambda x, y: lax.dot_general(x, y, dims, preferred_element_type=jnp.float32)
    return d(ah, bh) + (d(ah, bl) + d(al, bh))


_NT = (((1,), (1,)), ((), ()))
_TN = (((0,), (0,)), ((), ()))


def _rwkv_chunk_body(r_ref, lw_ref, k_ref, v_ref, kk_ref, ka_ref, s0_ref, y_ref, s_ref, *, n_chunks, n_valid, n_heads):
    f32 = jnp.float32
    N, C = RW_HD, RW_CHUNK
    c_idx = pl.program_id(2)

    @pl.when(c_idx == 0)
    def _():
        s_ref[...] = s0_ref[...]

    row = lax.broadcasted_iota(jnp.int32, (C, C), 0)
    col = lax.broadcasted_iota(jnp.int32, (C, C), 1)
    tril = (row >= col).astype(f32)
    strict = row > col

    hs = range(n_heads)
    for ci in range(n_chunks):
        rows = slice(ci * C, (ci + 1) * C)
        padded = n_valid < n_chunks * C
        live = (lax.broadcasted_iota(jnp.int32, (C, N), 0) + ci * C) < n_valid

        def ld(ref, h):
            x = ref[0, rows, h * N:(h + 1) * N]
            return jnp.where(live, x, 0.0) if padded else x

        lw = [ld(lw_ref, h) for h in hs]
        G = [_dot3(tril, lw[h]) for h in hs]
        eg = [jnp.exp(G[h]) for h in hs]
        ing = [jnp.exp(-G[h]) for h in hs]
        ar = [jnp.concatenate([-ld(kk_ref, h) * jnp.exp(G[h] - lw[h]), ld(r_ref, h) * eg[h]], axis=0) for h in hs]
        bk = [jnp.concatenate([ld(ka_ref, h) * ing[h], ld(k_ref, h) * ing[h]], axis=0) for h in hs]
        v = [ld(v_ref, h) for h in hs]
        S0 = [s_ref[0, h] for h in hs]
        M = [_dot3(ar[h], bk[h], _NT) for h in hs]
        AS = [_dot3(ar[h], S0[h], _NT) for h in hs]
        P = [jnp.where(strict, M[h][0:C, 0:C], 0.0) for h in hs]
        M2 = [jnp.where(strict, M[h][0:C, C:2 * C], 0.0) for h in hs]
        M34 = [jnp.concatenate([M[h][C:2 * C, 0:C] * tril, M[h][C:2 * C, C:2 * C] * tril], axis=1) for h in hs]
        rhs = [AS[h][0:C] + _dot3(M2[h], v[h]) for h in hs]
        X = [rhs[h] + _dot3(P[h], rhs[h]) for h in hs]
        for _ in range(5):
            P = [_dot3(P[h], P[h]) for h in hs]
            X = [X[h] + _dot3(P[h], X[h]) for h in hs]
        sav = [jnp.concatenate([X[h], v[h]], axis=0) for h in hs]
        for h in hs:
            y_ref[0, rows, h * N:(h + 1) * N] = AS[h][C:2 * C] + _dot3(M34[h], sav[h])
        dS = [_dot3(sav[h], bk[h], _TN) for h in hs]
        for h in hs:
            s_ref[0, h] = (S0[h] + dS[h]) * eg[h][C - 1:C, :]


def rwkv_scan_chunked(r, lw, k, v, kk, ka, s0, chunks_per_step=2, heads_per_step=16):
    B, T, W = r.shape
    N, C = RW_HD, RW_CHUNK
    H = W // N
    HB = min(heads_per_step, H)
    TB = C * chunks_per_step if T >= C * chunks_per_step else -(-T // C) * C
    Tp = -(-T // TB) * TB
    n_valid = T if Tp != T else TB

    def prep(x):
        return jnp.pad(x, ((0, 0), (0, Tp - T), (0, 0))) if Tp != T else x

    ins = [prep(x) for x in (r, lw, k, v, kk, ka)]
    seq = pl.BlockSpec((1, TB, HB * N), lambda b, p, c: (b, c, p))
    st = pl.BlockSpec((1, HB, N, N), lambda b, p, c: (b, p, 0, 0))
    y, sT = pl.pallas_call(
        functools.partial(_rwkv_chunk_body, n_chunks=TB // C, n_valid=n_valid, n_heads=HB),
        grid=(B, H // HB, Tp // TB),
        in_specs=[seq] * 6 + [st],
        out_specs=[seq, st],
        out_shape=[jax.ShapeDtypeStruct((B, Tp, W), jnp.float32), jax.ShapeDtypeStruct((B, H, N, N), jnp.float32)],
        compiler_params=pltpu.CompilerParams(dimension_semantics=("parallel", "parallel", "arbitrary"),
                                             vmem_limit_bytes=V7X_VMEM_LIMIT_BYTES),
        name="rwkv_chunked",
    )(*ins, s0)
    return y[:, :T], sT


def rwkv7_mixer(p, shift0, s0, mu, w0, w_up, a0, a_up, g_up, k_k, k_a, r_k, ln_g, ln_b):
    f32 = jnp.float32
    B, T, _ = p.shape
    prev = jnp.concatenate([shift0[:, None].astype(p.dtype), p[:, :-1]], axis=1)
    m = p + mu * (prev - p)
    r = m[..., :RW_W]
    k = m[..., RW_W:2 * RW_W]
    v = m[..., 2 * RW_W:3 * RW_W]
    o = 3 * RW_W
    wl = m[..., o:o + RW_DECAY_R]
    o += RW_DECAY_R
    al = m[..., o:o + RW_AAA_R]
    o += RW_AAA_R
    gl = m[..., o:o + RW_GATE_R]
    w = -jax.nn.softplus(-(w0 + jnp.tanh(wl) @ w_up).astype(f32)) - 0.5
    log_decay = -jnp.exp(w)
    a = jax.nn.sigmoid((a0 + al @ a_up).astype(f32))
    g = jax.nn.sigmoid(gl) @ g_up

    def heads(t):
        return t.reshape(B, T, RW_HEADS, RW_HD).astype(f32)

    kk = l2_normalize(heads(k * k_k)).reshape(B, T, RW_W)
    k = k * (1.0 + (a - 1.0) * k_a)
    r_, k_, v_ = heads(r), heads(k), heads(v)

    y, sT = rwkv_scan_chunked(r, log_decay, k, v, kk, kk * a, s0.astype(f32))
    y = heads(y)
    ym = jnp.mean(y, -1, keepdims=True)
    yv = jnp.mean(jnp.square(y - ym), -1, keepdims=True)
    y = ((y - ym) * lax.rsqrt(yv + RW_GN_EPS)).reshape(B, T, RW_W) * ln_g + ln_b
    bonus = jnp.sum(r_ * k_ * r_k, -1, keepdims=True) * v_
    y = (y + bonus.reshape(B, T, RW_W)) * g
    return y.astype(p.dtype), p[:, -1], sT


def _dot1(a, b, dims=(((1,), (0,)), ((), ()))):
    return lax.dot_general(a.astype(jnp.bfloat16), b.astype(jnp.bfloat16), dims, preferred_element_type=jnp.float32)


def _gdn_body(q_ref, k_ref, v_ref, beta_ref, g_ref, s0_ref, o_ref, s_ref, *, n_chunks, n_valid, n_heads):
    f32 = jnp.float32
    D, C = GD_HD, GD_CHUNK
    hg = pl.program_id(1)
    c_idx = pl.program_id(2)

    @pl.when(c_idx == 0)
    def _():
        s_ref[...] = s0_ref[...]

    row = lax.broadcasted_iota(jnp.int32, (C, C), 0)
    col = lax.broadcasted_iota(jnp.int32, (C, C), 1)
    tri = row >= col
    trif = tri.astype(f32)
    strict = row > col
    eye = row == col
    ones = jnp.ones((C, C), f32)
    lane_h = lax.broadcasted_iota(jnp.int32, (C, beta_ref.shape[2]), 1)
    hs = range(n_heads)
    for ci in range(n_chunks):
        rows = slice(ci * C, (ci + 1) * C)
        padded = n_valid < n_chunks * C
        live = (lax.broadcasted_iota(jnp.int32, (C, 1), 0) + ci * C) < n_valid

        def ld(ref, h):
            x = ref[0, rows, h * D:(h + 1) * D]
            return jnp.where(live, x, 0.0) if padded else x

        g_all = g_ref[0, rows, :]
        b_all = beta_ref[0, rows, :]
        if padded:
            g_all = jnp.where(live, g_all, 0.0)
            b_all = jnp.where(live, b_all, 0.0)
        G_all = _dot3(trif, g_all)

        def colof(x, h):
            return jnp.sum(jnp.where(lane_h == hg * n_heads + h, x, 0.0), axis=1, keepdims=True)

        Gc = [colof(G_all, h) for h in hs]
        bc = [colof(b_all, h) for h in hs]
        GB = [jnp.broadcast_to(Gc[h], (C, C)) for h in hs]
        GR = [_dot3(ones, jnp.where(eye, GB[h], 0.0)) for h in hs]
        decay = [jnp.where(tri, jnp.exp(jnp.where(tri, GB[h] - GR[h], 0.0)), 0.0) for h in hs]
        eG = [jnp.exp(Gc[h]) for h in hs]
        Glast = [Gc[h][C - 1:C, :] for h in hs]
        q = [ld(q_ref, h) for h in hs]
        k = [ld(k_ref, h) for h in hs]
        v = [ld(v_ref, h) for h in hs]
        kb = [k[h] * bc[h] for h in hs]
        P = [-jnp.where(strict, _dot1(kb[h], k[h], _NT) * decay[h], 0.0) for h in hs]
        Aqk = [_dot1(q[h], k[h], _NT) * decay[h] for h in hs]
        rhs = [jnp.concatenate([v[h] * bc[h], kb[h] * eG[h]], axis=1) for h in hs]
        X = [rhs[h] + _dot3(P[h], rhs[h]) for h in hs]
        for _ in range(5):
            P = [_dot3(P[h], P[h]) for h in hs]
            X = [X[h] + _dot3(P[h], X[h]) for h in hs]
        S = [s_ref[0, h] for h in hs]
        vn = [X[h][:, 0:D] - _dot1(X[h][:, D:2 * D], S[h]) for h in hs]
        for h in hs:
            o_ref[0, rows, h * D:(h + 1) * D] = _dot1(q[h] * eG[h], S[h]) + _dot1(Aqk[h], vn[h])
        for h in hs:
            s_ref[0, h] = S[h] * jnp.exp(Glast[h]) + _dot1(k[h] * jnp.exp(Glast[h] - Gc[h]), vn[h], _TN)


def gdn_chunked(qkv, beta, g, s0, chunks_per_step=2, heads_per_step=8):
    B, T, W3 = qkv.shape
    D, C = GD_HD, GD_CHUNK
    H = W3 // (3 * D)
    HB = min(heads_per_step, H)
    TB = C * chunks_per_step if T >= C * chunks_per_step else -(-T // C) * C
    Tp = -(-T // TB) * TB
    n_valid = T if Tp != T else TB
    if Tp != T:
        padt = ((0, 0), (0, Tp - T), (0, 0))
        qkv, beta, g = jnp.pad(qkv, padt), jnp.pad(beta, padt), jnp.pad(g, padt)
    ng = H // HB

    def seq(part):
        return pl.BlockSpec((1, TB, HB * D), lambda b, p, c, part=part: (b, c, part * ng + p))

    sc = pl.BlockSpec((1, TB, H), lambda b, p, c: (b, c, 0))
    st = pl.BlockSpec((1, HB, D, D), lambda b, p, c: (b, p, 0, 0))
    o, sT = pl.pallas_call(
        functools.partial(_gdn_body, n_chunks=TB // C, n_valid=n_valid, n_heads=HB),
        grid=(B, ng, Tp // TB),
        in_specs=[seq(0), seq(1), seq(2), sc, sc, st],
        out_specs=[pl.BlockSpec((1, TB, HB * D), lambda b, p, c: (b, c, p)), st],
        out_shape=[jax.ShapeDtypeStruct((B, Tp, H * D), jnp.float32), jax.ShapeDtypeStruct((B, H, D, D), jnp.float32)],
        compiler_params=pltpu.CompilerParams(dimension_semantics=("parallel", "parallel", "arbitrary"),
                                             vmem_limit_bytes=V7X_VMEM_LIMIT_BYTES),
        name="gdn_chunked",
    )(qkv, qkv, qkv, beta, g, s0)
    return o[:, :T], sT


def gated_deltanet_mixer(p, conv0, s0, conv_w, a_log, dt_bias, norm_g):
    f32 = jnp.float32
    B, T, _ = p.shape
    qkv = p[..., :GD_QKV]
    z = p[..., GD_QKV:GD_QKV + GD_W]
    bl = p[..., GD_QKV + GD_W:GD_QKV + GD_W + GD_HEADS]
    al = p[..., GD_QKV + GD_W + GD_HEADS:]
    xc = jnp.concatenate([conv0.astype(p.dtype), qkv], axis=1)
    conv = xc[:, :T] * conv_w[0]
    for i in range(1, GD_CONV):
        conv = conv + xc[:, i:i + T] * conv_w[i]
    conv = jax.nn.silu(conv).astype(f32)
    q = l2_normalize(conv[..., :GD_W].reshape(B, T, GD_HEADS, GD_HD)) * GD_HD ** -0.5
    k = l2_normalize(conv[..., GD_W:2 * GD_W].reshape(B, T, GD_HEADS, GD_HD))
    v = conv[..., 2 * GD_W:].reshape(B, T, GD_HEADS, GD_HD)
    beta = jax.nn.sigmoid(bl.astype(f32))
    g = -jnp.exp(a_log.astype(f32)) * jax.nn.softplus((al + dt_bias).astype(f32))
    qkv_n = jnp.concatenate([q.reshape(B, T, GD_W), k.reshape(B, T, GD_W), conv[..., 2 * GD_W:]], axis=-1)
    o, sT = gdn_chunked(qkv_n, beta, g, s0.astype(f32))
    o = o.reshape(B, T, GD_HEADS, GD_HD)
    o = o * lax.rsqrt(jnp.mean(o * o, -1, keepdims=True) + 1e-6) * norm_g
    o = o.reshape(B, T, GD_W) * jax.nn.silu(z.astype(f32))
    return o.astype(p.dtype), xc[:, -(GD_CONV - 1):], sT


TQ = 128
NEG_BIG = -1e30


def _stack_groups(x):
    return jnp.concatenate([x[:, g * NS_HD:(g + 1) * NS_HD] for g in range(NS_GROUP)], axis=0)


def _nsa_prompt_body(q_ref, ksel_ref, vsel_ref, kwin_ref, vwin_ref, gate_ref, kcmp_ref, vcmp_ref,
                     bsel_ref, bcmp_ref, o_ref, m_ref, l_ref, acc_ref, mask_ref, *, n_tiles):
    f32, bf16 = jnp.float32, jnp.bfloat16
    qi = pl.program_id(2)
    R = NS_GROUP * TQ
    qg = (_stack_groups(q_ref[0]) * NS_HD ** -0.5).astype(bf16)
    row = lax.broadcasted_iota(jnp.int32, (TQ, TQ), 0)
    col = lax.broadcasted_iota(jnp.int32, (TQ, TQ), 1)
    t_q = qi * TQ + row

    def nt_dot(a, b):
        return lax.dot_general(a, b, (((1,), (1,)), ((), ())), preferred_element_type=f32)

    def tile4(x):
        return jnp.concatenate([x] * NS_GROUP, axis=0)

    sc = nt_dot(qg, kcmp_ref[0, 0].astype(bf16))
    sc = sc + jnp.concatenate([bcmp_ref[g, 0] for g in range(NS_GROUP)], axis=0)
    okc = tile4((t_q - (col * CMP_STRIDE + CMP_LEN - 1)) >= 0)
    mc = jnp.max(jnp.where(okc, sc, NEG_BIG), axis=-1, keepdims=True)
    ec = jnp.where(okc, jnp.exp(sc - mc), 0.0)
    pc = ec / jnp.maximum(jnp.sum(ec, axis=-1, keepdims=True), 1e-30)
    o_cmp = jnp.dot(pc.astype(bf16), vcmp_ref[0, 0].astype(bf16), preferred_element_type=f32)

    pc_sum = pc[0:TQ] + pc[TQ:2 * TQ] + pc[2 * TQ:3 * TQ] + pc[3 * TQ:4 * TQ]
    ns = n_tiles * (TQ // SEL_BLOCK)
    nbp = -(-ns // 8) * 8
    jrow = lax.broadcasted_iota(jnp.int32, (nbp, TQ), 0)
    lcol = lax.broadcasted_iota(jnp.int32, (nbp, TQ), 1)
    overlap_t = ((lcol * CMP_STRIDE < (jrow + 1) * SEL_BLOCK) & (lcol * CMP_STRIDE + CMP_LEN > jrow * SEL_BLOCK)).astype(f32)
    ps = lax.dot_general(overlap_t, pc_sum, (((1,), (1,)), ((), ())), preferred_element_type=f32,
                         precision=lax.Precision.HIGHEST)
    cur = (qi * TQ + lcol) // SEL_BLOCK
    valid = (jrow <= cur) & (jrow < ns)
    forced = (jrow == 0) | (jrow == cur) | (jrow == cur - 1)
    score = jnp.where(valid, jnp.where(forced, FORCE_SCORE, ps), -jnp.inf)
    rank = jnp.zeros((nbp, TQ), jnp.int32)
    for i in range(ns):
        ri = score[i:i + 1, :]
        beats = (ri > score) | ((ri == score) & (jrow > i))
        rank = rank + beats.astype(jnp.int32)
    sel_t = (valid & (rank < min(SEL_TOPK, ns))).astype(bf16)
    for kj in range(n_tiles):
        expand = (jrow == (kj * (TQ // SEL_BLOCK) + lcol // SEL_BLOCK)).astype(bf16)
        mask_ref[kj] = lax.dot_general(sel_t, expand, (((0,), (0,)), ((), ())), preferred_element_type=f32)

    def attend(k_ref, v_ref, lo, kind):
        m_ref[...] = jnp.full((R, NS_HD), NEG_BIG, f32)
        l_ref[...] = jnp.zeros((R, NS_HD), f32)
        acc_ref[...] = jnp.zeros((R, NS_HD), f32)

        def step(pj, carry):
            kj = 2 * pj
            off = pl.multiple_of(kj * TQ, 2 * TQ)
            kt = k_ref[0, pl.ds(off, 2 * TQ), :].astype(bf16)
            vt = v_ref[0, pl.ds(off, 2 * TQ), :].astype(bf16)
            delta = qi - kj
            delta1 = jnp.maximum(delta - 1, 0)
            s = nt_dot(qg, kt)
            s = s + jnp.concatenate(
                [jnp.concatenate([bsel_ref[g, delta], bsel_ref[g, delta1]], axis=1) for g in range(NS_GROUP)], axis=0)
            d0 = delta * TQ + row - col
            d = jnp.concatenate([d0, d0 - TQ], axis=1)
            if kind == "sel":
                ok = (d >= 0) & (jnp.concatenate([mask_ref[kj], mask_ref[kj + 1]], axis=1) > 0.5)
            else:
                ok = (d >= 0) & (d <= WINDOW)
            ok = tile4(ok)
            m_old = m_ref[...]
            m_new = jnp.maximum(m_old, jnp.max(jnp.where(ok, s, NEG_BIG), axis=-1, keepdims=True))
            e = jnp.where(ok, jnp.exp(s - jnp.concatenate([m_new, m_new], axis=1)), 0.0)
            scale = jnp.exp(m_old - m_new)
            l_ref[...] = l_ref[...] * scale + jnp.sum(e, axis=-1, keepdims=True)
            acc_ref[...] = acc_ref[...] * scale + jnp.dot(e.astype(bf16), vt, preferred_element_type=f32)
            m_ref[...] = m_new
            return carry

        lax.fori_loop(lo // 2, qi // 2 + 1, step, 0)
        return acc_ref[...] / jnp.maximum(l_ref[...], 1e-30)

    o_sel = attend(ksel_ref, vsel_ref, 0, "sel")
    o_win = attend(kwin_ref, vwin_ref, jnp.maximum(qi - WINDOW // TQ, 0), "win")

    gates = jax.nn.sigmoid(gate_ref[0, 0].astype(f32))
    outs = []
    for g in range(NS_GROUP):
        sl = slice(g * TQ, (g + 1) * TQ)
        og = (gates[:, g:g + 1] * o_cmp[sl] + gates[:, NS_GROUP + g:NS_GROUP + g + 1] * o_sel[sl]
              + gates[:, 2 * NS_GROUP + g:2 * NS_GROUP + g + 1] * o_win[sl])
        outs.append(og)
    o_ref[0] = jnp.concatenate(outs, axis=-1)


def bias_lookup(rel_bias, d):
    onehot = jax.nn.one_hot(rel_bucket(d), REL_BUCKETS, dtype=jnp.float32)
    return jnp.dot(onehot, rel_bias.astype(jnp.float32), precision=lax.Precision.HIGHEST)


def nsa_bias_tiles(rel_bias, n_tiles):
    iq = jnp.arange(TQ)[:, None]
    ik = jnp.arange(TQ)[None, :]
    dl = jnp.arange(n_tiles)[:, None, None]
    bsel = jnp.transpose(bias_lookup(rel_bias, dl * TQ + iq - ik), (3, 0, 1, 2))
    bcmp = jnp.transpose(bias_lookup(rel_bias, dl * TQ + iq - (ik * CMP_STRIDE + CMP_LEN - 1)), (3, 0, 1, 2))
    return bsel, bcmp


def _cmp_prompt_body(k_ref, v_ref, k1_ref, k2_ref, v1_ref, v2_ref, kc_ref, vc_ref):
    f32, bf16 = jnp.float32, jnp.bfloat16
    n_sub = k_ref.shape[1] // CMP_STRIDE
    half = CMP_STRIDE * NS_HD
    for x_ref, w1_ref, w2_ref, o_ref in ((k_ref, k1_ref, k2_ref, kc_ref), (v_ref, v1_ref, v2_ref, vc_ref)):
        a = jnp.zeros((n_sub, CMP_HIDDEN), f32)
        b = jnp.zeros((n_sub, CMP_HIDDEN), f32)
        for p in range(CMP_STRIDE):
            x = x_ref[0, pl.ds(p, n_sub, stride=CMP_STRIDE), :].astype(bf16)
            a = a + jnp.dot(x, w1_ref[p * NS_HD:(p + 1) * NS_HD, :].astype(bf16), preferred_element_type=f32)
            b = b + jnp.dot(x, w1_ref[half + p * NS_HD:half + (p + 1) * NS_HD, :].astype(bf16), preferred_element_type=f32)
        h = gelu_erf(a + jnp.concatenate([b[1:], b[:1]], axis=0))
        o_ref[0, 0] = jnp.dot(h.astype(bf16), w2_ref[...].astype(bf16), preferred_element_type=f32)


def nsa_compress_prompt(pn, phi_k1, phi_k2, phi_v1, phi_v2):
    B, T, _ = pn.shape
    n_sub = T // CMP_STRIDE
    kv0 = NS_W // NS_HD
    w1 = pl.BlockSpec((CMP_LEN * NS_HD, CMP_HIDDEN), lambda b, k: (0, 0))
    w2 = pl.BlockSpec((CMP_HIDDEN, NS_HD), lambda b, k: (0, 0))
    out = pl.BlockSpec((1, 1, n_sub, NS_HD), lambda b, k: (b, k, 0, 0))
    return pl.pallas_call(
        _cmp_prompt_body,
        grid=(B, NS_KV),
        in_specs=[pl.BlockSpec((1, T, NS_HD), lambda b, k: (b, 0, kv0 + k)),
                  pl.BlockSpec((1, T, NS_HD), lambda b, k: (b, 0, kv0 + NS_KV + k)),
                  w1, w2, w1, w2],
        out_specs=[out, out],
        out_shape=[jax.ShapeDtypeStruct((B, NS_KV, n_sub, NS_HD), jnp.float32)] * 2,
        compiler_params=pltpu.CompilerParams(dimension_semantics=("parallel", "parallel"),
                                             vmem_limit_bytes=V7X_VMEM_LIMIT_BYTES),
        name="nsa_compress_prompt",
    )(pn, pn, phi_k1, phi_k2, phi_v1, phi_v2)


def nsa_prompt_attention(pn, kc, vc, bsel, bcmp):
    B, T, _ = pn.shape
    n_tiles = T // TQ
    assert T % (2 * TQ) == 0 and kc.shape[2] == TQ
    glog = pn[..., NS_W + 6 * NS_KVW:].reshape(B, T, 3, NS_KV, NS_GROUP)
    glog = jnp.transpose(glog, (0, 3, 1, 2, 4)).reshape(B, NS_KV, T, 3 * NS_GROUP)
    kv0 = NS_W // NS_HD

    def kv_spec(slot):
        return pl.BlockSpec((1, T, NS_HD), lambda b, k, i, s=slot: (b, 0, kv0 + s * NS_KV + k))

    R = NS_GROUP * TQ
    return pl.pallas_call(
        functools.partial(_nsa_prompt_body, n_tiles=n_tiles),
        grid=(B, NS_KV, n_tiles),
        in_specs=[
            pl.BlockSpec((1, TQ, NS_GROUP * NS_HD), lambda b, k, i: (b, i, k)),
            kv_spec(2), kv_spec(3), kv_spec(4), kv_spec(5),
            pl.BlockSpec((1, 1, TQ, 3 * NS_GROUP), lambda b, k, i: (b, k, i, 0)),
            pl.BlockSpec((1, 1, TQ, NS_HD), lambda b, k, i: (b, k, 0, 0)),
            pl.BlockSpec((1, 1, TQ, NS_HD), lambda b, k, i: (b, k, 0, 0)),
            pl.BlockSpec((NS_GROUP, n_tiles, TQ, TQ), lambda b, k, i: (k, 0, 0, 0)),
            pl.BlockSpec((NS_GROUP, 1, TQ, TQ), lambda b, k, i: (k, i, 0, 0)),
        ],
        out_specs=pl.BlockSpec((1, TQ, NS_GROUP * NS_HD), lambda b, k, i: (b, i, k)),
        out_shape=jax.ShapeDtypeStruct((B, T, NS_W), jnp.float32),
        scratch_shapes=[pltpu.VMEM((R, NS_HD), jnp.float32), pltpu.VMEM((R, NS_HD), jnp.float32),
                        pltpu.VMEM((R, NS_HD), jnp.float32), pltpu.VMEM((n_tiles, TQ, TQ), jnp.float32)],
        compiler_params=pltpu.CompilerParams(
            dimension_semantics=("parallel", "parallel", "arbitrary"),
            vmem_limit_bytes=V7X_VMEM_LIMIT_BYTES),
        name="nsa_prompt_attention",
    )(pn, pn, pn, pn, pn, glog, kc, vc, bsel, bcmp)


def nsa_prompt(p, phi_k1, phi_k2, phi_v1, phi_v2, bsel, bcmp):
    B, T, _ = p.shape
    kv = p[..., NS_W:NS_W + 6 * NS_KVW].reshape(B, T, 6, NS_KV, NS_HD)
    kc, vc = nsa_compress_prompt(p, phi_k1, phi_k2, phi_v1, phi_v2)
    o = nsa_prompt_attention(p, kc, vc, bsel, bcmp)
    wl = min(WINDOW, T)
    return o, kv[:, :, :4], kv[:, T - wl:, 4:]


CMP_PAGE_GROUP = 8
ROW_VECS = 4 * NS_KV
PAGE_ROWS = PAGE_SIZE * ROW_VECS


def _nt_dot(a, b):
    return lax.dot_general(a, b, (((1,), (1,)), ((), ())), preferred_element_type=jnp.float32)


def _cmp_pages_body(pt_ref, page_ref, k1_ref, v1_ref, ab_ref, seq_ref):
    bf16 = jnp.bfloat16
    pg = pl.program_id(1)
    slot_in_group = pg % CMP_PAGE_GROUP
    row0 = pl.multiple_of(slot_in_group * PAGE_ROWS, PAGE_ROWS)
    seq_ref[pl.ds(row0, PAGE_ROWS), :] = page_ref[...]

    @pl.when(slot_in_group == CMP_PAGE_GROUP - 1)
    def _():
        n_sub = CMP_PAGE_GROUP * PAGE_SIZE // CMP_STRIDE
        half = CMP_STRIDE * NS_HD
        for c in range(2 * NS_KV):
            w_ref = k1_ref if c < NS_KV else v1_ref
            a = jnp.zeros((n_sub, CMP_HIDDEN), jnp.float32)
            b = jnp.zeros((n_sub, CMP_HIDDEN), jnp.float32)
            for p in range(CMP_STRIDE):
                x = seq_ref[pl.ds(p * ROW_VECS + c, n_sub, stride=CMP_STRIDE * ROW_VECS), :].astype(bf16)
                a = a + jnp.dot(x, w_ref[p * NS_HD:(p + 1) * NS_HD, :].astype(bf16), preferred_element_type=jnp.float32)
                b = b + jnp.dot(x, w_ref[half + p * NS_HD:half + (p + 1) * NS_HD, :].astype(bf16), preferred_element_type=jnp.float32)
            ab_ref[0, c] = jnp.concatenate([a, b], axis=-1)


def cache_rows(cache_kv):
    return cache_kv.reshape(-1, NS_HD)


def nsa_compress_pages(rows, layer, n_phys, page_table, phi_k1, phi_v1):
    DB, n_pages = page_table.shape
    assert n_pages % CMP_PAGE_GROUP == 0
    subs_per_group = CMP_PAGE_GROUP * PAGE_SIZE // CMP_STRIDE
    n_sub = n_pages * PAGE_SIZE // CMP_STRIDE
    grid_spec = pltpu.PrefetchScalarGridSpec(
        num_scalar_prefetch=1,
        grid=(DB, n_pages),
        in_specs=[pl.BlockSpec((PAGE_ROWS, NS_HD), lambda b, g, pt: (layer * n_phys + pt[b, g], 0)),
                  pl.BlockSpec((CMP_LEN * NS_HD, CMP_HIDDEN), lambda b, g, pt: (0, 0)),
                  pl.BlockSpec((CMP_LEN * NS_HD, CMP_HIDDEN), lambda b, g, pt: (0, 0))],
        out_specs=pl.BlockSpec((1, 2 * NS_KV, subs_per_group, 2 * CMP_HIDDEN),
                               lambda b, g, pt: (b, 0, g // CMP_PAGE_GROUP, 0)),
        scratch_shapes=[pltpu.VMEM((CMP_PAGE_GROUP * PAGE_ROWS, NS_HD), jnp.float32)])
    return pl.pallas_call(
        _cmp_pages_body, grid_spec=grid_spec,
        out_shape=jax.ShapeDtypeStruct((DB, 2 * NS_KV, n_sub, 2 * CMP_HIDDEN), jnp.float32),
        compiler_params=pltpu.CompilerParams(dimension_semantics=("parallel", "arbitrary"),
                                             vmem_limit_bytes=V7X_VMEM_LIMIT_BYTES),
        name="nsa_compress_pages",
    )(page_table, rows, phi_k1, phi_v1)


def _nsa_sample_body(pt_ref, q_ref, page_ref, knew_ref, vnew_ref, win_ref, wnew_ref, gate_ref,
                     ab_ref, k2_ref, v2_ref, bsel_ref, bwin_ref, bcmp_ref, o_ref,
                     m_ref, l_ref, acc_ref, sel_ref, ocmp_ref, *, n_pages, n_new):
    f32, bf16 = jnp.float32, jnp.bfloat16
    pg = pl.program_id(1)
    QG = NS_GROUP * n_new
    R = NS_KV * QG
    n_sub = n_pages * PAGE_SIZE // CMP_STRIDE
    n_cmp = n_sub - 1
    ns = n_pages * (PAGE_SIZE // SEL_BLOCK) + 1
    NSP = sel_ref.shape[1]
    past = n_pages * PAGE_SIZE
    qs = (q_ref[0] * NS_HD ** -0.5).astype(bf16)

    @pl.when(pg == 0)
    def _():
        m_ref[...] = jnp.full((R, NS_HD), NEG_BIG, f32)
        l_ref[...] = jnp.zeros((R, NS_HD), f32)
        acc_ref[...] = jnp.zeros((R, NS_HD), f32)
        ncol = lax.broadcasted_iota(jnp.int32, (QG, n_sub), 1)
        okc = ncol < n_cmp
        orow = lax.broadcasted_iota(jnp.int32, (n_sub, NSP), 0)
        ocol = lax.broadcasted_iota(jnp.int32, (n_sub, NSP), 1)
        overlap = ((orow * CMP_STRIDE < (ocol + 1) * SEL_BLOCK) & (orow * CMP_STRIDE + CMP_LEN > ocol * SEL_BLOCK)
                   & (orow < n_cmp)).astype(f32)
        jcol = lax.broadcasted_iota(jnp.int32, (n_new, NSP), 1)
        qrow = lax.broadcasted_iota(jnp.int32, (n_new, NSP), 0)
        cur = (past + qrow) // SEL_BLOCK
        valid = (jcol <= cur) & (jcol < ns)
        forced = (jcol == 0) | (jcol == cur) | (jcol == cur - 1)
        for kv in range(NS_KV):
            def cmp_of(c, w2_ref):
                a = ab_ref[0, c, :, 0:CMP_HIDDEN]
                b = ab_ref[0, c, :, CMP_HIDDEN:2 * CMP_HIDDEN]
                b = jnp.concatenate([b[1:], b[:1]], axis=0)
                h = gelu_erf(a + b)
                return jnp.dot(h.astype(bf16), w2_ref[...].astype(bf16), preferred_element_type=f32)
            kc = cmp_of(kv, k2_ref)
            vc = cmp_of(NS_KV + kv, v2_ref)
            sc = _nt_dot(qs[kv * QG:(kv + 1) * QG], kc.astype(bf16)) + bcmp_ref[kv * QG:(kv + 1) * QG, :]
            mc = jnp.max(jnp.where(okc, sc, NEG_BIG), axis=-1, keepdims=True)
            ec = jnp.where(okc, jnp.exp(sc - mc), 0.0)
            pc = ec / jnp.maximum(jnp.sum(ec, axis=-1, keepdims=True), 1e-30)
            ocmp_ref[kv * QG:(kv + 1) * QG, :] = jnp.dot(pc.astype(bf16), vc.astype(bf16), preferred_element_type=f32)
            pc_sum = pc[0:n_new]
            for g in range(1, NS_GROUP):
                pc_sum = pc_sum + pc[g * n_new:(g + 1) * n_new]
            ps = jnp.dot(pc_sum, overlap, preferred_element_type=f32, precision=lax.Precision.HIGHEST)
            score = jnp.where(valid, jnp.where(forced, FORCE_SCORE, ps), -jnp.inf)
            rank = jnp.zeros((n_new, NSP), jnp.int32)
            for i in range(ns):
                ci = jnp.broadcast_to(score[:, i:i + 1], (n_new, NSP))
                rank = rank + ((ci > score) | ((ci == score) & (jcol > i))).astype(jnp.int32)
            sel = (valid & (rank < min(SEL_TOPK, ns))).astype(f32)
            sel_ref[kv * QG:(kv + 1) * QG, :] = jnp.concatenate([sel] * NS_GROUP, axis=0)

    is_new = pg == n_pages
    krow = lax.broadcasted_iota(jnp.int32, (NSP, PAGE_SIZE), 0)
    kcol = lax.broadcasted_iota(jnp.int32, (NSP, PAGE_SIZE), 1)
    expand = (krow == pg * (PAGE_SIZE // SEL_BLOCK) + kcol // SEL_BLOCK).astype(bf16)
    inblock = jnp.dot(sel_ref[...].astype(bf16), expand, preferred_element_type=f32) > 0.5
    rr = lax.broadcasted_iota(jnp.int32, (R, PAGE_SIZE), 0)
    cc = lax.broadcasted_iota(jnp.int32, (R, PAGE_SIZE), 1)
    causal_new = (cc <= rr % n_new) & (cc < n_new)
    ok = inblock & (jnp.logical_not(is_new) | causal_new)
    def page_vec(slot, kv, new_ref):
        old = page_ref[pl.ds(slot * NS_KV + kv, PAGE_SIZE, stride=ROW_VECS), :]
        return jnp.where(is_new, new_ref[0, :, kv * NS_HD:(kv + 1) * NS_HD], old).astype(bf16)

    s = jnp.concatenate([_nt_dot(qs[kv * QG:(kv + 1) * QG], page_vec(2, kv, knew_ref))
                         for kv in range(NS_KV)], axis=0) + bsel_ref[...]
    m_old = m_ref[...]
    m_new = jnp.maximum(m_old, jnp.max(jnp.where(ok, s, NEG_BIG), axis=-1, keepdims=True))
    e = jnp.where(ok, jnp.exp(s - m_new), 0.0)
    scale = jnp.exp(m_old - m_new)
    l_ref[...] = l_ref[...] * scale + jnp.sum(e, axis=-1, keepdims=True)
    pv = jnp.concatenate([jnp.dot(e[kv * QG:(kv + 1) * QG].astype(bf16), page_vec(3, kv, vnew_ref),
                                  preferred_element_type=f32) for kv in range(NS_KV)], axis=0)
    acc_ref[...] = acc_ref[...] * scale + pv
    m_ref[...] = m_new

    @pl.when(is_new)
    def _():
        o_sel = acc_ref[...] / jnp.maximum(l_ref[...], 1e-30)
        Wb = win_ref.shape[1]
        wc = lax.broadcasted_iota(jnp.int32, (R, Wb + PAGE_SIZE), 1)
        wr = lax.broadcasted_iota(jnp.int32, (R, Wb + PAGE_SIZE), 0) % n_new
        dw = (Wb + wr) - wc
        okw = (dw >= 0) & (dw <= WINDOW) & ((wc < Wb) | (wc - Wb < n_new))
        sw = []
        for kv in range(NS_KV):
            kw = jnp.concatenate([win_ref[0, :, kv * NS_HD:(kv + 1) * NS_HD],
                                  wnew_ref[0, :, kv * NS_HD:(kv + 1) * NS_HD]], axis=0).astype(bf16)
            sw.append(_nt_dot(qs[kv * QG:(kv + 1) * QG], kw))
        sw = jnp.concatenate(sw, axis=0) + bwin_ref[...]
        mw = jnp.max(jnp.where(okw, sw, NEG_BIG), axis=-1, keepdims=True)
        ew = jnp.where(okw, jnp.exp(sw - mw), 0.0)
        pw = (ew / jnp.maximum(jnp.sum(ew, axis=-1, keepdims=True), 1e-30)).astype(bf16)
        o_win = []
        for kv in range(NS_KV):
            vw = jnp.concatenate([win_ref[0, :, NS_KVW + kv * NS_HD:NS_KVW + (kv + 1) * NS_HD],
                                  wnew_ref[0, :, NS_KVW + kv * NS_HD:NS_KVW + (kv + 1) * NS_HD]], axis=0).astype(bf16)
            o_win.append(jnp.dot(pw[kv * QG:(kv + 1) * QG], vw, preferred_element_type=f32))
        o_win = jnp.concatenate(o_win, axis=0)
        gates = jax.nn.sigmoid(gate_ref[0].astype(f32))
        o = gates[:, 0:1] * ocmp_ref[...] + gates[:, 1:2] * o_sel + gates[:, 2:3] * o_win
        o_ref[0] = jnp.concatenate([o[h * n_new:(h + 1) * n_new] for h in range(NS_HEADS)], axis=-1)


def nsa_sample_bias(rel_bias, n_new, past, wb):
    n_sub = past // CMP_STRIDE
    tq = past + jnp.arange(n_new)[:, None]
    pos = jnp.arange(past + PAGE_SIZE)[None, :]
    R = NS_HEADS * n_new
    bsel = jnp.transpose(bias_lookup(rel_bias, tq - pos), (2, 0, 1)).reshape(R, past + PAGE_SIZE)
    cpos = (jnp.arange(n_sub) * CMP_STRIDE + CMP_LEN - 1)[None, :]
    bcmp = jnp.transpose(bias_lookup(rel_bias, tq - cpos), (2, 0, 1)).reshape(R, n_sub)
    return bsel, bsel[:, past - wb:], bcmp


def nsa_sample_attention(ps, rows, layer, n_phys, page_table, win_buf, ab, phi_k2, phi_v2, bias):
    bsel, bwin, bcmp = bias
    DB, Tn, _ = ps.shape
    n_pages = page_table.shape[1]
    Wb = win_buf.shape[1]
    R = NS_HEADS * Tn
    ns = n_pages * (PAGE_SIZE // SEL_BLOCK) + 1
    NSP = -(-ns // 128) * 128
    n_sub = n_pages * PAGE_SIZE // CMP_STRIDE
    q = jnp.transpose(ps[..., :NS_W].reshape(DB, Tn, NS_HEADS, NS_HD), (0, 2, 1, 3)).reshape(DB, R, NS_HD)
    kvn = ps[..., NS_W:NS_W + 6 * NS_KVW].reshape(DB, Tn, 6, NS_KVW)
    padn = ((0, 0), (0, PAGE_SIZE - Tn), (0, 0))
    knew = jnp.pad(kvn[:, :, 2], padn)
    vnew = jnp.pad(kvn[:, :, 3], padn)
    wnew = jnp.pad(jnp.concatenate([kvn[:, :, 4], kvn[:, :, 5]], axis=-1), padn)
    win2 = win_buf.reshape(DB, Wb, 2 * NS_KVW)
    glog = jnp.transpose(ps[..., NS_W + 6 * NS_KVW:].reshape(DB, Tn, 3, NS_HEADS), (0, 3, 1, 2)).reshape(DB, R, 3)
    last = n_pages - 1
    grid_spec = pltpu.PrefetchScalarGridSpec(
        num_scalar_prefetch=1,
        grid=(DB, n_pages + 1),
        in_specs=[
            pl.BlockSpec((1, R, NS_HD), lambda b, g, pt: (b, 0, 0)),
            pl.BlockSpec((PAGE_ROWS, NS_HD), lambda b, g, pt: (layer * n_phys + pt[b, jnp.minimum(g, last)], 0)),
            pl.BlockSpec((1, PAGE_SIZE, NS_KVW), lambda b, g, pt: (b, 0, 0)),
            pl.BlockSpec((1, PAGE_SIZE, NS_KVW), lambda b, g, pt: (b, 0, 0)),
            pl.BlockSpec((1, Wb, 2 * NS_KVW), lambda b, g, pt: (b, 0, 0)),
            pl.BlockSpec((1, PAGE_SIZE, 2 * NS_KVW), lambda b, g, pt: (b, 0, 0)),
            pl.BlockSpec((1, R, 3), lambda b, g, pt: (b, 0, 0)),
            pl.BlockSpec((1, 2 * NS_KV, n_sub, 2 * CMP_HIDDEN), lambda b, g, pt: (b, 0, 0, 0)),
            pl.BlockSpec((CMP_HIDDEN, NS_HD), lambda b, g, pt: (0, 0)),
            pl.BlockSpec((CMP_HIDDEN, NS_HD), lambda b, g, pt: (0, 0)),
            pl.BlockSpec((R, PAGE_SIZE), lambda b, g, pt: (0, g)),
            pl.BlockSpec((R, Wb + PAGE_SIZE), lambda b, g, pt: (0, 0)),
            pl.BlockSpec((R, n_sub), lambda b, g, pt: (0, 0)),
        ],
        out_specs=pl.BlockSpec((1, Tn, NS_W), lambda b, g, pt: (b, 0, 0)),
        scratch_shapes=[pltpu.VMEM((R, NS_HD), jnp.float32), pltpu.VMEM((R, NS_HD), jnp.float32),
                        pltpu.VMEM((R, NS_HD), jnp.float32), pltpu.VMEM((R, NSP), jnp.float32),
                        pltpu.VMEM((R, NS_HD), jnp.float32)])
    return pl.pallas_call(
        functools.partial(_nsa_sample_body, n_pages=n_pages, n_new=Tn),
        grid_spec=grid_spec,
        out_shape=jax.ShapeDtypeStruct((DB, Tn, NS_W), jnp.float32),
        compiler_params=pltpu.CompilerParams(dimension_semantics=("parallel", "arbitrary"),
                                             vmem_limit_bytes=V7X_VMEM_LIMIT_BYTES),
        name="nsa_sample_attention",
    )(page_table, q, rows, knew, vnew, win2, wnew, glog, ab, phi_k2, phi_v2, bsel, bwin, bcmp)


def nsa_sample(p, rows, layer, n_phys, page_table, win_buf, phi_k1, phi_k2, phi_v1, phi_v2, bias):
    DB, Tn, _ = p.shape
    assert Tn < CMP_STRIDE and Tn <= SEL_BLOCK
    kv = p[..., NS_W:NS_W + 6 * NS_KVW].reshape(DB, Tn, 6, NS_KV, NS_HD)
    ab = nsa_compress_pages(rows, layer, n_phys, page_table, phi_k1, phi_v1)
    o = nsa_sample_attention(p, rows, layer, n_phys, page_table, win_buf, ab, phi_k2, phi_v2, bias)
    win = jnp.concatenate([win_buf, kv[:, :, 4:].astype(win_buf.dtype)], axis=1)
    return o, kv[:, :, :4], win[:, Tn:]


PEER_ROUTE_TM = 128
PEER_ROUTE_UNROLL = 4
PEER_ROUTE_GROUP = 16


def gelu_erf(x):
    return 0.5 * x * (1.0 + lax.erf(x * (2.0 ** -0.5)))


def _top_rows(work, n_rows, k, row_iota):
    vals, idxs = [], []
    for _ in range(k):
        m = jnp.max(work, axis=0, keepdims=True)
        idx = jnp.min(jnp.where(work == m, row_iota, n_rows), axis=0, keepdims=True)
        vals.append(m)
        idxs.append(idx)
        work = jnp.where(row_iota == idx, -jnp.inf, work)
    return vals, idxs


def _peer_route_body(q_ref, k1_ref, k2_ref, g_ref, i1_s, i2_s, w_s, gt_s):
    f32, bf16 = jnp.float32, jnp.bfloat16
    tm = q_ref.shape[0]
    half = PEER_DKEY // 2
    K = PEER_TOPK
    rows = lax.broadcasted_iota(jnp.int32, (PEER_NKEYS, tm), 0)
    n_cand = K + (K // 2 - 1) * (K // 2) + K // 2
    crow = lax.broadcasted_iota(jnp.int32, (n_cand, tm), 0)
    k1 = k1_ref[...].astype(bf16)
    k2 = k2_ref[...].astype(bf16)

    def nt_dot(a, b):
        return lax.dot_general(a, b, (((1,), (1,)), ((), ())), preferred_element_type=f32)

    for h in range(PEER_HEADS):
        q1 = q_ref[:, h * PEER_DKEY:h * PEER_DKEY + half].astype(bf16)
        q2 = q_ref[:, h * PEER_DKEY + half:(h + 1) * PEER_DKEY].astype(bf16)
        v1, i1 = _top_rows(nt_dot(k1, q1), PEER_NKEYS, K, rows)
        v2, i2 = _top_rows(nt_dot(k2, q2), PEER_NKEYS, K, rows)
        v2m = jnp.concatenate(v2, axis=0)
        i2m = jnp.concatenate(i2, axis=0)
        v1m = jnp.concatenate(v1, axis=0)
        i1m = jnp.concatenate(i1, axis=0)
        hk = K // 2
        cand = jnp.concatenate([v1[0] + v2m] + [v1[a] + v2m[0:hk] for a in range(1, hk)] + [v1m[hk:K] + v2[0]], axis=0)
        cidx = jnp.concatenate([i1[0] * PEER_NKEYS + i2m] + [i1[a] * PEER_NKEYS + i2m[0:hk] for a in range(1, hk)]
                               + [i1m[hk:K] * PEER_NKEYS + i2[0]], axis=0)
        sv, pos = _top_rows(cand, n_cand, K, crow)
        eidx = [jnp.max(jnp.where(crow == pos[k], cidx, 0), axis=0, keepdims=True) for k in range(K)]
        svm = jnp.concatenate(sv, axis=0)
        em = jnp.concatenate(eidx, axis=0)
        e = jnp.exp(svm - svm[0:1])
        gw = e / jnp.sum(e, axis=0, keepdims=True)
        i1_s[h * K:(h + 1) * K, :] = (em // PEER_NKEYS).astype(f32)
        i2_s[h * K:(h + 1) * K, :] = (em % PEER_NKEYS).astype(f32)
        w_s[h * K:(h + 1) * K, :] = gw
    i1_s[...] = i1_s[...].T
    i2_s[...] = i2_s[...].T
    w_s[...] = w_s[...].T
    sub = lax.broadcasted_iota(jnp.int32, (PEER_NKEYS, PEER_HEADS * K), 0).astype(f32)

    def token_group(tg, carry):
        base = pl.multiple_of(tg * PEER_ROUTE_GROUP, PEER_ROUTE_GROUP)
        for part in range(PEER_ROUTE_GROUP // PEER_ROUTE_UNROLL):
            us = [part * PEER_ROUTE_UNROLL + u for u in range(PEER_ROUTE_UNROLL)]
            a_w = [jnp.where(sub == i1_s[pl.ds(base + u, 1), :], w_s[pl.ds(base + u, 1), :], 0.0).astype(bf16) for u in us]
            b_1 = [jnp.where(sub == i2_s[pl.ds(base + u, 1), :], 1.0, 0.0).astype(bf16) for u in us]
            g = [nt_dot(a, b) for a, b in zip(a_w, b_1)]
            for u, gt in zip(us, g):
                gt_s[u * PEER_NKEYS:(u + 1) * PEER_NKEYS, :] = gt
        for c in range(PEER_NKEYS):
            g_ref[pl.ds(base, PEER_ROUTE_GROUP), c * PEER_NKEYS:(c + 1) * PEER_NKEYS] = (
                gt_s[pl.ds(c, PEER_ROUTE_GROUP, stride=PEER_NKEYS), :].astype(g_ref.dtype))
        return carry

    lax.fori_loop(0, tm // PEER_ROUTE_GROUP, token_group, 0)


def peer_route(q, k1, k2):
    n = q.shape[0]
    tm = PEER_ROUTE_TM
    S = PEER_HEADS * PEER_TOPK
    assert n % tm == 0 and S == tm
    return pl.pallas_call(
        _peer_route_body,
        grid=(n // tm,),
        in_specs=[pl.BlockSpec((tm, PEER_HEADS * PEER_DKEY), lambda i: (i, 0)),
                  pl.BlockSpec((PEER_NKEYS, PEER_DKEY // 2), lambda i: (0, 0)),
                  pl.BlockSpec((PEER_NKEYS, PEER_DKEY // 2), lambda i: (0, 0))],
        out_specs=pl.BlockSpec((tm, PEER_EXPERTS), lambda i: (i, 0)),
        out_shape=jax.ShapeDtypeStruct((n, PEER_EXPERTS), jnp.bfloat16),
        scratch_shapes=[pltpu.VMEM((S, tm), jnp.float32)] * 3
        + [pltpu.VMEM((PEER_ROUTE_GROUP * PEER_NKEYS, PEER_NKEYS), jnp.float32)],
        compiler_params=pltpu.CompilerParams(dimension_semantics=("parallel",),
                                             vmem_limit_bytes=V7X_VMEM_LIMIT_BYTES),
        name="peer_route",
    )(q, k1, k2)


def _peer_expert_body(x_ref, g_ref, u_ref, v_ref, o_ref):
    f32, bf16 = jnp.float32, jnp.bfloat16
    e = pl.program_id(1)
    h = lax.dot_general(x_ref[...], u_ref[...], (((1,), (1,)), ((), ())), preferred_element_type=f32)
    p = (g_ref[...].astype(f32) * gelu_erf(h)).astype(bf16)
    upd = jnp.dot(p, v_ref[...], preferred_element_type=f32)

    @pl.when(e == 0)
    def _():
        o_ref[...] = upd

    @pl.when(e > 0)
    def _():
        o_ref[...] += upd


def peer_experts(x, g, u, v, layer, tm=512, te=512):
    n, D = x.shape
    E = u.shape[1]
    tm = min(tm, n)
    assert n % tm == 0 and E % te == 0
    return pl.pallas_call(
        _peer_expert_body,
        grid=(n // tm, E // te),
        in_specs=[pl.BlockSpec((tm, D), lambda i, e: (i, 0)),
                  pl.BlockSpec((tm, te), lambda i, e: (i, e)),
                  pl.BlockSpec((None, te, D), lambda i, e: (layer, e, 0)),
                  pl.BlockSpec((None, te, D), lambda i, e: (layer, e, 0))],
        out_specs=pl.BlockSpec((tm, D), lambda i, e: (i, 0)),
        out_shape=jax.ShapeDtypeStruct((n, D), jnp.float32),
        compiler_params=pltpu.CompilerParams(dimension_semantics=("parallel", "arbitrary"),
                                             vmem_limit_bytes=V7X_VMEM_LIMIT_BYTES),
        name="peer_experts",
    )(x, g, u, v)


def peer_ffn(x, layer, wq, k1, k2, u_bf, v_bf):
    Bx, T, D = x.shape
    n = Bx * T
    pad = -n % PEER_ROUTE_TM
    xt = jnp.pad(x.reshape(n, D), ((0, pad), (0, 0)))
    g = peer_route(matmul(xt, wq, layer), k1, k2)
    out = peer_experts(xt.astype(jnp.bfloat16), g, u_bf, v_bf, layer)
    return out[:n].reshape(Bx, T, D).astype(x.dtype)


def residual_block(x, mix, layer, w_out, ln1_g, ln1_b, ln2_g, ln2_b, peer_wq, peer_k1, peer_k2, peer_u, peer_v):
    x = deepnorm(x, matmul3(mix, w_out, layer), ln1_g, ln1_b)
    return deepnorm(x, peer_ffn(x, layer, peer_wq, peer_k1, peer_k2, peer_u, peer_v), ln2_g, ln2_b)


def kernel(x_prompt, x_sample, cache_kv, cache_win, state_rwkv, state_rwkv_shift, state_gdn, state_gdn_conv, page_table, w_in, w_out, ln1_g, ln1_b, ln2_g, ln2_b, rw_mu, rw_w0, rw_w_up, rw_a0, rw_a_up, rw_g_up, rw_k_k, rw_k_a, rw_r_k, rw_ln_g, rw_ln_b, gd_conv_w, gd_a_log, gd_dt_bias, gd_norm_g, ns_phi_k1, ns_phi_k2, ns_phi_v1, ns_phi_v2, rel_bias, peer_wq, peer_k1, peer_k2, peer_u, peer_v):
    xp, xs = x_prompt, x_sample
    B = xp.shape[0]
    o_b = RW_COLS
    o_c = RW_COLS + GD_COLS
    bsel, bcmp = nsa_bias_tiles(rel_bias, SEQ // TQ)
    sbias = nsa_sample_bias(rel_bias, x_sample.shape[1], page_table.shape[1] * PAGE_SIZE, cache_win.shape[2])
    w_in_bf, w_out_bf, wq_bf, u_bf, v_bf = (w.astype(jnp.bfloat16) for w in (w_in, w_out, peer_wq, peer_u, peer_v))
    kv_rows = cache_rows(cache_kv)
    kv_p, kv_s, win_p, win_s, rw_p, rw_s, sh_p, sh_s, gd_p, gd_s, cv_p, cv_s = ([] for _ in range(12))
    for l in range(DEPTH):
        rw = (rw_mu[l], rw_w0[l], rw_w_up[l], rw_a0[l], rw_a_up[l], rw_g_up[l], rw_k_k[l], rw_k_a[l], rw_r_k[l], rw_ln_g[l], rw_ln_b[l])
        gd = (gd_conv_w[l], gd_a_log[l], gd_dt_bias[l], gd_norm_g[l])
        phi = (ns_phi_k1[l], ns_phi_k2[l], ns_phi_v1[l], ns_phi_v2[l])
        tail = (l, w_out_bf, ln1_g[l], ln1_b[l], ln2_g[l], ln2_b[l], wq_bf, peer_k1[l], peer_k2[l], u_bf, v_bf)
        pp = matmul3(xp, w_in_bf, l)
        a, sh, rs = rwkv7_mixer(pp[..., :o_b], jnp.zeros((B, RW_COLS), pp.dtype), jnp.zeros((B, RW_HEADS, RW_HD, RW_HD), jnp.float32), *rw)
        b, cv, gs = gated_deltanet_mixer(pp[..., o_b:o_c], jnp.zeros((B, GD_CONV - 1, GD_QKV), pp.dtype), jnp.zeros((B, GD_HEADS, GD_HD, GD_HD), jnp.float32), *gd)
        c, kvr, wr = nsa_prompt(pp[..., o_c:], *phi, bsel, bcmp)
        xp = residual_block(xp, jnp.concatenate([a, b, c.astype(a.dtype)], -1), *tail)
        kv_p.append(kvr)
        win_p.append(wr)
        rw_p.append(rs)
        sh_p.append(sh)
        gd_p.append(gs)
        cv_p.append(cv)
        ps = matmul3(xs, w_in_bf, l)
        a, sh, rs = rwkv7_mixer(ps[..., :o_b], state_rwkv_shift[l], state_rwkv[l], *rw)
        b, cv, gs = gated_deltanet_mixer(ps[..., o_b:o_c], state_gdn_conv[l], state_gdn[l], *gd)
        c, kvr, wr = nsa_sample(ps[..., o_c:], kv_rows, l, cache_kv.shape[1], page_table, cache_win[l], *phi, sbias)
        xs = residual_block(xs, jnp.concatenate([a, b, c.astype(a.dtype)], -1), *tail)
        kv_s.append(kvr)
        win_s.append(wr)
        rw_s.append(rs)
        sh_s.append(sh)
        gd_s.append(gs)
        cv_s.append(cv)
    st = jnp.stack
    return (xp, xs, st(kv_p), st(kv_s), st(win_p), st(win_s), st(rw_p), st(rw_s), st(sh_p), st(sh_s), st(gd_p), st(gd_s), st(cv_p), st(cv_s))
```
